```python
import math
import jax, jax.numpy as jnp
from jax import lax
import numpy as np

D_MODEL = 2048
BATCH = 2
SEQ = 4096
DEPTH = 4
DEC_BATCH = 16
DEC_SEQ = 64
PAST_LEN = 4096

CHUNK = 64
MIX_WIDTH = D_MODEL
GROUP_WIDTH = MIX_WIDTH // 4
A_WIDTH = GROUP_WIDTH
A_HEADS = 4
A_HD = A_WIDTH // A_HEADS
GMLP_CHUNK = 128
B_WIDTH = GROUP_WIDTH
S5_CH = 16
B_GROUPS = B_WIDTH // S5_CH
S5_P = 64
C_WIDTH = GROUP_WIDTH
C_HD = 64
C_HEADS = C_WIDTH // C_HD
SB_BLOCK = 128
D_WIDTH = MIX_WIDTH - A_WIDTH - B_WIDTH - C_WIDTH
D_HD = 64
D_HEADS = D_WIDTH // D_HD
D_W_RANK = 64
D_A_RANK = 64
D_G_RANK = 128
D_COLS = 3 * D_WIDTH + D_W_RANK + D_A_RANK + D_G_RANK
IN_COLS = 2 * A_WIDTH + B_WIDTH + 3 * C_WIDTH + D_COLS
D_FF = ((8 * D_MODEL // 3 + 255) // 256) * 256
RMS_EPS = 1e-6
GN_EPS = 64e-5

kernel_name = 'hybrid_streaming_encoder_step'


def _rmsnorm(x, g):
    xf = x.astype(jnp.float32)
    y = xf * lax.rsqrt(jnp.mean(xf * xf, axis=-1, keepdims=True) + RMS_EPS)
    return (y * g.astype(jnp.float32)).astype(x.dtype)


def _macaron_half_ffn(x, ln, wg, wu, wd):
    h = _rmsnorm(x, ln)
    return x + 0.5 * ((jax.nn.silu(h @ wg) * (h @ wu)) @ wd)


def _gmlp_mixer(za, v_norm, ws, b):
    bsz, t, _ = za.shape
    z = jax.nn.gelu(za)
    u, v = jnp.split(z, 2, axis=-1)
    v = _rmsnorm(v.reshape(bsz, t, A_HEADS, A_HD), v_norm)
    L = min(t, GMLP_CHUNK)
    n = t // L
    causal = jnp.tril(jnp.ones((L, L), dtype=bool))
    w = jnp.where(causal[None], ws[:, :L, :L], 0.0)
    s = jnp.einsum('hts,bnshd->bnthd', w, v.reshape(bsz, n, L, A_HEADS, A_HD))
    s = s + jnp.transpose(b[:, :L])[None, None, :, :, None]
    out = u * s.reshape(bsz, t, A_WIDTH)
    return out, v.reshape(bsz, t, A_WIDTH)


def _s5_mixer(zb, h0_re, h0_im, lam_re, lam_im, log_dt, b_re, b_im, c_re, c_im, d_skip, w_glu, b_glu):
    f32 = jnp.float32
    bsz, t, _ = zb.shape
    u = zb.astype(f32).reshape(bsz, t, B_GROUPS, S5_CH)
    lr, li = lam_re.astype(f32), lam_im.astype(f32)
    dt = jnp.exp(log_dt.astype(f32))[:, None]
    mag = jnp.exp(lr * dt)
    ab_re, ab_im = mag * jnp.cos(li * dt), mag * jnp.sin(li * dt)
    den = lr * lr + li * li
    nr, ni = ab_re - 1.0, ab_im
    cf_re = (nr * lr + ni * li) / den
    cf_im = (ni * lr - nr * li) / den
    br, bi = b_re.astype(f32), b_im.astype(f32)
    bb_re = cf_re[..., None] * br - cf_im[..., None] * bi
    bb_im = cf_re[..., None] * bi + cf_im[..., None] * br
    bu_re = jnp.einsum('gpc,btgc->btgp', bb_re, u)
    bu_im = jnp.einsum('gpc,btgc->btgp', bb_im, u)
    if h0_re is not None:
        h0r, h0i = h0_re.astype(f32), h0_im.astype(f32)
        bu_re = bu_re.at[:, 0].add(ab_re * h0r - ab_im * h0i)
        bu_im = bu_im.at[:, 0].add(ab_re * h0i + ab_im * h0r)
    a_re = jnp.broadcast_to(ab_re, bu_re.shape)
    a_im = jnp.broadcast_to(ab_im, bu_im.shape)

    def combine(e1, e2):
        a1r, a1i, b1r, b1i = e1
        a2r, a2i, b2r, b2i = e2
        return (a2r * a1r - a2i * a1i, a2r * a1i + a2i * a1r,
                a2r * b1r - a2i * b1i + b2r, a2r * b1i + a2i * b1r + b2i)

    _, _, h_re, h_im = lax.associative_scan(combine, (a_re, a_im, bu_re, bu_im), axis=1)
    y = (jnp.einsum('gcp,btgp->btgc', c_re.astype(f32), h_re)
         - jnp.einsum('gcp,btgp->btgc', c_im.astype(f32), h_im))
    y = y.reshape(bsz, t, B_WIDTH) + d_skip.astype(f32) * u.reshape(bsz, t, B_WIDTH)
    g = jax.nn.gelu(y)
    out = g * jax.nn.sigmoid(g @ w_glu.astype(f32) + b_glu.astype(f32))
    return out.astype(zb.dtype), h_re[:, -1], h_im[:, -1]


def _sb_block(q, qpos, k, v):
    z = jnp.einsum('bqhd,bkhd->bhqk', q, k).astype(jnp.float32) * (C_HD ** -0.5)
    kpos = jnp.arange(k.shape[1])
    causal = kpos[None, :] < qpos[:, None]
    log_keep = jnp.where(causal, jax.nn.log_sigmoid(-z), 0.0)
    after = lax.cumsum(log_keep, axis=3, reverse=True) - log_keep
    w = jnp.where(causal, jnp.exp(jax.nn.log_sigmoid(z) + after), 0.0)
    return jnp.einsum('bhqk,bkhd->bqhd', w.astype(v.dtype), v)


def _sb_mixer(zc, q_norm, k_norm, past_k, past_v):
    bsz, t, _ = zc.shape
    q, k, v = jnp.split(zc, 3, axis=-1)
    q = _rmsnorm(q.reshape(bsz, t, C_HEADS, C_HD), q_norm)
    k = _rmsnorm(k.reshape(bsz, t, C_HEADS, C_HD), k_norm)
    v = v.reshape(bsz, t, C_HEADS, C_HD)
    if past_k is None:
        k_all, v_all = k, v
    else:
        k_all = jnp.concatenate([past_k.astype(k.dtype), k], axis=1)
        v_all = jnp.concatenate([past_v.astype(v.dtype), v], axis=1)
    offset = k_all.shape[1] - t
    nb = t // SB_BLOCK if t % SB_BLOCK == 0 else 1
    tb = t // nb
    qpos = (offset + jnp.arange(t)).reshape(nb, tb)
    qb = jnp.moveaxis(q.reshape(bsz, nb, tb, C_HEADS, C_HD), 1, 0)
    ob = lax.map(lambda args: _sb_block(args[0], args[1], k_all, v_all), (qb, qpos))
    out = jnp.moveaxis(ob, 0, 1).reshape(bsz, t, C_WIDTH)
    return out, k, v


def _rwkv7_mixer(zd, shift0, wkv0, mu, w0, w2, a0, a2, g2, k_k, k_a, r_k, lnx_w, lnx_b):
    f32 = jnp.float32
    bsz, t, _ = zd.shape
    zf = zd.astype(f32)
    if shift0 is None:
        first = jnp.zeros((bsz, 1, D_COLS), f32)
    else:
        first = shift0.astype(f32)[:, None]
    prev = jnp.concatenate([first, zf[:, :-1]], axis=1)
    zs = zf + (prev - zf) * mu.astype(f32)
    cuts = [D_WIDTH, 2 * D_WIDTH, 3 * D_WIDTH, 3 * D_WIDTH + D_W_RANK, 3 * D_WIDTH + D_W_RANK + D_A_RANK]
    r, k, v, w_lr, a_lr, g_lr = jnp.split(zs, cuts, axis=-1)
    w_log = -jax.nn.softplus(-(w0.astype(f32) + jnp.tanh(w_lr) @ w2.astype(f32))) - 0.5
    decay = jnp.exp(-jnp.exp(w_log))
    a = jax.nn.sigmoid(a0.astype(f32) + a_lr @ a2.astype(f32))
    g = jax.nn.sigmoid(g_lr) @ g2.astype(f32)

    def heads(x_):
        return x_.reshape(bsz, t, D_HEADS, D_HD)

    kk = heads(k * k_k.astype(f32))
    kk = kk / jnp.maximum(jnp.sqrt(jnp.sum(kk * kk, axis=-1, keepdims=True)), 1e-12)
    k = k * (1.0 + (a - 1.0) * k_a.astype(f32))
    r, k, v, decay, a = heads(r), heads(k), heads(v), heads(decay), heads(a)

    def step(S, inp):
        r_t, w_t, k_t, v_t, kk_t, a_t = inp
        sa = jnp.einsum('bhvk,bhk->bhv', S, kk_t)
        S = (S * w_t[:, :, None, :] - sa[..., None] * (kk_t * a_t)[:, :, None, :]
             + v_t[..., None] * k_t[:, :, None, :])
        return S, jnp.einsum('bhvk,bhk->bhv', S, r_t)

    if wkv0 is None:
        S0 = jnp.zeros((bsz, D_HEADS, D_HD, D_HD), f32)
    else:
        S0 = wkv0.astype(f32)
    seq_in = tuple(jnp.moveaxis(x_, 1, 0) for x_ in (r, decay, k, v, kk, a))
    S_T, y = lax.scan(step, S0, seq_in)
    y = jnp.moveaxis(y, 0, 1)
    mean = jnp.mean(y, axis=-1, keepdims=True)
    var = jnp.mean((y - mean) ** 2, axis=-1, keepdims=True)
    y = ((y - mean) * lax.rsqrt(var + GN_EPS)).reshape(bsz, t, D_WIDTH)
    y = y * lnx_w.astype(f32) + lnx_b.astype(f32)
    bonus = jnp.sum(r * k * r_k.astype(f32), axis=-1, keepdims=True) * v
    y = (y + bonus.reshape(bsz, t, D_WIDTH)) * g
    return y.astype(zd.dtype), zf[:, -1], S_T


def _layer(x, sb_past_k, sb_past_v, s5_h_re, s5_h_im, wkv0, shift0,
           ln_ffn1, w_ffn1_gate, w_ffn1_up, w_ffn1_down, ln_mix, w_in,
           gmlp_v_norm, gmlp_ws, gmlp_b, out_norm_a,
           s5_lam_re, s5_lam_im, s5_log_dt, s5_b_re, s5_b_im, s5_c_re, s5_c_im, s5_d, s5_w_glu, s5_b_glu, out_norm_b,
           sb_q_norm, sb_k_norm, out_norm_c,
           rwkv_mu, rwkv_w0, rwkv_w2, rwkv_a0, rwkv_a2, rwkv_g2, rwkv_k_k, rwkv_k_a, rwkv_r_k, rwkv_lnx_w, rwkv_lnx_b,
           w_out, ln_ffn2, w_ffn2_gate, w_ffn2_up, w_ffn2_down):
    x = _macaron_half_ffn(x, ln_ffn1, w_ffn1_gate, w_ffn1_up, w_ffn1_down)
    h = _rmsnorm(x, ln_mix)
    z = h @ w_in
    cuts = [2 * A_WIDTH, 2 * A_WIDTH + B_WIDTH, 2 * A_WIDTH + B_WIDTH + 3 * C_WIDTH]
    za, zb, zc, zd = jnp.split(z, cuts, axis=-1)
    oa, gmlp_v = _gmlp_mixer(za, gmlp_v_norm, gmlp_ws, gmlp_b)
    ob, s5_re, s5_im = _s5_mixer(zb, s5_h_re, s5_h_im, s5_lam_re, s5_lam_im, s5_log_dt,
                                 s5_b_re, s5_b_im, s5_c_re, s5_c_im, s5_d, s5_w_glu, s5_b_glu)
    oc, sb_k, sb_v = _sb_mixer(zc, sb_q_norm, sb_k_norm, sb_past_k, sb_past_v)
    od, shift_new, wkv_new = _rwkv7_mixer(zd, shift0, wkv0, rwkv_mu, rwkv_w0, rwkv_w2, rwkv_a0, rwkv_a2,
                                          rwkv_g2, rwkv_k_k, rwkv_k_a, rwkv_r_k, rwkv_lnx_w, rwkv_lnx_b)
    o = jnp.concatenate([_rmsnorm(oa, out_norm_a), _rmsnorm(ob, out_norm_b),
                         _rmsnorm(oc, out_norm_c), od.astype(oa.dtype)], axis=-1)
    x = x + o @ w_out
    x = _macaron_half_ffn(x, ln_ffn2, w_ffn2_gate, w_ffn2_up, w_ffn2_down)
    return x, (sb_k, sb_v, s5_re, s5_im, wkv_new, shift_new, gmlp_v)


def setup_inputs(seed: int = 0) -> dict:
    key = jax.random.key(seed)
    ks = jax.random.split(key, 64)
    cnt = [0]
    f32 = jnp.float32

    def nxt():
        k = ks[cnt[0]]
        cnt[0] += 1
        return k

    def nrm(shape, scale=1.0):
        return scale * jax.random.normal(nxt(), shape, f32)

    def unif(shape, lo, hi):
        return jax.random.uniform(nxt(), shape, f32, lo, hi)

    def gain(shape):
        return 1.0 + nrm(shape, 0.02)

    Dm = D_MODEL
    inp = {}
    inp['x_prompt'] = nrm((BATCH, SEQ, Dm))
    inp['x_sample'] = nrm((DEC_BATCH, DEC_SEQ, Dm))
    inp['cache_sb_k'] = nrm((DEPTH, DEC_BATCH, PAST_LEN, C_HEADS, C_HD))
    inp['cache_sb_v'] = nrm((DEPTH, DEC_BATCH, PAST_LEN, C_HEADS, C_HD))
    inp['state_s5_re'] = nrm((DEPTH, DEC_BATCH, B_GROUPS, S5_P), 0.3)
    inp['state_s5_im'] = nrm((DEPTH, DEC_BATCH, B_GROUPS, S5_P), 0.3)
    inp['state_rwkv_wkv'] = nrm((DEPTH, DEC_BATCH, D_HEADS, D_HD, D_HD))
    inp['state_rwkv_shift'] = nrm((DEPTH, DEC_BATCH, D_COLS))
    inp['ln_ffn1'] = gain((DEPTH, Dm))
    inp['w_ffn1_gate'] = nrm((DEPTH, Dm, D_FF), Dm ** -0.5)
    inp['w_ffn1_up'] = nrm((DEPTH, Dm, D_FF), Dm ** -0.5)
    inp['w_ffn1_down'] = nrm((DEPTH, D_FF, Dm), D_FF ** -0.5)
    inp['ln_mix'] = gain((DEPTH, Dm))
    inp['w_in'] = nrm((DEPTH, Dm, IN_COLS), Dm ** -0.5)
    inp['gmlp_v_norm'] = gain((DEPTH, A_HD))
    inp['gmlp_ws'] = nrm((DEPTH, A_HEADS, GMLP_CHUNK, GMLP_CHUNK), GMLP_CHUNK ** -0.5)
    inp['gmlp_b'] = gain((DEPTH, A_HEADS, GMLP_CHUNK))
    inp['out_norm_a'] = gain((DEPTH, A_WIDTH))
    inp['s5_lam_re'] = -0.5 + nrm((DEPTH, B_GROUPS, S5_P), 0.01)
    inp['s5_lam_im'] = jnp.pi * jnp.arange(S5_P, dtype=f32) + nrm((DEPTH, B_GROUPS, S5_P), 0.01)
    inp['s5_log_dt'] = unif((DEPTH, B_GROUPS), math.log(1e-3), math.log(1e-1))
    inp['s5_b_re'] = nrm((DEPTH, B_GROUPS, S5_P, S5_CH), S5_CH ** -0.5)
    inp['s5_b_im'] = nrm((DEPTH, B_GROUPS, S5_P, S5_CH), S5_CH ** -0.5)
    inp['s5_c_re'] = nrm((DEPTH, B_GROUPS, S5_CH, S5_P), 0.5)
    inp['s5_c_im'] = nrm((DEPTH, B_GROUPS, S5_CH, S5_P), 0.5)
    inp['s5_d'] = nrm((DEPTH, B_WIDTH), 0.5)
    inp['s5_w_glu'] = nrm((DEPTH, B_WIDTH, B_WIDTH), B_WIDTH ** -0.5)
    inp['s5_b_glu'] = nrm((DEPTH, B_WIDTH), 0.02)
    inp['out_norm_b'] = gain((DEPTH, B_WIDTH))
    inp['sb_q_norm'] = gain((DEPTH, C_HD))
    inp['sb_k_norm'] = gain((DEPTH, C_HD))
    inp['out_norm_c'] = gain((DEPTH, C_WIDTH))
    inp['rwkv_mu'] = unif((DEPTH, D_COLS), 0.0, 1.0)
    ramp = (jnp.arange(D_WIDTH, dtype=f32) / (D_WIDTH - 1)) ** 0.9
    inp['rwkv_w0'] = -6.5 + 5.0 * ramp + nrm((DEPTH, D_WIDTH), 0.1)
    inp['rwkv_w2'] = nrm((DEPTH, D_W_RANK, D_WIDTH), 0.5 * D_W_RANK ** -0.5)
    inp['rwkv_a0'] = nrm((DEPTH, D_WIDTH), 0.1)
    inp['rwkv_a2'] = nrm((DEPTH, D_A_RANK, D_WIDTH), 0.5 * D_A_RANK ** -0.5)
    inp['rwkv_g2'] = nrm((DEPTH, D_G_RANK, D_WIDTH), D_G_RANK ** -0.5)
    inp['rwkv_k_k'] = 0.85 + nrm((DEPTH, D_WIDTH), 0.02)
    inp['rwkv_k_a'] = gain((DEPTH, D_WIDTH))
    inp['rwkv_r_k'] = nrm((DEPTH, D_HEADS, D_HD), 0.1)
    inp['rwkv_lnx_w'] = gain((DEPTH, D_WIDTH))
    inp['rwkv_lnx_b'] = nrm((DEPTH, D_WIDTH), 0.02)
    inp['w_out'] = nrm((DEPTH, MIX_WIDTH, Dm), MIX_WIDTH ** -0.5)
    inp['ln_ffn2'] = gain((DEPTH, Dm))
    inp['w_ffn2_gate'] = nrm((DEPTH, Dm, D_FF), Dm ** -0.5)
    inp['w_ffn2_up'] = nrm((DEPTH, Dm, D_FF), Dm ** -0.5)
    inp['w_ffn2_down'] = nrm((DEPTH, D_FF, Dm), D_FF ** -0.5)
    return inp


def reference(x_prompt, x_sample, cache_sb_k, cache_sb_v, state_s5_re, state_s5_im, state_rwkv_wkv, state_rwkv_shift,
              ln_ffn1, w_ffn1_gate, w_ffn1_up, w_ffn1_down, ln_mix, w_in,
              gmlp_v_norm, gmlp_ws, gmlp_b, out_norm_a,
              s5_lam_re, s5_lam_im, s5_log_dt, s5_b_re, s5_b_im, s5_c_re, s5_c_im, s5_d, s5_w_glu, s5_b_glu, out_norm_b,
              sb_q_norm, sb_k_norm, out_norm_c,
              rwkv_mu, rwkv_w0, rwkv_w2, rwkv_a0, rwkv_a2, rwkv_g2, rwkv_k_k, rwkv_k_a, rwkv_r_k, rwkv_lnx_w, rwkv_lnx_b,
              w_out, ln_ffn2, w_ffn2_gate, w_ffn2_up, w_ffn2_down):
    weights = (ln_ffn1, w_ffn1_gate, w_ffn1_up, w_ffn1_down, ln_mix, w_in,
               gmlp_v_norm, gmlp_ws, gmlp_b, out_norm_a,
               s5_lam_re, s5_lam_im, s5_log_dt, s5_b_re, s5_b_im, s5_c_re, s5_c_im, s5_d, s5_w_glu, s5_b_glu, out_norm_b,
               sb_q_norm, sb_k_norm, out_norm_c,
               rwkv_mu, rwkv_w0, rwkv_w2, rwkv_a0, rwkv_a2, rwkv_g2, rwkv_k_k, rwkv_k_a, rwkv_r_k, rwkv_lnx_w, rwkv_lnx_b,
               w_out, ln_ffn2, w_ffn2_gate, w_ffn2_up, w_ffn2_down)
    xp, xs = x_prompt, x_sample
    new_p, new_s = [], []
    for l in range(DEPTH):
        wl = [w[l] for w in weights]
        xp, st_p = _layer(xp, None, None, None, None, None, None, *wl)
        xs, st_s = _layer(xs, cache_sb_k[l], cache_sb_v[l], state_s5_re[l], state_s5_im[l],
                          state_rwkv_wkv[l], state_rwkv_shift[l], *wl)
        new_p.append(st_p)
        new_s.append(st_s)

    def stk(lst, i):
        return jnp.stack([s[i] for s in lst], axis=0)

    return (xp, xs,
            stk(new_p, 0), stk(new_p, 1), stk(new_s, 0), stk(new_s, 1),
            stk(new_p, 2), stk(new_p, 3), stk(new_s, 2), stk(new_s, 3),
            stk(new_p, 4), stk(new_s, 4),
            stk(new_p, 5), stk(new_s, 5),
            stk(new_s, 6))
```

```python
import functools

import jax
import jax.numpy as jnp
from jax import lax
from jax.experimental import pallas as pl
from jax.experimental.pallas import tpu as pltpu

F32 = jnp.float32
BF16 = jnp.bfloat16

RMS_EPS = 1e-6
GN_EPS = 64e-5
A_HEADS = 4
GMLP_CHUNK = 128
HD = 64
GW = 512
S5_STATE = 2048
RWKV_CHUNK = 64
VMEM_LIMIT = 56 * 1024 * 1024


def _cparams(*sem):
    return pltpu.CompilerParams(dimension_semantics=sem, vmem_limit_bytes=VMEM_LIMIT)


def _bf(x):
    return x.astype(BF16)


_NN = (((1,), (0,)), ((), ()))
_NT = (((1,), (1,)), ((), ()))
_TN = (((0,), (0,)), ((), ()))


def _dot(a, b, dims=_NN):
    return lax.dot_general(_bf(a), _bf(b), dims, preferred_element_type=F32)


def _split(x):
    hi = _bf(x)
    lo = _bf(x - hi.astype(F32))
    return hi, lo


def _dot3(a, b, dims=_NN):
    ah, al = _split(a)
    bh, bl = _split(b)
    d = functools.partial(lax.dot_general, dimension_numbers=dims, preferred_element_type=F32)
    return d(ah, bh) + (d(ah, bl) + d(al, bh))


def _dot2_exact_rhs(a, b_bf, dims=_NN):
    ah, al = _split(a)
    d = functools.partial(lax.dot_general, dimension_numbers=dims, preferred_element_type=F32)
    return d(ah, b_bf) + d(al, b_bf)


def _dot2_exact_lhs(a_bf, b, dims=_NN):
    bh, bl = _split(b)
    d = functools.partial(lax.dot_general, dimension_numbers=dims, preferred_element_type=F32)
    return d(a_bf, bh) + d(a_bf, bl)


def _rms_rows(x, gain):
    ms = jnp.mean(x * x, axis=-1, keepdims=True)
    return x * lax.rsqrt(ms + RMS_EPS) * gain


def _iota2(n, m, axis):
    return lax.broadcasted_iota(jnp.int32, (n, m), axis)


def _rms_kernel(x_ref, g_ref, h_ref):
    h_ref[...] = _bf(_rms_rows(x_ref[...], g_ref[...]))


def _rms_call(x, gain, tm=512):
    n, d = x.shape
    return pl.pallas_call(
        _rms_kernel,
        grid=(n // tm,),
        in_specs=[pl.BlockSpec((tm, d), lambda i: (i, 0)), pl.BlockSpec((1, d), lambda i: (0, 0))],
        out_specs=pl.BlockSpec((tm, d), lambda i: (i, 0)),
        out_shape=jax.ShapeDtypeStruct((n, d), BF16),
        compiler_params=_cparams("parallel"),
        name="rms",
    )(x, gain)


def _ffn_kernel(x_ref, h_ref, wg_ref, wu_ref, wd_ref, gn_ref, o_ref, hn_ref, *, nj):
    j = pl.program_id(1)

    @pl.when(j == 0)
    def _():
        o_ref[...] = jnp.zeros_like(o_ref)

    h = h_ref[...]
    g = jnp.dot(h, wg_ref[...], preferred_element_type=F32)
    u = jnp.dot(h, wu_ref[...], preferred_element_type=F32)
    a = _bf(g * jax.nn.sigmoid(g) * u)
    o_ref[...] += jnp.dot(a, wd_ref[...], preferred_element_type=F32)

    @pl.when(j == nj - 1)
    def _():
        y = x_ref[...] + 0.5 * o_ref[...]
        o_ref[...] = y
        hn_ref[...] = _bf(_rms_rows(y, gn_ref[...]))


def _ffn_call(x, h, wg, wu, wd, gain_next, layer, tm=512, tf=512):
    n, d = x.shape
    ff = wg.shape[-1]
    nj = ff // tf
    return pl.pallas_call(
        functools.partial(_ffn_kernel, nj=nj),
        grid=(n // tm, nj),
        in_specs=[
            pl.BlockSpec((tm, d), lambda i, j: (i, 0)),
            pl.BlockSpec((tm, d), lambda i, j: (i, 0)),
            pl.BlockSpec((None, d, tf), lambda i, j: (layer, 0, j)),
            pl.BlockSpec((None, d, tf), lambda i, j: (layer, 0, j)),
            pl.BlockSpec((None, tf, d), lambda i, j: (layer, j, 0)),
            pl.BlockSpec((1, d), lambda i, j: (0, 0)),
        ],
        out_specs=[pl.BlockSpec((tm, d), lambda i, j: (i, 0)), pl.BlockSpec((tm, d), lambda i, j: (i, 0))],
        out_shape=[jax.ShapeDtypeStruct((n, d), F32), jax.ShapeDtypeStruct((n, d), BF16)],
        compiler_params=_cparams("parallel", "arbitrary"),
        name="ffn",
    )(x, h, wg, wu, wd, gain_next)


def _inproj_kernel(h_ref, w_ref, z_ref):
    z_ref[...] = jnp.dot(h_ref[...], w_ref[...], preferred_element_type=F32)


def _inproj_call(h, w_in, layer, tm=512):
    n, d = h.shape
    cols = w_in.shape[-1]
    tn = cols // 2
    return pl.pallas_call(
        _inproj_kernel,
        grid=(2, n // tm),
        in_specs=[pl.BlockSpec((tm, d), lambda j, i: (i, 0)),
                  pl.BlockSpec((None, d, tn), lambda j, i: (layer, 0, j))],
        out_specs=pl.BlockSpec((tm, tn), lambda j, i: (i, j)),
        out_shape=jax.ShapeDtypeStruct((n, cols), F32),
        compiler_params=_cparams("parallel", "parallel"),
        name="inproj",
    )(h, w_in)


def _outproj_kernel(x_ref, oa_ref, ob_ref, oc_ref, od_ref, w_ref, gn_ref, o_ref, hn_ref):
    acc = x_ref[...]
    for i, r in enumerate((oa_ref, ob_ref, oc_ref, od_ref)):
        acc = acc + jnp.dot(_bf(r[...]), w_ref[i * GW:(i + 1) * GW, :], preferred_element_type=F32)
    o_ref[...] = acc
    hn_ref[...] = _bf(_rms_rows(acc, gn_ref[...]))


def _outproj_call(x, oa, ob, oc, od, w_out, gain_next, layer, tm=256):
    n, d = x.shape
    row = lambda i: (i, 0)
    return pl.pallas_call(
        _outproj_kernel,
        grid=(n // tm,),
        in_specs=[pl.BlockSpec((tm, d), row)] + [pl.BlockSpec((tm, GW), row)] * 4 + [
            pl.BlockSpec((None, d, d), lambda i: (layer, 0, 0)),
            pl.BlockSpec((1, d), lambda i: (0, 0))],
        out_specs=[pl.BlockSpec((tm, d), row), pl.BlockSpec((tm, d), row)],
        out_shape=[jax.ShapeDtypeStruct((n, d), F32), jax.ShapeDtypeStruct((n, d), BF16)],
        compiler_params=_cparams("parallel"),
        name="outproj",
    )(x, oa, ob, oc, od, w_out, gain_next)


def _gmlp_kernel(z_ref, vn_ref, ws_ref, bt_ref, on_ref, o_ref, v_ref, *, L):
    z = jax.nn.gelu(z_ref[...])
    causal = _iota2(L, L, 1) <= _iota2(L, L, 0)
    hw = GW // A_HEADS
    ss = jnp.zeros((L, 1), F32)
    outs = []
    for h in range(A_HEADS):
        vh = _rms_rows(z[:, GW + h * hw:GW + (h + 1) * hw], vn_ref[...])
        v_ref[:, h * hw:(h + 1) * hw] = vh
        w = jnp.where(causal, ws_ref[h], 0.0)
        s = _dot(w, vh) + bt_ref[:, h:h + 1]
        oh = z[:, h * hw:(h + 1) * hw] * s
        ss = ss + jnp.sum(oh * oh, axis=-1, keepdims=True)
        outs.append(oh)
    scale = lax.rsqrt(ss * (1.0 / GW) + RMS_EPS)
    for h in range(A_HEADS):
        o_ref[:, h * hw:(h + 1) * hw] = outs[h] * scale * on_ref[:, h * hw:(h + 1) * hw]


def _gmlp_call(z, row0, nrows, L, v_norm, ws, b, out_norm):
    nb = nrows // L
    b0 = row0 // L
    ws_l = ws[:, :L, :L]
    bt = jnp.transpose(b[:, :L])
    row = lambda i: (i, 0)
    return pl.pallas_call(
        functools.partial(_gmlp_kernel, L=L),
        grid=(nb,),
        in_specs=[pl.BlockSpec((L, 2 * GW), lambda i: (b0 + i, 0)),
                  pl.BlockSpec((1, GW // A_HEADS), lambda i: (0, 0)),
                  pl.BlockSpec((A_HEADS, L, L), lambda i: (0, 0, 0)),
                  pl.BlockSpec((L, A_HEADS), lambda i: (0, 0)),
                  pl.BlockSpec((1, GW), lambda i: (0, 0))],
        out_specs=[pl.BlockSpec((L, GW), row), pl.BlockSpec((L, GW), row)],
        out_shape=[jax.ShapeDtypeStruct((nrows, GW), F32), jax.ShapeDtypeStruct((nrows, GW), F32)],
        compiler_params=_cparams("parallel"),
        name="gmlp",
    )(z, v_norm, ws_l, bt, out_norm)


def _s5_param_kernel(lr_ref, li_ref, ldt_ref, brt_ref, bit_ref, pw_re_ref, pw_im_ref, bb_re_ref, bb_im_ref):
    lr = lr_ref[...]
    li = li_ref[...]
    dt = jnp.exp(ldt_ref[...])
    n = (_iota2(8, S5_STATE, 0) + 1).astype(F32)
    mag = jnp.exp(n * (lr * dt))
    ang = n * (li * dt)
    pw_re = mag * jnp.cos(ang)
    pw_im = mag * jnp.sin(ang)
    pw_re_ref[...] = pw_re
    pw_im_ref[...] = pw_im
    ab_re = pw_re[0:1, :]
    ab_im = pw_im[0:1, :]
    den = lr * lr + li * li
    nr, ni = ab_re - 1.0, ab_im
    cf_re = (nr * lr + ni * li) / den
    cf_im = (ni * lr - nr * li) / den
    br = brt_ref[...]
    bi = bit_ref[...]
    bb_re_ref[...] = cf_re * br - cf_im * bi
    bb_im_ref[...] = cf_re * bi + cf_im * br


def _s5_params(lam_re, lam_im, log_dt, b_re, b_im):
    g, p = lam_re.shape
    s = g * p
    ch = b_re.shape[-1]
    ldt = jnp.broadcast_to(log_dt[:, None], (g, p)).reshape(1, s)
    brt = jnp.transpose(b_re.reshape(s, ch))
    bit = jnp.transpose(b_im.reshape(s, ch))
    return pl.pallas_call(
        _s5_param_kernel,
        out_shape=[jax.ShapeDtypeStruct((8, s), F32), jax.ShapeDtypeStruct((8, s), F32),
                   jax.ShapeDtypeStruct((ch, s), F32), jax.ShapeDtypeStruct((ch, s), F32)],
        name="s5_params",
    )(lam_re.reshape(1, s), lam_im.reshape(1, s), ldt, brt, bit)


def _cmul(ar, ai, xr, xi):
    return ar * xr - ai * xi, ar * xi + ai * xr


def _s5_kernel(u_ref, h0r_ref, h0i_ref, pwr_ref, pwi_ref, bblk_ref, cre_ref, cim_ref, d_ref, wglu_ref, bglu_ref,
               on_ref, o_ref, hr_out_ref, hi_out_ref, bu_ref, hre_ref, him_ref, cr_ref, ci_ref, *, tc, nct):
    c = pl.program_id(1)
    S = S5_STATE
    LW = 512

    @pl.when(c == 0)
    def _():
        cr_ref[...] = h0r_ref[0]
        ci_ref[...] = h0i_ref[0]

    u = u_ref[...]
    bu_ref[...] = jnp.dot(_bf(u), bblk_ref[...], preferred_element_type=F32)

    rows = _iota2(8, LW, 0)
    for lc in range(S // LW):
        sl = slice(lc * LW, (lc + 1) * LW)
        pr = pwr_ref[:, sl]
        pi = pwi_ref[:, sl]
        a1r = jnp.where(rows >= 1, pr[0:1, :], 0.0)
        a1i = jnp.where(rows >= 1, pi[0:1, :], 0.0)
        a2r = jnp.where(rows >= 2, pr[1:2, :], 0.0)
        a2i = jnp.where(rows >= 2, pi[1:2, :], 0.0)
        a4r = jnp.where(rows >= 4, pr[3:4, :], 0.0)
        a4i = jnp.where(rows >= 4, pi[3:4, :], 0.0)

        def tile(i, carry):
            kr, ki = carry
            r0 = pl.multiple_of(i * 8, 8)
            xr = bu_ref[pl.ds(r0, 8), lc * LW:(lc + 1) * LW]
            xi = bu_ref[pl.ds(r0, 8), S + lc * LW:S + (lc + 1) * LW]
            for (ar, ai, sh) in ((a1r, a1i, 1), (a2r, a2i, 2), (a4r, a4i, 4)):
                sr, si = _cmul(ar, ai, pltpu.roll(xr, sh, 0), pltpu.roll(xi, sh, 0))
                xr, xi = xr + sr, xi + si
            sr, si = _cmul(pr, pi, kr, ki)
            xr, xi = xr + sr, xi + si
            hre_ref[pl.ds(r0, 8), sl] = xr
            him_ref[pl.ds(r0, 8), sl] = xi
            return xr[7:8, :], xi[7:8, :]

        kr, ki = lax.fori_loop(0, tc // 8, tile, (cr_ref[:, sl], ci_ref[:, sl]))
        cr_ref[:, sl] = kr
        ci_ref[:, sl] = ki

    y = (jnp.dot(_bf(hre_ref[...]), cre_ref[...], preferred_element_type=F32)
         - jnp.dot(_bf(him_ref[...]), cim_ref[...], preferred_element_type=F32))
    y = y + d_ref[...] * u
    g = jax.nn.gelu(y)
    out = g * jax.nn.sigmoid(_dot(g, wglu_ref[...]) + bglu_ref[...])
    o_ref[...] = _rms_rows(out, on_ref[...])

    @pl.when(c == nct - 1)
    def _():
        hr_out_ref[0] = cr_ref[...]
        hi_out_ref[0] = ci_ref[...]


def _s5_call(z, row0, nseq, t, h0_re, h0_im, pw_re, pw_im, bblk, c_re, c_im, d_skip, w_glu, b_glu, out_norm,
             col_blk, tc):
    nct = t // tc
    S = S5_STATE
    blk0 = row0 // tc
    const2 = lambda b, c: (0, 0)
    outs = pl.pallas_call(
        functools.partial(_s5_kernel, tc=tc, nct=nct),
        grid=(nseq, nct),
        in_specs=[pl.BlockSpec((tc, GW), lambda b, c: (blk0 + b * nct + c, col_blk)),
                  pl.BlockSpec((1, 1, S), lambda b, c: (b, 0, 0)),
                  pl.BlockSpec((1, 1, S), lambda b, c: (b, 0, 0)),
                  pl.BlockSpec((8, S), const2), pl.BlockSpec((8, S), const2),
                  pl.BlockSpec((GW, 2 * S), const2),
                  pl.BlockSpec((S, GW), const2), pl.BlockSpec((S, GW), const2),
                  pl.BlockSpec((1, GW), const2),
                  pl.BlockSpec((GW, GW), const2), pl.BlockSpec((1, GW), const2), pl.BlockSpec((1, GW), const2)],
        out_specs=[pl.BlockSpec((tc, GW), lambda b, c: (b * nct + c, 0)),
                   pl.BlockSpec((1, 1, S), lambda b, c: (b, 0, 0)),
                   pl.BlockSpec((1, 1, S), lambda b, c: (b, 0, 0))],
        out_shape=[jax.ShapeDtypeStruct((nseq * t, GW), F32),
                   jax.ShapeDtypeStruct((nseq, 1, S), F32), jax.ShapeDtypeStruct((nseq, 1, S), F32)],
        scratch_shapes=[pltpu.VMEM((tc, 2 * S), F32), pltpu.VMEM((tc, S), F32), pltpu.VMEM((tc, S), F32),
                        pltpu.VMEM((1, S), F32), pltpu.VMEM((1, S), F32)],
        compiler_params=_cparams("parallel", "arbitrary"),
        name="s5",
    )(z, h0_re.reshape(nseq, 1, S), h0_im.reshape(nseq, 1, S), pw_re, pw_im, bblk, c_re, c_im,
      d_skip, w_glu, b_glu, out_norm)
    return outs[0], outs[1].reshape(nseq, S), outs[2].reshape(nseq, S)


def _sbprep_kernel(q_ref, k_ref, gmat_ref, qg_ref, kg_ref, qn_ref, kn_ref):
    gmat = gmat_ref[...]
    for src, gain, dst, post in ((q_ref, qg_ref, qn_ref, HD ** -0.5), (k_ref, kg_ref, kn_ref, 1.0)):
        x = src[...]
        ms = _dot2_exact_rhs(x * x, gmat) * (1.0 / HD)
        y = x * lax.rsqrt(ms + RMS_EPS) * gain[...]
        dst[...] = y * post if post != 1.0 else y


def _sbprep_call(z, gmat, q_gain, k_gain, tm=512):
    n = z.shape[0]
    row = lambda i: (i, 0)
    return pl.pallas_call(
        _sbprep_kernel,
        grid=(n // tm,),
        in_specs=[pl.BlockSpec((tm, GW), lambda i: (i, 3)), pl.BlockSpec((tm, GW), lambda i: (i, 4)),
                  pl.BlockSpec((GW, GW), lambda i: (0, 0)),
                  pl.BlockSpec((1, GW), lambda i: (0, 0)), pl.BlockSpec((1, GW), lambda i: (0, 0))],
        out_specs=[pl.BlockSpec((tm, GW), row), pl.BlockSpec((tm, GW), row)],
        out_shape=[jax.ShapeDtypeStruct((n, GW), F32), jax.ShapeDtypeStruct((n, GW), F32)],
        compiler_params=_cparams("parallel"),
        name="sbprep",
    )(z, z, gmat, q_gain, k_gain)


def _sb_tile(qm, k2, v2, carry, acc, umat, causal):
    z = lax.dot_general(qm, k2, _NT, preferred_element_type=F32)
    lk = -(jnp.maximum(z, 0.0) + jnp.log(1.0 + jnp.exp(-jnp.abs(z))))
    if causal is not None:
        lk = jnp.where(causal, lk, 0.0)
    aft = _dot2_exact_rhs(lk, umat)
    w = jnp.exp(z + lk + aft + carry)
    if causal is not None:
        w = jnp.where(causal, w, 0.0)
    acc = acc + jnp.dot(_bf(w), v2, preferred_element_type=F32)
    carry = carry + aft[:, 0:1] + lk[:, 0:1]
    return carry, acc


def _sb_kernel(q_ref, kd_ref, vd_ref, kp_ref, vp_ref, o_ref, *, tq, tk, npast, past_from_grid):
    lane = _iota2(1, 2 * HD, 1)
    m0 = lane < HD
    q = q_ref[...]
    qm = (_bf(jnp.where(m0, q, 0.0)), _bf(jnp.where(m0, 0.0, q)))
    causal = _iota2(tq, tq, 1) < _iota2(tq, tq, 0)
    ud = jnp.where(_iota2(tq, tq, 0) > _iota2(tq, tq, 1), 1.0, 0.0).astype(BF16)
    up = jnp.where(_iota2(tk, tk, 0) > _iota2(tk, tk, 1), 1.0, 0.0).astype(BF16)

    kd = _bf(kd_ref[...])
    vd = _bf(vd_ref[...])
    state = []
    for s in range(2):
        c, a = _sb_tile(qm[s], kd, vd, jnp.zeros((tq, 1), F32), jnp.zeros((tq, 2 * HD), F32), ud, causal)
        state += [c, a]

    nblk = pl.program_id(2) * (tq // tk) if past_from_grid else npast

    def body(i, st):
        j = nblk - 1 - i
        r0 = pl.multiple_of(j * tk, tk)
        k2 = _bf(kp_ref[pl.ds(r0, tk), :])
        v2 = _bf(vp_ref[pl.ds(r0, tk), :])
        c0, a0 = _sb_tile(qm[0], k2, v2, st[0], st[1], up, None)
        c1, a1 = _sb_tile(qm[1], k2, v2, st[2], st[3], up, None)
        return (c0, a0, c1, a1)

    state = lax.fori_loop(0, nblk, body, tuple(state))
    o_ref[...] = jnp.where(m0, state[1], state[3])


def _sb_call(qn, kn, z, kpast, vpast, row0, nseq, t, tq, tk, past_len, v_col_blk):
    nq = t // tq
    blk0 = row0 // tq
    npairs = GW // (2 * HD)
    if kpast is None:
        kp_arr, vp_arr = kn, z
        sblk = row0 // t
        kp_spec = pl.BlockSpec((t, 2 * HD), lambda b, p, i: (sblk + b, p))
        vp_spec = pl.BlockSpec((t, 2 * HD), lambda b, p, i: (sblk + b, v_col_blk * npairs + p))
        npast, from_grid = 0, True
    else:
        kp_arr, vp_arr = kpast, vpast
        kp_spec = pl.BlockSpec((None, past_len, 2 * HD), lambda b, p, i: (b, 0, p))
        vp_spec = kp_spec
        npast, from_grid = past_len // tk, False
    qrow = lambda b, p, i: (blk0 + b * nq + i, p)
    return pl.pallas_call(
        functools.partial(_sb_kernel, tq=tq, tk=tk, npast=npast, past_from_grid=from_grid),
        grid=(nseq, npairs, nq),
        in_specs=[pl.BlockSpec((tq, 2 * HD), qrow), pl.BlockSpec((tq, 2 * HD), qrow),
                  pl.BlockSpec((tq, 2 * HD), lambda b, p, i: (blk0 + b * nq + i, v_col_blk * npairs + p)),
                  kp_spec, vp_spec],
        out_specs=pl.BlockSpec((tq, 2 * HD), lambda b, p, i: (b * nq + i, p)),
        out_shape=jax.ShapeDtypeStruct((nseq * t, GW), F32),
        compiler_params=_cparams("parallel", "parallel", "arbitrary"),
        name="sb_attn",
    )(qn, kn, z, kp_arr, vp_arr)


def _rmsgain_kernel(x_ref, g_ref, o_ref):
    o_ref[...] = _rms_rows(x_ref[...], g_ref[...])


def _rmsgain_call(x, gain, tm=512):
    n, d = x.shape
    return pl.pallas_call(
        _rmsgain_kernel,
        grid=(n // tm,),
        in_specs=[pl.BlockSpec((tm, d), lambda i: (i, 0)), pl.BlockSpec((1, d), lambda i: (0, 0))],
        out_specs=pl.BlockSpec((tm, d), lambda i: (i, 0)),
        out_shape=jax.ShapeDtypeStruct((n, d), F32),
        compiler_params=_cparams("parallel"),
        name="rmsgain",
    )(x, gain)


def _rwkv_prep_kernel(zr_ref, zk_ref, zv_ref, zwa_ref, zg_ref, first_ref, mu_ref, w0_ref, w2_ref, a0_ref, a2_ref,
                      g2_ref, kk_ref, ka_ref, rk_ref, gmat_ref,
                      r_out, lw_out, k_out, v_out, kk_out, b_out, g_out, bonus_out, prev_ref, *, tm, wr):
    c = pl.program_id(1)
    cols = prev_ref.shape[1]

    @pl.when(c == 0)
    def _():
        prev_ref[...] = first_ref[0]

    first_row = _iota2(tm, 1, 0) == 0

    def shifted(ref, lo, width):
        x = ref[...]
        prev = jnp.where(first_row, prev_ref[:, lo:lo + width], pltpu.roll(x, 1, 0))
        return x + (prev - x) * mu_ref[:, lo:lo + width]

    r = shifted(zr_ref, 0, GW)
    k = shifted(zk_ref, GW, GW)
    v = shifted(zv_ref, 2 * GW, GW)
    wa = shifted(zwa_ref, 3 * GW, 2 * wr)
    gl = shifted(zg_ref, 3 * GW + 2 * wr, cols - 3 * GW - 2 * wr)
    for ref, lo in ((zr_ref, 0), (zk_ref, GW), (zv_ref, 2 * GW), (zwa_ref, 3 * GW), (zg_ref, 3 * GW + 2 * wr)):
        prev_ref[:, lo:lo + ref.shape[1]] = ref[tm - 1:tm, :]

    xw = w0_ref[...] + _dot(jnp.tanh(wa), w2_ref[...])
    w_log = -(jnp.maximum(-xw, 0.0) + jnp.log(1.0 + jnp.exp(-jnp.abs(xw)))) - 0.5
    a = jax.nn.sigmoid(a0_ref[...] + _dot(wa, a2_ref[...]))
    g_out[...] = _dot(jax.nn.sigmoid(gl), g2_ref[...])
    gmat = gmat_ref[...]
    kk = k * kk_ref[...]
    kk = kk / jnp.maximum(jnp.sqrt(_dot2_exact_rhs(kk * kk, gmat)), 1e-12)
    k = k * (1.0 + (a - 1.0) * ka_ref[...])
    r_out[...] = r
    lw_out[...] = -jnp.exp(w_log)
    k_out[...] = k
    v_out[...] = v
    kk_out[...] = kk
    b_out[...] = kk * a
    bonus_out[...] = _dot2_exact_rhs(r * k * rk_ref[...], gmat) * v


def _rwkv_prep_call(z, row0, nseq, t, tm, first, mu, w0, w2p, a0, a2p, g2, k_k, k_a, r_k, gmat, wr):
    nct = t // tm
    blk0 = row0 // tm
    cols = mu.shape[-1]
    zrow = lambda blkw, off: pl.BlockSpec((tm, blkw), lambda b, c: (blk0 + b * nct + c, off))
    c0 = (z.shape[1] - cols)
    const = lambda shp: pl.BlockSpec(shp, lambda b, c: (0,) * len(shp))
    orow = pl.BlockSpec((tm, GW), lambda b, c: (b * nct + c, 0))
    return pl.pallas_call(
        functools.partial(_rwkv_prep_kernel, tm=tm, wr=wr),
        grid=(nseq, nct),
        in_specs=[zrow(GW, c0 // GW), zrow(GW, c0 // GW + 1), zrow(GW, c0 // GW + 2),
                  zrow(2 * wr, (c0 + 3 * GW) // (2 * wr)), zrow(cols - 3 * GW - 2 * wr, (c0 + 3 * GW) // (2 * wr) + 1),
                  pl.BlockSpec((1, 1, cols), lambda b, c: (b, 0, 0)),
                  const((1, cols)), const((1, GW)), const((2 * wr, GW)), const((1, GW)), const((2 * wr, GW)),
                  const((cols - 3 * GW - 2 * wr, GW)), const((1, GW)), const((1, GW)), const((1, GW)),
                  const((GW, GW))],
        out_specs=[orow] * 8,
        out_shape=[jax.ShapeDtypeStruct((nseq * t, GW), F32)] * 8,
        scratch_shapes=[pltpu.VMEM((1, cols), F32)],
        compiler_params=_cparams("parallel", "arbitrary"),
        name="rwkv_prep",
    )(z, z, z, z, z, first.reshape(nseq, 1, cols), mu, w0, w2p, a0, a2p, g2, k_k, k_a, r_k, gmat)


def _hat(x, m0):
    return jnp.concatenate([jnp.where(m0, x, 0.0), jnp.where(m0, 0.0, x)], axis=0)


def _rwkv_chunk_kernel(r_ref, lw_ref, k_ref, v_ref, kk_ref, b_ref, rt_out, p3_out, m_out, n_out):
    C = RWKV_CHUNK
    C2 = 2 * C
    lane = _iota2(1, 2 * HD, 1)
    m0 = lane < HD
    ri = _iota2(C2, C2, 0)
    ci = _iota2(C2, C2, 1)
    same = (ri < C) == (ci < C)
    strict = jnp.logical_and(same, ri > ci)
    incl = jnp.logical_and(same, ri >= ci)
    ltri = jnp.where(_iota2(C, C, 0) >= _iota2(C, C, 1), 1.0, 0.0).astype(BF16)
    eye = _iota2(2 * HD, 2 * HD, 0) == _iota2(2 * HD, 2 * HD, 1)
    blockdiag = (_iota2(2 * HD, 2 * HD, 0) < HD) == (_iota2(2 * HD, 2 * HD, 1) < HD)

    for p in range(GW // (2 * HD)):
        sl = slice(p * 2 * HD, (p + 1) * 2 * HD)
        lw = lw_ref[:, sl]
        cl = _dot2_exact_lhs(ltri, lw)
        clast = cl[C - 1:C, :]
        kkt = _hat(kk_ref[:, sl] * jnp.exp(cl - lw), m0)
        rt = _hat(r_ref[:, sl] * jnp.exp(cl), m0)
        einv = jnp.exp(-cl)
        kb = _hat(k_ref[:, sl] * einv, m0)
        bb = _hat(b_ref[:, sl] * einv, m0)
        efin = jnp.exp(clast - cl)
        kh = _hat(k_ref[:, sl] * efin, m0)
        bh = _hat(b_ref[:, sl] * efin, m0)
        vh = _hat(v_ref[:, sl], m0)

        lhs = jnp.concatenate([kkt, rt], axis=0)
        gk = _dot3(lhs, kb, _NT)
        gb = _dot3(lhs, bb, _NT)
        a_kk = jnp.where(strict, gk[:C2], 0.0)
        a_rk = jnp.where(incl, gk[C2:], 0.0)
        a_kb = jnp.where(strict, gb[:C2], 0.0)
        a_rb = jnp.where(incl, gb[C2:], 0.0)

        p1 = _dot3(a_kk, vh)
        x = jnp.concatenate([kkt, p1], axis=1)
        x = x - _dot3(a_kb, x)
        lp = a_kb
        n = 2
        while n < C:
            lp = _dot3(lp, lp)
            x = x + _dot3(lp, x)
            n *= 2
        kt = x[:, :2 * HD]
        p2 = x[:, 2 * HD:]
        y2 = _dot3(a_rb, x)
        rt_out[p] = rt - y2[:, :2 * HD]
        p3_out[p] = _dot3(a_rk, vh) - y2[:, 2 * HD:]
        dg = jnp.where(eye, jnp.exp(clast), 0.0)
        m_out[p] = dg - _dot3(kt, bh, _TN)
        n_out[p] = _dot3(vh, kh, _TN) - _dot3(p2, bh, _TN)


def _rwkv_chunk_call(r, lw, k, v, kk, b):
    n = r.shape[0]
    C = RWKV_CHUNK
    nc = n // C
    npairs = GW // (2 * HD)
    row = pl.BlockSpec((C, GW), lambda i: (i, 0))
    blk = lambda rows: pl.BlockSpec((None, npairs, rows, 2 * HD), lambda i: (i, 0, 0, 0))
    shp = lambda rows: jax.ShapeDtypeStruct((nc, npairs, rows, 2 * HD), F32)
    return pl.pallas_call(
        _rwkv_chunk_kernel,
        grid=(nc,),
        in_specs=[row] * 6,
        out_specs=[blk(2 * C), blk(2 * C), blk(2 * HD), blk(2 * HD)],
        out_shape=[shp(2 * C), shp(2 * C), shp(2 * HD), shp(2 * HD)],
        compiler_params=_cparams("parallel"),
        name="rwkv_chunk",
    )(r, lw, k, v, kk, b)


def _rwkv_seq_kernel(rt_ref, p3_ref, m_ref, n_ref, s0_ref, g_ref, bonus_ref, lnw_ref, lnb_ref, gmat_ref,
                     o_ref, s_out_ref, s_ref, *, nct):
    c = pl.program_id(1)
    C = RWKV_CHUNK

    @pl.when(c == 0)
    def _():
        s_ref[...] = s0_ref[0]

    gm = gmat_ref[0:2 * HD, 0:2 * HD]
    for p in range(GW // (2 * HD)):
        sl = slice(p * 2 * HD, (p + 1) * 2 * HD)
        s = s_ref[p]
        yh = _dot3(rt_ref[p], s, _NT) + p3_ref[p]
        y = yh[:C] + yh[C:]
        mean = _dot2_exact_rhs(y, gm) * (1.0 / HD)
        d = y - mean
        var = _dot2_exact_rhs(d * d, gm) * (1.0 / HD)
        yn = d * lax.rsqrt(var + GN_EPS) * lnw_ref[:, sl] + lnb_ref[:, sl]
        o_ref[:, sl] = (yn + bonus_ref[:, sl]) * g_ref[:, sl]
        s_ref[p] = _dot3(s, m_ref[p]) + n_ref[p]

    @pl.when(c == nct - 1)
    def _():
        s_out_ref[0] = s_ref[...]


def _rwkv_seq_call(rt, p3, m, n, chunk0, nseq, nct, s0, g, bonus, row0, lnx_w, lnx_b, gmat):
    C = RWKV_CHUNK
    npairs = GW // (2 * HD)
    blk0 = row0 // C
    cblk = lambda rows: pl.BlockSpec((None, npairs, rows, 2 * HD), lambda b, c: (chunk0 + b * nct + c, 0, 0, 0))
    sblk = pl.BlockSpec((1, npairs, 2 * HD, 2 * HD), lambda b, c: (b, 0, 0, 0))
    rowin = pl.BlockSpec((C, GW), lambda b, c: (blk0 + b * nct + c, 0))
    const = lambda shp: pl.BlockSpec(shp, lambda b, c: (0,) * len(shp))
    return pl.pallas_call(
        functools.partial(_rwkv_seq_kernel, nct=nct),
        grid=(nseq, nct),
        in_specs=[cblk(2 * C), cblk(2 * C), cblk(2 * HD), cblk(2 * HD), sblk, rowin, rowin,
                  const((1, GW)), const((1, GW)), const((GW, GW))],
        out_specs=[pl.BlockSpec((C, GW), lambda b, c: (b * nct + c, 0)), sblk],
        out_shape=[jax.ShapeDtypeStruct((nseq * nct * C, GW), F32),
                   jax.ShapeDtypeStruct((nseq, npairs, 2 * HD, 2 * HD), F32)],
        scratch_shapes=[pltpu.VMEM((npairs, 2 * HD, 2 * HD), F32)],
        compiler_params=_cparams("parallel", "arbitrary"),
        name="rwkv_seq",
    )(rt, p3, m, n, s0, g, bonus, lnx_w, lnx_b, gmat)


def _pair_states(s):
    b, h = s.shape[:2]
    s = s.reshape(b, h // 2, 2, HD, HD)
    z = jnp.zeros_like(s[:, :, 0])
    top = jnp.concatenate([s[:, :, 0], z], axis=-1)
    bot = jnp.concatenate([z, s[:, :, 1]], axis=-1)
    return jnp.concatenate([top, bot], axis=-2)


def _unpair_states(s2):
    b, hp = s2.shape[:2]
    return jnp.stack([s2[:, :, :HD, :HD], s2[:, :, HD:, HD:]], axis=2).reshape(b, 2 * hp, HD, HD)


def _forward(xp, xs, cache_k, cache_v, s5_re0, s5_im0, wkv0, shift0, W):
    Bp, Tp, D = xp.shape
    Bs, Ts, _ = xs.shape
    depth = W['ln_ffn1'].shape[0]
    past_len = cache_k.shape[2]
    Np, Ns = Bp * Tp, Bs * Ts
    n_heads_d = GW // HD
    wr = W['rwkv_w2'].shape[1]
    dcols = W['rwkv_mu'].shape[-1]

    x = jnp.concatenate([xp.reshape(Np, D), xs.reshape(Ns, D)], axis=0)
    wbf = {k: _bf(W[k]) for k in ('w_ffn1_gate', 'w_ffn1_up', 'w_ffn1_down', 'w_in', 'w_out',
                                  'w_ffn2_gate', 'w_ffn2_up', 'w_ffn2_down')}
    hid = lax.broadcasted_iota(jnp.int32, (GW, GW), 0) // HD
    gmat = (hid == jnp.transpose(hid)).astype(BF16)
    grp = lax.broadcasted_iota(jnp.int32, (GW, S5_STATE), 0) // (GW // 32) == \
        lax.broadcasted_iota(jnp.int32, (GW, S5_STATE), 1) // (S5_STATE // 32)
    zeros_first = jnp.zeros((Bp, dcols), F32)
    zeros_s5 = jnp.zeros((Bp, S5_STATE), F32)
    zeros_wkv = jnp.zeros((Bp, n_heads_d // 2, 2 * HD, 2 * HD), F32)

    h = _rms_call(x, W['ln_ffn1'][0][None])
    outs = {k: [] for k in ('kp', 'vp', 'ks', 'vs', 's5rp', 's5ip', 's5rs', 's5is', 'wkvp', 'wkvs', 'shp', 'shs', 'gv')}
    for l in range(depth):
        g1 = lambda name: W[name][l][None]
        x, h = _ffn_call(x, h, wbf['w_ffn1_gate'], wbf['w_ffn1_up'], wbf['w_ffn1_down'], g1('ln_mix'), l)
        z = _inproj_call(h, wbf['w_in'], l)

        oa_p, _ = _gmlp_call(z, 0, Np, min(Tp, GMLP_CHUNK), g1('gmlp_v_norm'), W['gmlp_ws'][l], W['gmlp_b'][l],
                             g1('out_norm_a'))
        oa_s, gv_s = _gmlp_call(z, Np, Ns, min(Ts, GMLP_CHUNK), g1('gmlp_v_norm'), W['gmlp_ws'][l], W['gmlp_b'][l],
                                g1('out_norm_a'))
        oa = jnp.concatenate([oa_p, oa_s], axis=0)

        pw_re, pw_im, bb_re, bb_im = _s5_params(W['s5_lam_re'][l], W['s5_lam_im'][l], W['s5_log_dt'][l],
                                                W['s5_b_re'][l], W['s5_b_im'][l])
        bblk = jnp.concatenate([jnp.where(grp, jnp.tile(bb_re, (32, 1)), 0.0),
                                jnp.where(grp, jnp.tile(bb_im, (32, 1)), 0.0)], axis=1).astype(BF16)
        grp_t = jnp.transpose(grp)
        cre = jnp.where(grp_t, jnp.tile(jnp.transpose(W['s5_c_re'][l], (0, 2, 1)).reshape(S5_STATE, -1), (1, 32)),
                        0.0).astype(BF16)
        cim = jnp.where(grp_t, jnp.tile(jnp.transpose(W['s5_c_im'][l], (0, 2, 1)).reshape(S5_STATE, -1), (1, 32)),
                        0.0).astype(BF16)
        s5_args = (pw_re, pw_im, bblk, cre, cim, g1('s5_d'), W['s5_w_glu'][l], g1('s5_b_glu'), g1('out_norm_b'))
        ob_p, s5rp, s5ip = _s5_call(z, 0, Bp, Tp, zeros_s5, zeros_s5, *s5_args, col_blk=2, tc=min(Tp, 256))
        ob_s, s5rs, s5is = _s5_call(z, Np, Bs, Ts, s5_re0[l].reshape(Bs, S5_STATE), s5_im0[l].reshape(Bs, S5_STATE),
                                    *s5_args, col_blk=2, tc=min(Ts, 256))
        ob = jnp.concatenate([ob_p, ob_s], axis=0)

        qg = jnp.tile(W['sb_q_norm'][l], GW // HD)[None]
        kg = jnp.tile(W['sb_k_norm'][l], GW // HD)[None]
        qn, kn = _sbprep_call(z, gmat, qg, kg)
        tq = min(Tp, 128)
        oc_p = _sb_call(qn, kn, z, None, None, 0, Bp, Tp, tq, tq, 0, v_col_blk=5)
        oc_s = _sb_call(qn, kn, z, cache_k[l].reshape(Bs, past_len, GW), cache_v[l].reshape(Bs, past_len, GW),
                        Np, Bs, Ts, Ts, min(past_len, 128), past_len, v_col_blk=5)
        oc = _rmsgain_call(jnp.concatenate([oc_p, oc_s], axis=0), g1('out_norm_c'))
        vcol = z[:, 5 * GW:6 * GW]

        w2p = jnp.concatenate([W['rwkv_w2'][l], jnp.zeros_like(W['rwkv_a2'][l])], axis=0)
        a2p = jnp.concatenate([jnp.zeros_like(W['rwkv_w2'][l]), W['rwkv_a2'][l]], axis=0)
        rk = W['rwkv_r_k'][l].reshape(1, GW)
        prep_args = (g1('rwkv_mu'), g1('rwkv_w0'), w2p, g1('rwkv_a0'), a2p, W['rwkv_g2'][l], g1('rwkv_k_k'),
                     g1('rwkv_k_a'), rk, gmat, wr)
        pp = _rwkv_prep_call(z, 0, Bp, Tp, min(Tp, 512), zeros_first, *prep_args)
        ps = _rwkv_prep_call(z, Np, Bs, Ts, Ts, shift0[l], *prep_args)
        r_, lw_, k_, v_, kk_, b_, g_, bonus_ = [jnp.concatenate([a, b], axis=0) for a, b in zip(pp, ps)]
        rt, p3, mm, nn = _rwkv_chunk_call(r_, lw_, k_, v_, kk_, b_)
        seq_args = (g1('rwkv_lnx_w'), g1('rwkv_lnx_b'), gmat)
        od_p, wkv_p = _rwkv_seq_call(rt, p3, mm, nn, 0, Bp, Tp // RWKV_CHUNK, zeros_wkv, g_, bonus_, 0, *seq_args)
        od_s, wkv_s = _rwkv_seq_call(rt, p3, mm, nn, Np // RWKV_CHUNK, Bs, Ts // RWKV_CHUNK, _pair_states(wkv0[l]),
                                     g_, bonus_, Np, *seq_args)
        od = jnp.concatenate([od_p, od_s], axis=0)

        gain_next = g1('ln_ffn2')
        x, h = _outproj_call(x, oa, ob, oc, od, wbf['w_out'], gain_next, l)
        gain_next = W['ln_ffn1'][l + 1][None] if l + 1 < depth else g1('ln_ffn2')
        x, h = _ffn_call(x, h, wbf['w_ffn2_gate'], wbf['w_ffn2_up'], wbf['w_ffn2_down'], gain_next, l)

        nh = GW // HD
        outs['kp'].append(kn[:Np].reshape(Bp, Tp, nh, HD))
        outs['vp'].append(vcol[:Np].reshape(Bp, Tp, nh, HD))
        outs['ks'].append(kn[Np:].reshape(Bs, Ts, nh, HD))
        outs['vs'].append(vcol[Np:].reshape(Bs, Ts, nh, HD))
        outs['s5rp'].append(s5rp.reshape(Bp, 32, -1))
        outs['s5ip'].append(s5ip.reshape(Bp, 32, -1))
        outs['s5rs'].append(s5rs.reshape(Bs, 32, -1))
        outs['s5is'].append(s5is.reshape(Bs, 32, -1))
        outs['wkvp'].append(_unpair_states(wkv_p))
        outs['wkvs'].append(_unpair_states(wkv_s))
        zd = z[:, z.shape[1] - dcols:]
        outs['shp'].append(zd[:Np].reshape(Bp, Tp, dcols)[:, -1])
        outs['shs'].append(zd[Np:].reshape(Bs, Ts, dcols)[:, -1])
        outs['gv'].append(gv_s.reshape(Bs, Ts, GW))

    st = lambda k: jnp.stack(outs[k], axis=0)
    return (x[:Np].reshape(Bp, Tp, D), x[Np:].reshape(Bs, Ts, D),
            st('kp'), st('vp'), st('ks'), st('vs'),
            st('s5rp'), st('s5ip'), st('s5rs'), st('s5is'),
            st('wkvp'), st('wkvs'), st('shp'), st('shs'), st('gv'))


_WEIGHT_NAMES = ('ln_ffn1', 'w_ffn1_gate', 'w_ffn1_up', 'w_ffn1_down', 'ln_mix', 'w_in',
                 'gmlp_v_norm', 'gmlp_ws', 'gmlp_b', 'out_norm_a',
                 's5_lam_re', 's5_lam_im', 's5_log_dt', 's5_b_re', 's5_b_im', 's5_c_re', 's5_c_im', 's5_d',
                 's5_w_glu', 's5_b_glu', 'out_norm_b',
                 'sb_q_norm', 'sb_k_norm', 'out_norm_c',
                 'rwkv_mu', 'rwkv_w0', 'rwkv_w2', 'rwkv_a0', 'rwkv_a2', 'rwkv_g2', 'rwkv_k_k', 'rwkv_k_a', 'rwkv_r_k',
                 'rwkv_lnx_w', 'rwkv_lnx_b',
                 'w_out', 'ln_ffn2', 'w_ffn2_gate', 'w_ffn2_up', 'w_ffn2_down')


def kernel(x_prompt, x_sample, cache_sb_k, cache_sb_v, state_s5_re, state_s5_im, state_rwkv_wkv, state_rwkv_shift,
           ln_ffn1, w_ffn1_gate, w_ffn1_up, w_ffn1_down, ln_mix, w_in,
           gmlp_v_norm, gmlp_ws, gmlp_b, out_norm_a,
           s5_lam_re, s5_lam_im, s5_log_dt, s5_b_re, s5_b_im, s5_c_re, s5_c_im, s5_d, s5_w_glu, s5_b_glu, out_norm_b,
           sb_q_norm, sb_k_norm, out_norm_c,
           rwkv_mu, rwkv_w0, rwkv_w2, rwkv_a0, rwkv_a2, rwkv_g2, rwkv_k_k, rwkv_k_a, rwkv_r_k, rwkv_lnx_w, rwkv_lnx_b,
           w_out, ln_ffn2, w_ffn2_gate, w_ffn2_up, w_ffn2_down):
    weights = (ln_ffn1, w_ffn1_gate, w_ffn1_up, w_ffn1_down, ln_mix, w_in,
               gmlp_v_norm, gmlp_ws, gmlp_b, out_norm_a,
               s5_lam_re, s5_lam_im, s5_log_dt, s5_b_re, s5_b_im, s5_c_re, s5_c_im, s5_d, s5_w_glu, s5_b_glu,
               out_norm_b, sb_q_norm, sb_k_norm, out_norm_c,
               rwkv_mu, rwkv_w0, rwkv_w2, rwkv_a0, rwkv_a2, rwkv_g2, rwkv_k_k, rwkv_k_a, rwkv_r_k, rwkv_lnx_w,
               rwkv_lnx_b, w_out, ln_ffn2, w_ffn2_gate, w_ffn2_up, w_ffn2_down)
    W = dict(zip(_WEIGHT_NAMES, weights))
    return _forward(x_prompt, x_sample, cache_sb_k, cache_sb_v, state_s5_re, state_s5_im, state_rwkv_wkv,
                    state_rwkv_shift, W)
```

```python
import functools

import jax
import jax.numpy as jnp
from jax import lax
from jax.experimental import pallas as pl
from jax.experimental.pallas import tpu as pltpu

F32 = jnp.float32
BF16 = jnp.bfloat16

RMS_EPS = 1e-6
GN_EPS = 64e-5
A_HEADS = 4
GMLP_CHUNK = 128
HD = 64
GW = 512
S5_STATE = 2048
RWKV_CHUNK = 64
VMEM_LIMIT = 56 * 1024 * 1024


def _cparams(*sem):
    return pltpu.CompilerParams(dimension_semantics=sem, vmem_limit_bytes=VMEM_LIMIT)


def _bf(x):
    return x.astype(BF16)


_NN = (((1,), (0,)), ((), ()))
_NT = (((1,), (1,)), ((), ()))
_TN = (((0,), (0,)), ((), ()))


def _dot(a, b, dims=_NN):
    return lax.dot_general(_bf(a), _bf(b), dims, preferred_element_type=F32)


def _split(x):
    hi = _bf(x)
    lo = _bf(x - hi.astype(F32))
    return hi, lo


def _dot3(a, b, dims=_NN):
    ah, al = _split(a)
    bh, bl = _split(b)
    d = functools.partial(lax.dot_general, dimension_numbers=dims, preferred_element_type=F32)
    return d(ah, bh) + (d(ah, bl) + d(al, bh))


def _dot3s(a, b, dims=_NN):
    d = functools.partial(lax.dot_general, dimension_numbers=dims, preferred_element_type=F32)
    return d(a[0], b[0]) + (d(a[0], b[1]) + d(a[1], b[0]))


def _dot2_exact_rhs(a, b_bf, dims=_NN):
    ah, al = _split(a)
    d = functools.partial(lax.dot_general, dimension_numbers=dims, preferred_element_type=F32)
    return d(ah, b_bf) + d(al, b_bf)


def _dot2_exact_lhs(a_bf, b, dims=_NN):
    bh, bl = _split(b)
    d = functools.partial(lax.dot_general, dimension_numbers=dims, preferred_element_type=F32)
    return d(a_bf, bh) + d(a_bf, bl)


def _rms_rows(x, gain):
    ms = jnp.mean(x * x, axis=-1, keepdims=True)
    return x * lax.rsqrt(ms + RMS_EPS) * gain


def _iota2(n, m, axis):
    return lax.broadcasted_iota(jnp.int32, (n, m), axis)


def _rms_kernel(x_ref, g_ref, h_ref):
    h_ref[...] = _bf(_rms_rows(x_ref[...], g_ref[...]))


def _rms_call(x, gain, tm=512):
    n, d = x.shape
    return pl.pallas_call(
        _rms_kernel,
        grid=(n // tm,),
        in_specs=[pl.BlockSpec((tm, d), lambda i: (i, 0)), pl.BlockSpec((1, d), lambda i: (0, 0))],
        out_specs=pl.BlockSpec((tm, d), lambda i: (i, 0)),
        out_shape=jax.ShapeDtypeStruct((n, d), BF16),
        compiler_params=_cparams("parallel"),
        name="rms",
    )(x, gain)


def _ffn_kernel(x_ref, h_ref, wg_ref, wu_ref, wd_ref, gn_ref, o_ref, hn_ref, *, nj):
    j = pl.program_id(1)

    @pl.when(j == 0)
    def _():
        o_ref[...] = jnp.zeros_like(o_ref)

    h = h_ref[...]
    g = jnp.dot(h, wg_ref[...], preferred_element_type=F32)
    u = jnp.dot(h, wu_ref[...], preferred_element_type=F32)
    a = _bf(g * jax.nn.sigmoid(g) * u)
    o_ref[...] += jnp.dot(a, wd_ref[...], preferred_element_type=F32)

    @pl.when(j == nj - 1)
    def _():
        y = x_ref[...] + 0.5 * o_ref[...]
        o_ref[...] = y
        hn_ref[...] = _bf(_rms_rows(y, gn_ref[...]))


def _ffn_call(x, h, wg, wu, wd, gain_next, layer, tm=512, tf=512):
    n, d = x.shape
    ff = wg.shape[-1]
    nj = ff // tf
    return pl.pallas_call(
        functools.partial(_ffn_kernel, nj=nj),
        grid=(n // tm, nj),
        in_specs=[
            pl.BlockSpec((tm, d), lambda i, j: (i, 0)),
            pl.BlockSpec((tm, d), lambda i, j: (i, 0)),
            pl.BlockSpec((None, d, tf), lambda i, j: (layer, 0, j)),
            pl.BlockSpec((None, d, tf), lambda i, j: (layer, 0, j)),
            pl.BlockSpec((None, tf, d), lambda i, j: (layer, j, 0)),
            pl.BlockSpec((1, d), lambda i, j: (0, 0)),
        ],
        out_specs=[pl.BlockSpec((tm, d), lambda i, j: (i, 0)), pl.BlockSpec((tm, d), lambda i, j: (i, 0))],
        out_shape=[jax.ShapeDtypeStruct((n, d), F32), jax.ShapeDtypeStruct((n, d), BF16)],
        compiler_params=_cparams("parallel", "arbitrary"),
        name="ffn",
    )(x, h, wg, wu, wd, gain_next)


def _inproj_kernel(h_ref, w_ref, z_ref):
    z_ref[...] = jnp.dot(h_ref[...], w_ref[...], preferred_element_type=F32)


def _inproj_call(h, w_in, layer, tm=512):
    n, d = h.shape
    cols = w_in.shape[-1]
    tn = cols // 2
    return pl.pallas_call(
        _inproj_kernel,
        grid=(2, n // tm),
        in_specs=[pl.BlockSpec((tm, d), lambda j, i: (i, 0)),
                  pl.BlockSpec((None, d, tn), lambda j, i: (layer, 0, j))],
        out_specs=pl.BlockSpec((tm, tn), lambda j, i: (i, j)),
        out_shape=jax.ShapeDtypeStruct((n, cols), F32),
        compiler_params=_cparams("parallel", "parallel"),
        name="inproj",
    )(h, w_in)


def _outproj_kernel(x_ref, oa_ref, ob_ref, oc_ref, od_ref, w_ref, gn_ref, o_ref, hn_ref):
    acc = x_ref[...]
    for i, r in enumerate((oa_ref, ob_ref, oc_ref, od_ref)):
        acc = acc + jnp.dot(_bf(r[...]), w_ref[i * GW:(i + 1) * GW, :], preferred_element_type=F32)
    o_ref[...] = acc
    hn_ref[...] = _bf(_rms_rows(acc, gn_ref[...]))


def _outproj_call(x, oa, ob, oc, od, w_out, gain_next, layer, tm=256):
    n, d = x.shape
    row = lambda i: (i, 0)
    return pl.pallas_call(
        _outproj_kernel,
        grid=(n // tm,),
        in_specs=[pl.BlockSpec((tm, d), row)] + [pl.BlockSpec((tm, GW), row)] * 4 + [
            pl.BlockSpec((None, d, d), lambda i: (layer, 0, 0)),
            pl.BlockSpec((1, d), lambda i: (0, 0))],
        out_specs=[pl.BlockSpec((tm, d), row), pl.BlockSpec((tm, d), row)],
        out_shape=[jax.ShapeDtypeStruct((n, d), F32), jax.ShapeDtypeStruct((n, d), BF16)],
        compiler_params=_cparams("parallel"),
        name="outproj",
    )(x, oa, ob, oc, od, w_out, gain_next)


def _gmlp_kernel(z_ref, vn_ref, ws_ref, bt_ref, on_ref, o_ref, v_ref, *, L):
    z = jax.nn.gelu(z_ref[...])
    causal = _iota2(L, L, 1) <= _iota2(L, L, 0)
    hw = GW // A_HEADS
    ss = jnp.zeros((L, 1), F32)
    outs = []
    for h in range(A_HEADS):
        vh = _rms_rows(z[:, GW + h * hw:GW + (h + 1) * hw], vn_ref[...])
        v_ref[:, h * hw:(h + 1) * hw] = vh
        w = jnp.where(causal, ws_ref[h], 0.0)
        s = _dot(w, vh) + bt_ref[:, h:h + 1]
        oh = z[:, h * hw:(h + 1) * hw] * s
        ss = ss + jnp.sum(oh * oh, axis=-1, keepdims=True)
        outs.append(oh)
    scale = lax.rsqrt(ss * (1.0 / GW) + RMS_EPS)
    for h in range(A_HEADS):
        o_ref[:, h * hw:(h + 1) * hw] = outs[h] * scale * on_ref[:, h * hw:(h + 1) * hw]


def _gmlp_call(z, row0, nrows, L, v_norm, ws, b, out_norm):
    nb = nrows // L
    b0 = row0 // L
    ws_l = ws[:, :L, :L]
    bt = jnp.transpose(b[:, :L])
    row = lambda i: (i, 0)
    return pl.pallas_call(
        functools.partial(_gmlp_kernel, L=L),
        grid=(nb,),
        in_specs=[pl.BlockSpec((L, 2 * GW), lambda i: (b0 + i, 0)),
                  pl.BlockSpec((1, GW // A_HEADS), lambda i: (0, 0)),
                  pl.BlockSpec((A_HEADS, L, L), lambda i: (0, 0, 0)),
                  pl.BlockSpec((L, A_HEADS), lambda i: (0, 0)),
                  pl.BlockSpec((1, GW), lambda i: (0, 0))],
        out_specs=[pl.BlockSpec((L, GW), row), pl.BlockSpec((L, GW), row)],
        out_shape=[jax.ShapeDtypeStruct((nrows, GW), F32), jax.ShapeDtypeStruct((nrows, GW), F32)],
        compiler_params=_cparams("parallel"),
        name="gmlp",
    )(z, v_norm, ws_l, bt, out_norm)


def _s5_param_kernel(lr_ref, li_ref, ldt_ref, brt_ref, bit_ref, pw_re_ref, pw_im_ref, bb_re_ref, bb_im_ref):
    lr = lr_ref[...]
    li = li_ref[...]
    dt = jnp.exp(ldt_ref[...])
    n = (_iota2(8, S5_STATE, 0) + 1).astype(F32)
    mag = jnp.exp(n * (lr * dt))
    ang = n * (li * dt)
    pw_re = mag * jnp.cos(ang)
    pw_im = mag * jnp.sin(ang)
    pw_re_ref[...] = pw_re
    pw_im_ref[...] = pw_im
    ab_re = pw_re[0:1, :]
    ab_im = pw_im[0:1, :]
    den = lr * lr + li * li
    nr, ni = ab_re - 1.0, ab_im
    cf_re = (nr * lr + ni * li) / den
    cf_im = (ni * lr - nr * li) / den
    br = brt_ref[...]
    bi = bit_ref[...]
    bb_re_ref[...] = cf_re * br - cf_im * bi
    bb_im_ref[...] = cf_re * bi + cf_im * br


def _s5_params(lam_re, lam_im, log_dt, b_re, b_im):
    g, p = lam_re.shape
    s = g * p
    ch = b_re.shape[-1]
    ldt = jnp.broadcast_to(log_dt[:, None], (g, p)).reshape(1, s)
    brt = jnp.transpose(b_re.reshape(s, ch))
    bit = jnp.transpose(b_im.reshape(s, ch))
    return pl.pallas_call(
        _s5_param_kernel,
        out_shape=[jax.ShapeDtypeStruct((8, s), F32), jax.ShapeDtypeStruct((8, s), F32),
                   jax.ShapeDtypeStruct((ch, s), F32), jax.ShapeDtypeStruct((ch, s), F32)],
        name="s5_params",
    )(lam_re.reshape(1, s), lam_im.reshape(1, s), ldt, brt, bit)


def _cmul(ar, ai, xr, xi):
    return ar * xr - ai * xi, ar * xi + ai * xr


def _s5_kernel(u_ref, h0r_ref, h0i_ref, pwr_ref, pwi_ref, bblk_ref, cre_ref, cim_ref, d_ref, wglu_ref, bglu_ref,
               on_ref, o_ref, hr_out_ref, hi_out_ref, bu_ref, hre_ref, him_ref, cr_ref, ci_ref, *, tc, nct):
    c = pl.program_id(1)
    S = S5_STATE
    LW = 512

    @pl.when(c == 0)
    def _():
        cr_ref[...] = h0r_ref[0]
        ci_ref[...] = h0i_ref[0]

    u = u_ref[...]
    NK = GW // 128
    SW = S // NK
    for kc in range(NK):
        ukc = _bf(u[:, kc * 128:(kc + 1) * 128])
        for half in range(2):
            cs = slice(half * S + kc * SW, half * S + (kc + 1) * SW)
            bu_ref[:, cs] = jnp.dot(ukc, bblk_ref[kc * 128:(kc + 1) * 128, cs], preferred_element_type=F32)

    rows = _iota2(8, LW, 0)
    for lc in range(S // LW):
        sl = slice(lc * LW, (lc + 1) * LW)
        pr = pwr_ref[:, sl]
        pi = pwi_ref[:, sl]
        a1r = jnp.where(rows >= 1, pr[0:1, :], 0.0)
        a1i = jnp.where(rows >= 1, pi[0:1, :], 0.0)
        a2r = jnp.where(rows >= 2, pr[1:2, :], 0.0)
        a2i = jnp.where(rows >= 2, pi[1:2, :], 0.0)
        a4r = jnp.where(rows >= 4, pr[3:4, :], 0.0)
        a4i = jnp.where(rows >= 4, pi[3:4, :], 0.0)

        def tile(i, carry):
            kr, ki = carry
            r0 = pl.multiple_of(i * 8, 8)
            xr = bu_ref[pl.ds(r0, 8), lc * LW:(lc + 1) * LW]
            xi = bu_ref[pl.ds(r0, 8), S + lc * LW:S + (lc + 1) * LW]
            for (ar, ai, sh) in ((a1r, a1i, 1), (a2r, a2i, 2), (a4r, a4i, 4)):
                sr, si = _cmul(ar, ai, pltpu.roll(xr, sh, 0), pltpu.roll(xi, sh, 0))
                xr, xi = xr + sr, xi + si
            sr, si = _cmul(pr, pi, kr, ki)
            xr, xi = xr + sr, xi + si
            hre_ref[pl.ds(r0, 8), sl] = xr
            him_ref[pl.ds(r0, 8), sl] = xi
            return xr[7:8, :], xi[7:8, :]

        kr, ki = lax.fori_loop(0, tc // 8, tile, (cr_ref[:, sl], ci_ref[:, sl]))
        cr_ref[:, sl] = kr
        ci_ref[:, sl] = ki

    ys = []
    for kc in range(NK):
        ss, os_ = slice(kc * SW, (kc + 1) * SW), slice(kc * 128, (kc + 1) * 128)
        ys.append(jnp.dot(_bf(hre_ref[:, ss]), cre_ref[ss, os_], preferred_element_type=F32)
                  - jnp.dot(_bf(him_ref[:, ss]), cim_ref[ss, os_], preferred_element_type=F32))
    y = jnp.concatenate(ys, axis=1) + d_ref[...] * u
    g = jax.nn.gelu(y)
    out = g * jax.nn.sigmoid(_dot(g, wglu_ref[...]) + bglu_ref[...])
    o_ref[...] = _rms_rows(out, on_ref[...])

    @pl.when(c == nct - 1)
    def _():
        hr_out_ref[0] = cr_ref[...]
        hi_out_ref[0] = ci_ref[...]


def _s5_call(z, row0, nseq, t, h0_re, h0_im, pw_re, pw_im, bblk, c_re, c_im, d_skip, w_glu, b_glu, out_norm,
             col_blk, tc):
    nct = t // tc
    S = S5_STATE
    blk0 = row0 // tc
    const2 = lambda b, c: (0, 0)
    outs = pl.pallas_call(
        functools.partial(_s5_kernel, tc=tc, nct=nct),
        grid=(nseq, nct),
        in_specs=[pl.BlockSpec((tc, GW), lambda b, c: (blk0 + b * nct + c, col_blk)),
                  pl.BlockSpec((1, 1, S), lambda b, c: (b, 0, 0)),
                  pl.BlockSpec((1, 1, S), lambda b, c: (b, 0, 0)),
                  pl.BlockSpec((8, S), const2), pl.BlockSpec((8, S), const2),
                  pl.BlockSpec((GW, 2 * S), const2),
                  pl.BlockSpec((S, GW), const2), pl.BlockSpec((S, GW), const2),
                  pl.BlockSpec((1, GW), const2),
                  pl.BlockSpec((GW, GW), const2), pl.BlockSpec((1, GW), const2), pl.BlockSpec((1, GW), const2)],
        out_specs=[pl.BlockSpec((tc, GW), lambda b, c: (b * nct + c, 0)),
                   pl.BlockSpec((1, 1, S), lambda b, c: (b, 0, 0)),
                   pl.BlockSpec((1, 1, S), lambda b, c: (b, 0, 0))],
        out_shape=[jax.ShapeDtypeStruct((nseq * t, GW), F32),
                   jax.ShapeDtypeStruct((nseq, 1, S), F32), jax.ShapeDtypeStruct((nseq, 1, S), F32)],
        scratch_shapes=[pltpu.VMEM((tc, 2 * S), F32), pltpu.VMEM((tc, S), F32), pltpu.VMEM((tc, S), F32),
                        pltpu.VMEM((1, S), F32), pltpu.VMEM((1, S), F32)],
        compiler_params=_cparams("parallel", "arbitrary"),
        name="s5",
    )(z, h0_re.reshape(nseq, 1, S), h0_im.reshape(nseq, 1, S), pw_re, pw_im, bblk, c_re, c_im,
      d_skip, w_glu, b_glu, out_norm)
    return outs[0], outs[1].reshape(nseq, S), outs[2].reshape(nseq, S)


def _sbprep_kernel(q_ref, k_ref, v_ref, gmat_ref, qg_ref, kg_ref, qb_ref, kn_ref, kb_ref, vb_ref):
    gmat = gmat_ref[...]

    def head_rms(x, gain):
        ms = _dot2_exact_rhs(x * x, gmat) * (1.0 / HD)
        return x * lax.rsqrt(ms + RMS_EPS) * gain

    qb_ref[...] = _bf(head_rms(q_ref[...], qg_ref[...]) * (HD ** -0.5))
    kn = head_rms(k_ref[...], kg_ref[...])
    kn_ref[...] = kn
    kb_ref[...] = _bf(kn)
    vb_ref[...] = _bf(v_ref[...])


def _sbprep_call(z, gmat, q_gain, k_gain, tm=512):
    n = z.shape[0]
    row = pl.BlockSpec((tm, GW), lambda i: (i, 0))
    return pl.pallas_call(
        _sbprep_kernel,
        grid=(n // tm,),
        in_specs=[pl.BlockSpec((tm, GW), lambda i: (i, 3)), pl.BlockSpec((tm, GW), lambda i: (i, 4)),
                  pl.BlockSpec((tm, GW), lambda i: (i, 5)),
                  pl.BlockSpec((GW, GW), lambda i: (0, 0)),
                  pl.BlockSpec((1, GW), lambda i: (0, 0)), pl.BlockSpec((1, GW), lambda i: (0, 0))],
        out_specs=[row, row, row, row],
        out_shape=[jax.ShapeDtypeStruct((n, GW), BF16), jax.ShapeDtypeStruct((n, GW), F32),
                   jax.ShapeDtypeStruct((n, GW), BF16), jax.ShapeDtypeStruct((n, GW), BF16)],
        compiler_params=_cparams("parallel"),
        name="sbprep",
    )(z, z, z, gmat, q_gain, k_gain)


def _sb_tiles(qhs, k2s, v2s, carries, umat, ones, mask):
    d = functools.partial(jnp.dot, preferred_element_type=F32)
    n = len(qhs)
    zs = [lax.dot_general(qhs[p], k2s[p], _NT, preferred_element_type=F32) for p in range(n)]
    lks, afts, tots = [], [], []
    for p in range(n):
        z = zs[p]
        lk = -(jnp.maximum(z, 0.0) + jnp.log(1.0 + jnp.exp(-jnp.abs(z))))
        if mask is not None:
            lk = jnp.where(mask, lk, 0.0)
        lh, ll = _split(lk)
        lks.append(lk)
        aft = d(lh, umat) + d(ll, umat)
        afts.append(aft)
        if ones is None:
            tots.append(jnp.broadcast_to(aft[:, 0:1] + lk[:, 0:1], (lk.shape[0], 2 * HD)))
        else:
            tots.append(d(lh, ones) + d(ll, ones))
    pvs = []
    for p in range(n):
        e = zs[p] + lks[p] + afts[p]
        if carries[p] is not None:
            e = e + carries[p]
        w = jnp.exp(e)
        if mask is not None:
            w = jnp.where(mask, w, 0.0)
        pvs.append(d(_bf(w), v2s[p]))
    return pvs, tots


def _sb_kernel(q_ref, kd_ref, vd_ref, kp_ref, vp_ref, on_ref, o_ref, qh_ref, acc_ref, carry_ref,
               *, tq, tk, npast, past_from_grid):
    PW = 2 * HD
    npairs = GW // PW
    m0 = _iota2(1, PW, 1) < HD
    row = _iota2(2 * tq, tq, 0)
    causal = _iota2(2 * tq, tq, 1) < jnp.where(row >= tq, row - tq, row)

    ud = jnp.where(_iota2(tq, tq, 0) > _iota2(tq, tq, 1), 1.0, 0.0).astype(BF16)
    up = jnp.where(_iota2(tk, tk, 0) > _iota2(tk, tk, 1), 1.0, 0.0).astype(BF16)
    ones_d = jnp.ones((tq, PW), BF16)
    ones_p = jnp.ones((tk, PW), BF16)
    pairs = [slice(p * PW, (p + 1) * PW) for p in range(npairs)]

    qhs = [_bf(_hat(q_ref[:, sl], m0)) for sl in pairs]
    pvs, tots = _sb_tiles(qhs, [_bf(kd_ref[:, sl]) for sl in pairs], [_bf(vd_ref[:, sl]) for sl in pairs],
                          [None] * npairs, ud, ones_d, causal)
    for p in range(npairs):
        qh_ref[p] = qhs[p]
        acc_ref[p] = pvs[p]
        carry_ref[p] = tots[p]

    nblk = pl.program_id(1) * (tq // tk) if past_from_grid else npast

    def body(i, _):
        j = nblk - 1 - i
        r0 = pl.multiple_of(j * tk, tk)
        pvs, tots = _sb_tiles([qh_ref[p] for p in range(npairs)],
                              [_bf(kp_ref[pl.ds(r0, tk), sl]) for sl in pairs],
                              [_bf(vp_ref[pl.ds(r0, tk), sl]) for sl in pairs],
                              [carry_ref[p] for p in range(npairs)], up, None, None)
        for p in range(npairs):
            acc_ref[p] += pvs[p]
            carry_ref[p] += tots[p]
        return 0

    lax.fori_loop(0, nblk, body, 0)

    outs = []
    ss = jnp.zeros((tq, 1), F32)
    for p in range(npairs):
        a = acc_ref[p]
        o = jnp.where(m0, a[:tq], a[tq:])
        ss = ss + jnp.sum(o * o, axis=-1, keepdims=True)
        outs.append(o)
    scale = lax.rsqrt(ss * (1.0 / GW) + RMS_EPS)
    for p in range(npairs):
        sl = slice(p * PW, (p + 1) * PW)
        o_ref[:, sl] = outs[p] * scale * on_ref[:, sl]


def _sb_call(qn, kn, z, kpast, vpast, row0, nseq, t, tq, tk, past_len, v_col_blk, out_norm, layer=0):
    nq = t // tq
    blk0 = row0 // tq
    npairs = GW // (2 * HD)
    if kpast is None:
        kp_arr, vp_arr = kn, z
        sblk = row0 // t
        kp_spec = pl.BlockSpec((t, GW), lambda b, i: (sblk + b, 0))
        vp_spec = pl.BlockSpec((t, GW), lambda b, i: (sblk + b, v_col_blk))
        npast, from_grid = 0, True
    else:
        kp_arr, vp_arr = kpast, vpast
        kp_spec = pl.BlockSpec((None, None, past_len, GW), lambda b, i: (layer, b, 0, 0))
        vp_spec = kp_spec
        npast, from_grid = past_len // tk, False
    qrow = lambda b, i: (blk0 + b * nq + i, 0)
    return pl.pallas_call(
        functools.partial(_sb_kernel, tq=tq, tk=tk, npast=npast, past_from_grid=from_grid),
        grid=(nseq, nq),
        in_specs=[pl.BlockSpec((tq, GW), qrow), pl.BlockSpec((tq, GW), qrow),
                  pl.BlockSpec((tq, GW), lambda b, i: (blk0 + b * nq + i, v_col_blk)),
                  kp_spec, vp_spec, pl.BlockSpec((1, GW), lambda b, i: (0, 0))],
        out_specs=pl.BlockSpec((tq, GW), lambda b, i: (b * nq + i, 0)),
        out_shape=jax.ShapeDtypeStruct((nseq * t, GW), F32),
        scratch_shapes=[pltpu.VMEM((npairs, 2 * tq, 2 * HD), BF16), pltpu.VMEM((npairs, 2 * tq, 2 * HD), F32),
                        pltpu.VMEM((npairs, 2 * tq, 2 * HD), F32)],
        compiler_params=_cparams("parallel", "arbitrary"),
        name="sb_attn",
    )(qn, kn, z, kp_arr, vp_arr, out_norm)


def _rmsgain_kernel(x_ref, g_ref, o_ref):
    o_ref[...] = _rms_rows(x_ref[...], g_ref[...])


def _rmsgain_call(x, gain, tm=512):
    n, d = x.shape
    return pl.pallas_call(
        _rmsgain_kernel,
        grid=(n // tm,),
        in_specs=[pl.BlockSpec((tm, d), lambda i: (i, 0)), pl.BlockSpec((1, d), lambda i: (0, 0))],
        out_specs=pl.BlockSpec((tm, d), lambda i: (i, 0)),
        out_shape=jax.ShapeDtypeStruct((n, d), F32),
        compiler_params=_cparams("parallel"),
        name="rmsgain",
    )(x, gain)


def _rwkv_prep_kernel(zr_ref, zk_ref, zv_ref, zwa_ref, zg_ref, first_ref, mu_ref, w0_ref, w2_ref, a0_ref, a2_ref,
                      g2_ref, kk_ref, ka_ref, rk_ref, gmat_ref,
                      r_out, lw_out, k_out, v_out, kk_out, b_out, g_out, bonus_out, prev_ref, *, tm, wr):
    c = pl.program_id(1)
    cols = prev_ref.shape[1]

    @pl.when(c == 0)
    def _():
        prev_ref[...] = first_ref[0]

    first_row = _iota2(tm, 1, 0) == 0

    def shifted(ref, lo, width):
        x = ref[...]
        prev = jnp.where(first_row, prev_ref[:, lo:lo + width], pltpu.roll(x, 1, 0))
        return x + (prev - x) * mu_ref[:, lo:lo + width]

    r = shifted(zr_ref, 0, GW)
    k = shifted(zk_ref, GW, GW)
    v = shifted(zv_ref, 2 * GW, GW)
    wa = shifted(zwa_ref, 3 * GW, 2 * wr)
    gl = shifted(zg_ref, 3 * GW + 2 * wr, cols - 3 * GW - 2 * wr)
    for ref, lo in ((zr_ref, 0), (zk_ref, GW), (zv_ref, 2 * GW), (zwa_ref, 3 * GW), (zg_ref, 3 * GW + 2 * wr)):
        prev_ref[:, lo:lo + ref.shape[1]] = ref[tm - 1:tm, :]

    xw = w0_ref[...] + _dot(jnp.tanh(wa), w2_ref[...])
    w_log = -(jnp.maximum(-xw, 0.0) + jnp.log(1.0 + jnp.exp(-jnp.abs(xw)))) - 0.5
    a = jax.nn.sigmoid(a0_ref[...] + _dot(wa, a2_ref[...]))
    g_out[...] = _dot(jax.nn.sigmoid(gl), g2_ref[...])
    gmat = gmat_ref[...]
    kk = k * kk_ref[...]
    kk = kk / jnp.maximum(jnp.sqrt(_dot2_exact_rhs(kk * kk, gmat)), 1e-12)
    k = k * (1.0 + (a - 1.0) * ka_ref[...])
    r_out[...] = r
    lw_out[...] = -jnp.exp(w_log)
    k_out[...] = k
    v_out[...] = v
    kk_out[...] = kk
    b_out[...] = kk * a
    bonus_out[...] = _dot2_exact_rhs(r * k * rk_ref[...], gmat) * v


def _rwkv_prep_call(z, row0, nseq, t, tm, first, mu, w0, w2p, a0, a2p, g2, k_k, k_a, r_k, gmat, wr):
    nct = t // tm
    blk0 = row0 // tm
    cols = mu.shape[-1]
    zrow = lambda blkw, off: pl.BlockSpec((tm, blkw), lambda b, c: (blk0 + b * nct + c, off))
    c0 = (z.shape[1] - cols)
    const = lambda shp: pl.BlockSpec(shp, lambda b, c: (0,) * len(shp))
    orow = pl.BlockSpec((tm, GW), lambda b, c: (b * nct + c, 0))
    return pl.pallas_call(
        functools.partial(_rwkv_prep_kernel, tm=tm, wr=wr),
        grid=(nseq, nct),
        in_specs=[zrow(GW, c0 // GW), zrow(GW, c0 // GW + 1), zrow(GW, c0 // GW + 2),
                  zrow(2 * wr, (c0 + 3 * GW) // (2 * wr)), zrow(cols - 3 * GW - 2 * wr, (c0 + 3 * GW) // (2 * wr) + 1),
                  pl.BlockSpec((1, 1, cols), lambda b, c: (b, 0, 0)),
                  const((1, cols)), const((1, GW)), const((2 * wr, GW)), const((1, GW)), const((2 * wr, GW)),
                  const((cols - 3 * GW - 2 * wr, GW)), const((1, GW)), const((1, GW)), const((1, GW)),
                  const((GW, GW))],
        out_specs=[orow] * 8,
        out_shape=[jax.ShapeDtypeStruct((nseq * t, GW), F32)] * 8,
        scratch_shapes=[pltpu.VMEM((1, cols), F32)],
        compiler_params=_cparams("parallel", "arbitrary"),
        name="rwkv_prep",
    )(z, z, z, z, z, first.reshape(nseq, 1, cols), mu, w0, w2p, a0, a2p, g2, k_k, k_a, r_k, gmat)


def _hat(x, m0):
    return jnp.concatenate([jnp.where(m0, x, 0.0), jnp.where(m0, 0.0, x)], axis=0)


def _rwkv_chunk_kernel(r_ref, lw_ref, k_ref, v_ref, kk_ref, b_ref, rt_out, p3_out, m_out, n_out, *, group):
    C = RWKV_CHUNK
    C2 = 2 * C
    lane = _iota2(1, 2 * HD, 1)
    m0 = lane < HD
    ri = _iota2(C2, C2, 0)
    ci = _iota2(C2, C2, 1)
    same = (ri < C) == (ci < C)
    strict = jnp.logical_and(same, ri > ci)
    incl = jnp.logical_and(same, ri >= ci)
    ltri = jnp.where(_iota2(C, C, 0) >= _iota2(C, C, 1), 1.0, 0.0).astype(BF16)
    eye = _iota2(2 * HD, 2 * HD, 0) == _iota2(2 * HD, 2 * HD, 1)
    eye2 = ri == ci

    def pair_stages(p):
        sl = slice(p * 2 * HD, (p + 1) * 2 * HD)
        lw = lw_ref[:, sl]
        cl = _dot2_exact_lhs(ltri, lw)
        yield
        clast = cl[C - 1:C, :]
        kkt = _hat(kk_ref[:, sl] * jnp.exp(cl - lw), m0)
        rt = _hat(r_ref[:, sl] * jnp.exp(cl), m0)
        einv = jnp.exp(-cl)
        kb = _split(_hat(k_ref[:, sl] * einv, m0))
        bb = _split(_hat(b_ref[:, sl] * einv, m0))
        efin = jnp.exp(clast - cl)
        kh = _split(_hat(k_ref[:, sl] * efin, m0))
        bh = _split(_hat(b_ref[:, sl] * efin, m0))
        vh = _split(_hat(v_ref[:, sl], m0))
        lhs = _split(jnp.concatenate([kkt, rt], axis=0))
        gk = _dot3s(lhs, kb, _NT)
        gb = _dot3s(lhs, bb, _NT)
        yield
        a_kk = jnp.where(strict, gk[:C2], 0.0)
        a_rk = _split(jnp.where(incl, gk[C2:], 0.0))
        a_kb = jnp.where(strict, gb[:C2], 0.0)
        a_rb = _split(jnp.where(incl, gb[C2:], 0.0))
        p1 = _dot3s(_split(a_kk), vh)
        tinv = jnp.where(eye2, 1.0, 0.0) - a_kb
        lp = _split(a_kb)
        n = 2
        while n < C:
            yield
            lpf = _dot3s(lp, lp)
            lp = _split(lpf)
            tinv = tinv + _dot3s(_split(tinv), lp)
            n *= 2
        yield
        x = _dot3s(_split(tinv), _split(jnp.concatenate([kkt, p1], axis=1)))
        yield
        xs = _split(x)
        kt = (xs[0][:, :2 * HD], xs[1][:, :2 * HD])
        p2 = (xs[0][:, 2 * HD:], xs[1][:, 2 * HD:])
        y2 = _dot3s(a_rb, xs)
        p3 = _dot3s(a_rk, vh)
        mm = _dot3s(kt, bh, _TN)
        nn = _dot3s(vh, kh, _TN) - _dot3s(p2, bh, _TN)
        yield
        rt_out[p] = rt - y2[:, :2 * HD]
        p3_out[p] = p3 - y2[:, 2 * HD:]
        m_out[p] = jnp.where(eye, jnp.exp(clast), 0.0) - mm
        n_out[p] = nn

    npairs = GW // (2 * HD)
    for p0 in range(0, npairs, group):
        gens = [pair_stages(p) for p in range(p0, p0 + group)]
        while gens:
            for g in list(gens):
                try:
                    next(g)
                except StopIteration:
                    gens.remove(g)


def _rwkv_chunk_call(r, lw, k, v, kk, b):
    n = r.shape[0]
    C = RWKV_CHUNK
    nc = n // C
    npairs = GW // (2 * HD)
    row = pl.BlockSpec((C, GW), lambda i: (i, 0))
    blk = lambda rows: pl.BlockSpec((None, npairs, rows, 2 * HD), lambda i: (i, 0, 0, 0))
    shp = lambda rows: jax.ShapeDtypeStruct((nc, npairs, rows, 2 * HD), F32)
    return pl.pallas_call(
        functools.partial(_rwkv_chunk_kernel, group=4),
        grid=(nc,),
        in_specs=[row] * 6,
        out_specs=[blk(2 * C), blk(2 * C), blk(2 * HD), blk(2 * HD)],
        out_shape=[shp(2 * C), shp(2 * C), shp(2 * HD), shp(2 * HD)],
        compiler_params=_cparams("parallel"),
        name="rwkv_chunk",
    )(r, lw, k, v, kk, b)


def _rwkv_seq_kernel(rt_ref, p3_ref, m_ref, n_ref, s0_ref, g_ref, bonus_ref, lnw_ref, lnb_ref, gmat_ref,
                     o_ref, s_out_ref, s_ref, *, nct):
    c = pl.program_id(1)
    C = RWKV_CHUNK

    @pl.when(c == 0)
    def _():
        s_ref[...] = s0_ref[0]

    gm = gmat_ref[0:2 * HD, 0:2 * HD]
    npairs = GW // (2 * HD)
    pairs = [slice(p * 2 * HD, (p + 1) * 2 * HD) for p in range(npairs)]
    ss = [_split(s_ref[p]) for p in range(npairs)]
    yhs = [_dot3s(_split(rt_ref[p]), ss[p], _NT) + p3_ref[p] for p in range(npairs)]
    snew = [_dot3s(ss[p], _split(m_ref[p])) + n_ref[p] for p in range(npairs)]
    ys = [yh[:C] + yh[C:] for yh in yhs]
    means = [_dot2_exact_rhs(y, gm) * (1.0 / HD) for y in ys]
    ds = [y - m for y, m in zip(ys, means)]
    vrs = [_dot2_exact_rhs(d * d, gm) * (1.0 / HD) for d in ds]
    for p, sl in enumerate(pairs):
        yn = ds[p] * lax.rsqrt(vrs[p] + GN_EPS) * lnw_ref[:, sl] + lnb_ref[:, sl]
        o_ref[:, sl] = (yn + bonus_ref[:, sl]) * g_ref[:, sl]
        s_ref[p] = snew[p]

    @pl.when(c == nct - 1)
    def _():
        s_out_ref[0] = s_ref[...]


def _rwkv_seq_call(rt, p3, m, n, chunk0, nseq, nct, s0, g, bonus, row0, lnx_w, lnx_b, gmat):
    C = RWKV_CHUNK
    npairs = GW // (2 * HD)
    blk0 = row0 // C
    cblk = lambda rows: pl.BlockSpec((None, npairs, rows, 2 * HD), lambda b, c: (chunk0 + b * nct + c, 0, 0, 0))
    sblk = pl.BlockSpec((1, npairs, 2 * HD, 2 * HD), lambda b, c: (b, 0, 0, 0))
    rowin = pl.BlockSpec((C, GW), lambda b, c: (blk0 + b * nct + c, 0))
    const = lambda shp: pl.BlockSpec(shp, lambda b, c: (0,) * len(shp))
    return pl.pallas_call(
        functools.partial(_rwkv_seq_kernel, nct=nct),
        grid=(nseq, nct),
        in_specs=[cblk(2 * C), cblk(2 * C), cblk(2 * HD), cblk(2 * HD), sblk, rowin, rowin,
                  const((1, GW)), const((1, GW)), const((GW, GW))],
        out_specs=[pl.BlockSpec((C, GW), lambda b, c: (b * nct + c, 0)), sblk],
        out_shape=[jax.ShapeDtypeStruct((nseq * nct * C, GW), F32),
                   jax.ShapeDtypeStruct((nseq, npairs, 2 * HD, 2 * HD), F32)],
        scratch_shapes=[pltpu.VMEM((npairs, 2 * HD, 2 * HD), F32)],
        compiler_params=_cparams("parallel", "arbitrary"),
        name="rwkv_seq",
    )(rt, p3, m, n, s0, g, bonus, lnx_w, lnx_b, gmat)


def _pair_states(s):
    b, h = s.shape[:2]
    s = s.reshape(b, h // 2, 2, HD, HD)
    z = jnp.zeros_like(s[:, :, 0])
    top = jnp.concatenate([s[:, :, 0], z], axis=-1)
    bot = jnp.concatenate([z, s[:, :, 1]], axis=-1)
    return jnp.concatenate([top, bot], axis=-2)


def _unpair_states(s2):
    b, hp = s2.shape[:2]
    return jnp.stack([s2[:, :, :HD, :HD], s2[:, :, HD:, HD:]], axis=2).reshape(b, 2 * hp, HD, HD)


def _forward(xp, xs, cache_k, cache_v, s5_re0, s5_im0, wkv0, shift0, W):
    Bp, Tp, D = xp.shape
    Bs, Ts, _ = xs.shape
    depth = W['ln_ffn1'].shape[0]
    past_len = cache_k.shape[2]
    Np, Ns = Bp * Tp, Bs * Ts
    n_heads_d = GW // HD
    wr = W['rwkv_w2'].shape[1]
    dcols = W['rwkv_mu'].shape[-1]

    x = jnp.concatenate([xp.reshape(Np, D), xs.reshape(Ns, D)], axis=0)
    cache_kb = _bf(cache_k.reshape(depth, Bs, past_len, GW))
    cache_vb = _bf(cache_v.reshape(depth, Bs, past_len, GW))
    wbf = {k: _bf(W[k]) for k in ('w_ffn1_gate', 'w_ffn1_up', 'w_ffn1_down', 'w_in', 'w_out',
                                  'w_ffn2_gate', 'w_ffn2_up', 'w_ffn2_down')}
    hid = lax.broadcasted_iota(jnp.int32, (GW, GW), 0) // HD
    gmat = (hid == jnp.transpose(hid)).astype(BF16)
    grp = lax.broadcasted_iota(jnp.int32, (GW, S5_STATE), 0) // (GW // 32) == \
        lax.broadcasted_iota(jnp.int32, (GW, S5_STATE), 1) // (S5_STATE // 32)
    zeros_first = jnp.zeros((Bp, dcols), F32)
    zeros_s5 = jnp.zeros((Bp, S5_STATE), F32)
    zeros_wkv = jnp.zeros((Bp, n_heads_d // 2, 2 * HD, 2 * HD), F32)

    h = _rms_call(x, W['ln_ffn1'][0][None])
    outs = {k: [] for k in ('kp', 'vp', 'ks', 'vs', 's5rp', 's5ip', 's5rs', 's5is', 'wkvp', 'wkvs', 'shp', 'shs', 'gv')}
    for l in range(depth):
        g1 = lambda name: W[name][l][None]
        x, h = _ffn_call(x, h, wbf['w_ffn1_gate'], wbf['w_ffn1_up'], wbf['w_ffn1_down'], g1('ln_mix'), l)
        z = _inproj_call(h, wbf['w_in'], l)

        oa_p, _ = _gmlp_call(z, 0, Np, min(Tp, GMLP_CHUNK), g1('gmlp_v_norm'), W['gmlp_ws'][l], W['gmlp_b'][l],
                             g1('out_norm_a'))
        oa_s, gv_s = _gmlp_call(z, Np, Ns, min(Ts, GMLP_CHUNK), g1('gmlp_v_norm'), W['gmlp_ws'][l], W['gmlp_b'][l],
                                g1('out_norm_a'))
        oa = jnp.concatenate([oa_p, oa_s], axis=0)

        pw_re, pw_im, bb_re, bb_im = _s5_params(W['s5_lam_re'][l], W['s5_lam_im'][l], W['s5_log_dt'][l],
                                                W['s5_b_re'][l], W['s5_b_im'][l])
        bblk = jnp.concatenate([jnp.where(grp, jnp.tile(bb_re, (32, 1)), 0.0),
                                jnp.where(grp, jnp.tile(bb_im, (32, 1)), 0.0)], axis=1).astype(BF16)
        grp_t = jnp.transpose(grp)
        cre = jnp.where(grp_t, jnp.tile(jnp.transpose(W['s5_c_re'][l], (0, 2, 1)).reshape(S5_STATE, -1), (1, 32)),
                        0.0).astype(BF16)
        cim = jnp.where(grp_t, jnp.tile(jnp.transpose(W['s5_c_im'][l], (0, 2, 1)).reshape(S5_STATE, -1), (1, 32)),
                        0.0).astype(BF16)
        s5_args = (pw_re, pw_im, bblk, cre, cim, g1('s5_d'), W['s5_w_glu'][l], g1('s5_b_glu'), g1('out_norm_b'))
        ob_p, s5rp, s5ip = _s5_call(z, 0, Bp, Tp, zeros_s5, zeros_s5, *s5_args, col_blk=2, tc=min(Tp, 256))
        ob_s, s5rs, s5is = _s5_call(z, Np, Bs, Ts, s5_re0[l].reshape(Bs, S5_STATE), s5_im0[l].reshape(Bs, S5_STATE),
                                    *s5_args, col_blk=2, tc=min(Ts, 256))
        ob = jnp.concatenate([ob_p, ob_s], axis=0)

        qg = jnp.tile(W['sb_q_norm'][l], GW // HD)[None]
        kg = jnp.tile(W['sb_k_norm'][l], GW // HD)[None]
        qb, kn, kb, vb = _sbprep_call(z, gmat, qg, kg)
        tq = min(Tp, 128)
        oc_p = _sb_call(qb, kb, vb, None, None, 0, Bp, Tp, tq, tq, 0, 0, g1('out_norm_c'))
        oc_s = _sb_call(qb, kb, vb, cache_kb, cache_vb, Np, Bs, Ts, Ts, min(past_len, 128), past_len, 0,
                        g1('out_norm_c'), layer=l)
        oc = jnp.concatenate([oc_p, oc_s], axis=0)
        vcol = z[:, 5 * GW:6 * GW]

        w2p = jnp.concatenate([W['rwkv_w2'][l], jnp.zeros_like(W['rwkv_a2'][l])], axis=0)
        a2p = jnp.concatenate([jnp.zeros_like(W['rwkv_w2'][l]), W['rwkv_a2'][l]], axis=0)
        rk = W['rwkv_r_k'][l].reshape(1, GW)
        prep_args = (g1('rwkv_mu'), g1('rwkv_w0'), w2p, g1('rwkv_a0'), a2p, W['rwkv_g2'][l], g1('rwkv_k_k'),
                     g1('rwkv_k_a'), rk, gmat, wr)
        pp = _rwkv_prep_call(z, 0, Bp, Tp, min(Tp, 512), zeros_first, *prep_args)
        ps = _rwkv_prep_call(z, Np, Bs, Ts, Ts, shift0[l], *prep_args)
        r_, lw_, k_, v_, kk_, b_, g_, bonus_ = [jnp.concatenate([a, b], axis=0) for a, b in zip(pp, ps)]
        rt, p3, mm, nn = _rwkv_chunk_call(r_, lw_, k_, v_, kk_, b_)
        seq_args = (g1('rwkv_lnx_w'), g1('rwkv_lnx_b'), gmat)
        od_p, wkv_p = _rwkv_seq_call(rt, p3, mm, nn, 0, Bp, Tp // RWKV_CHUNK, zeros_wkv, g_, bonus_, 0, *seq_args)
        od_s, wkv_s = _rwkv_seq_call(rt, p3, mm, nn, Np // RWKV_CHUNK, Bs, Ts // RWKV_CHUNK, _pair_states(wkv0[l]),
                                     g_, bonus_, Np, *seq_args)
        od = jnp.concatenate([od_p, od_s], axis=0)

        gain_next = g1('ln_ffn2')
        x, h = _outproj_call(x, oa, ob, oc, od, wbf['w_out'], gain_next, l)
        gain_next = W['ln_ffn1'][l + 1][None] if l + 1 < depth else g1('ln_ffn2')
        x, h = _ffn_call(x, h, wbf['w_ffn2_gate'], wbf['w_ffn2_up'], wbf['w_ffn2_down'], gain_next, l)

        nh = GW // HD
        outs['kp'].append(kn[:Np].reshape(Bp, Tp, nh, HD))
        outs['vp'].append(vcol[:Np].reshape(Bp, Tp, nh, HD))
        outs['ks'].append(kn[Np:].reshape(Bs, Ts, nh, HD))
        outs['vs'].append(vcol[Np:].reshape(Bs, Ts, nh, HD))
        outs['s5rp'].append(s5rp.reshape(Bp, 32, -1))
        outs['s5ip'].append(s5ip.reshape(Bp, 32, -1))
        outs['s5rs'].append(s5rs.reshape(Bs, 32, -1))
        outs['s5is'].append(s5is.reshape(Bs, 32, -1))
        outs['wkvp'].append(_unpair_states(wkv_p))
        outs['wkvs'].append(_unpair_states(wkv_s))
        zd = z[:, z.shape[1] - dcols:]
        outs['shp'].append(zd[:Np].reshape(Bp, Tp, dcols)[:, -1])
        outs['shs'].append(zd[Np:].reshape(Bs, Ts, dcols)[:, -1])
        outs['gv'].append(gv_s.reshape(Bs, Ts, GW))

    st = lambda k: jnp.stack(outs[k], axis=0)
    return (x[:Np].reshape(Bp, Tp, D), x[Np:].reshape(Bs, Ts, D),
            st('kp'), st('vp'), st('ks'), st('vs'),
            st('s5rp'), st('s5ip'), st('s5rs'), st('s5is'),
            st('wkvp'), st('wkvs'), st('shp'), st('shs'), st('gv'))


_WEIGHT_NAMES = ('ln_ffn1', 'w_ffn1_gate', 'w_ffn1_up', 'w_ffn1_down', 'ln_mix', 'w_in',
                 'gmlp_v_norm', 'gmlp_ws', 'gmlp_b', 'out_norm_a',
                 's5_lam_re', 's5_lam_im', 's5_log_dt', 's5_b_re', 's5_b_im', 's5_c_re', 's5_c_im', 's5_d',
                 's5_w_glu', 's5_b_glu', 'out_norm_b',
                 'sb_q_norm', 'sb_k_norm', 'out_norm_c',
                 'rwkv_mu', 'rwkv_w0', 'rwkv_w2', 'rwkv_a0', 'rwkv_a2', 'rwkv_g2', 'rwkv_k_k', 'rwkv_k_a', 'rwkv_r_k',
                 'rwkv_lnx_w', 'rwkv_lnx_b',
                 'w_out', 'ln_ffn2', 'w_ffn2_gate', 'w_ffn2_up', 'w_ffn2_down')


def kernel(x_prompt, x_sample, cache_sb_k, cache_sb_v, state_s5_re, state_s5_im, state_rwkv_wkv, state_rwkv_shift,
           ln_ffn1, w_ffn1_gate, w_ffn1_up, w_ffn1_down, ln_mix, w_in,
           gmlp_v_norm, gmlp_ws, gmlp_b, out_norm_a,
           s5_lam_re, s5_lam_im, s5_log_dt, s5_b_re, s5_b_im, s5_c_re, s5_c_im, s5_d, s5_w_glu, s5_b_glu, out_norm_b,
           sb_q_norm, sb_k_norm, out_norm_c,
           rwkv_mu, rwkv_w0, rwkv_w2, rwkv_a0, rwkv_a2, rwkv_g2, rwkv_k_k, rwkv_k_a, rwkv_r_k, rwkv_lnx_w, rwkv_lnx_b,
           w_out, ln_ffn2, w_ffn2_gate, w_ffn2_up, w_ffn2_down):
    weights = (ln_ffn1, w_ffn1_gate, w_ffn1_up, w_ffn1_down, ln_mix, w_in,
               gmlp_v_norm, gmlp_ws, gmlp_b, out_norm_a,
               s5_lam_re, s5_lam_im, s5_log_dt, s5_b_re, s5_b_im, s5_c_re, s5_c_im, s5_d, s5_w_glu, s5_b_glu,
               out_norm_b, sb_q_norm, sb_k_norm, out_norm_c,
               rwkv_mu, rwkv_w0, rwkv_w2, rwkv_a0, rwkv_a2, rwkv_g2, rwkv_k_k, rwkv_k_a, rwkv_r_k, rwkv_lnx_w,
               rwkv_lnx_b, w_out, ln_ffn2, w_ffn2_gate, w_ffn2_up, w_ffn2_down)
    W = dict(zip(_WEIGHT_NAMES, weights))
    return _forward(x_prompt, x_sample, cache_sb_k, cache_sb_v, state_s5_re, state_s5_im, state_rwkv_wkv,
                    state_rwkv_shift, W)
```

```python
import functools

import jax
import jax.numpy as jnp
from jax import lax
from jax.experimental import pallas as pl
from jax.experimental.pallas import tpu as pltpu

F32 = jnp.float32
BF16 = jnp.bfloat16

RMS_EPS = 1e-6
GN_EPS = 64e-5
A_HEADS = 4
GMLP_CHUNK = 128
HD = 64
GW = 512
S5_STATE = 2048
RWKV_CHUNK = 64
SB_TQ, SB_TK = 256, 128
SB_TK_SAMPLE = 256
VMEM_LIMIT = 56 * 1024 * 1024


def _cparams(*sem):
    return pltpu.CompilerParams(dimension_semantics=sem, vmem_limit_bytes=VMEM_LIMIT)


def _bf(x):
    return x.astype(BF16)


_NN = (((1,), (0,)), ((), ()))
_NT = (((1,), (1,)), ((), ()))
_TN = (((0,), (0,)), ((), ()))


def _dot(a, b, dims=_NN):
    return lax.dot_general(_bf(a), _bf(b), dims, preferred_element_type=F32)


def _split(x):
    hi = _bf(x)
    lo = _bf(x - hi.astype(F32))
    return hi, lo


def _dot3(a, b, dims=_NN):
    ah, al = _split(a)
    bh, bl = _split(b)
    d = functools.partial(lax.dot_general, dimension_numbers=dims, preferred_element_type=F32)
    return d(ah, bh) + (d(ah, bl) + d(al, bh))


def _dot3s(a, b, dims=_NN):
    d = functools.partial(lax.dot_general, dimension_numbers=dims, preferred_element_type=F32)
    return d(a[0], b[0]) + (d(a[0], b[1]) + d(a[1], b[0]))


def _dot2_exact_rhs(a, b_bf, dims=_NN):
    ah, al = _split(a)
    d = functools.partial(lax.dot_general, dimension_numbers=dims, preferred_element_type=F32)
    return d(ah, b_bf) + d(al, b_bf)


def _dot2_exact_lhs(a_bf, b, dims=_NN):
    bh, bl = _split(b)
    d = functools.partial(lax.dot_general, dimension_numbers=dims, preferred_element_type=F32)
    return d(a_bf, bh) + d(a_bf, bl)


def _rms_rows(x, gain):
    ms = jnp.mean(x * x, axis=-1, keepdims=True)
    return x * lax.rsqrt(ms + RMS_EPS) * gain


def _iota2(n, m, axis):
    return lax.broadcasted_iota(jnp.int32, (n, m), axis)


def _rms_kernel(x_ref, g_ref, h_ref):
    h_ref[...] = _bf(_rms_rows(x_ref[...], g_ref[...]))


def _rms_call(x, gain, tm=512):
    n, d = x.shape
    return pl.pallas_call(
        _rms_kernel,
        grid=(n // tm,),
        in_specs=[pl.BlockSpec((tm, d), lambda i: (i, 0)), pl.BlockSpec((1, d), lambda i: (0, 0))],
        out_specs=pl.BlockSpec((tm, d), lambda i: (i, 0)),
        out_shape=jax.ShapeDtypeStruct((n, d), BF16),
        compiler_params=_cparams("parallel"),
        name="rms",
    )(x, gain)


def _ffn_kernel(x_ref, h_ref, wg_ref, wu_ref, wd_ref, gn_ref, o_ref, hn_ref, *, nj):
    j = pl.program_id(1)

    @pl.when(j == 0)
    def _():
        o_ref[...] = jnp.zeros_like(o_ref)

    h = h_ref[...]
    g = jnp.dot(h, wg_ref[...], preferred_element_type=F32)
    u = jnp.dot(h, wu_ref[...], preferred_element_type=F32)
    a = _bf(g * jax.nn.sigmoid(g) * u)
    o_ref[...] += jnp.dot(a, wd_ref[...], preferred_element_type=F32)

    @pl.when(j == nj - 1)
    def _():
        y = x_ref[...] + 0.5 * o_ref[...]
        o_ref[...] = y
        hn_ref[...] = _bf(_rms_rows(y, gn_ref[...]))


def _ffn_call(x, h, wg, wu, wd, gain_next, layer, tm=512, tf=512):
    n, d = x.shape
    ff = wg.shape[-1]
    nj = ff // tf
    return pl.pallas_call(
        functools.partial(_ffn_kernel, nj=nj),
        grid=(n // tm, nj),
        in_specs=[
            pl.BlockSpec((tm, d), lambda i, j: (i, 0)),
            pl.BlockSpec((tm, d), lambda i, j: (i, 0)),
            pl.BlockSpec((None, d, tf), lambda i, j: (layer, 0, j)),
            pl.BlockSpec((None, d, tf), lambda i, j: (layer, 0, j)),
            pl.BlockSpec((None, tf, d), lambda i, j: (layer, j, 0)),
            pl.BlockSpec((1, d), lambda i, j: (0, 0)),
        ],
        out_specs=[pl.BlockSpec((tm, d), lambda i, j: (i, 0)), pl.BlockSpec((tm, d), lambda i, j: (i, 0))],
        out_shape=[jax.ShapeDtypeStruct((n, d), F32), jax.ShapeDtypeStruct((n, d), BF16)],
        compiler_params=_cparams("parallel", "arbitrary"),
        name="ffn",
    )(x, h, wg, wu, wd, gain_next)


def _inproj_kernel(h_ref, w_ref, z_ref):
    z_ref[...] = jnp.dot(h_ref[...], w_ref[...], preferred_element_type=F32)


def _inproj_call(h, w_in, layer, tm=512):
    n, d = h.shape
    cols = w_in.shape[-1]
    tn = cols // 2
    return pl.pallas_call(
        _inproj_kernel,
        grid=(2, n // tm),
        in_specs=[pl.BlockSpec((tm, d), lambda j, i: (i, 0)),
                  pl.BlockSpec((None, d, tn), lambda j, i: (layer, 0, j))],
        out_specs=pl.BlockSpec((tm, tn), lambda j, i: (i, j)),
        out_shape=jax.ShapeDtypeStruct((n, cols), F32),
        compiler_params=_cparams("parallel", "parallel"),
        name="inproj",
    )(h, w_in)


def _outproj_kernel(x_ref, *refs, n_first):
    first, second = refs[0:4], refs[4:8]
    w_ref, gn_ref, o_ref, hn_ref = refs[8:]

    def run(mix_refs):
        acc = x_ref[...]
        for i, r in enumerate(mix_refs):
            acc = acc + jnp.dot(_bf(r[...]), w_ref[i * GW:(i + 1) * GW, :], preferred_element_type=F32)
        o_ref[...] = acc
        hn_ref[...] = _bf(_rms_rows(acc, gn_ref[...]))

    i = pl.program_id(0)
    pl.when(i < n_first)(lambda: run(first))
    pl.when(i >= n_first)(lambda: run(second))


def _outproj_call(x, mix_first, mix_second, w_out, gain_next, layer, tm=256):
    n, d = x.shape
    n_first = mix_first[0].shape[0] // tm
    n_second = mix_second[0].shape[0] // tm
    row = lambda i: (i, 0)
    first_row = lambda i: (jnp.minimum(i, n_first - 1), 0)
    second_row = lambda i: (jnp.maximum(i - n_first, 0), 0)
    return pl.pallas_call(
        functools.partial(_outproj_kernel, n_first=n_first),
        grid=(n_first + n_second,),
        in_specs=[pl.BlockSpec((tm, d), row)] + [pl.BlockSpec((tm, GW), first_row)] * 4
        + [pl.BlockSpec((tm, GW), second_row)] * 4
        + [pl.BlockSpec((None, d, d), lambda i: (layer, 0, 0)), pl.BlockSpec((1, d), lambda i: (0, 0))],
        out_specs=[pl.BlockSpec((tm, d), row), pl.BlockSpec((tm, d), row)],
        out_shape=[jax.ShapeDtypeStruct((n, d), F32), jax.ShapeDtypeStruct((n, d), BF16)],
        compiler_params=_cparams("parallel"),
        name="outproj",
    )(x, *mix_first, *mix_second, w_out, gain_next)


def _gmlp_kernel(z_ref, vn_ref, ws_ref, bt_ref, on_ref, o_ref, v_ref, *, L):
    z = jax.nn.gelu(z_ref[...])
    causal = _iota2(L, L, 1) <= _iota2(L, L, 0)
    hw = GW // A_HEADS
    ss = jnp.zeros((L, 1), F32)
    outs = []
    for h in range(A_HEADS):
        vh = _rms_rows(z[:, GW + h * hw:GW + (h + 1) * hw], vn_ref[...])
        v_ref[:, h * hw:(h + 1) * hw] = vh
        w = jnp.where(causal, ws_ref[h], 0.0)
        s = _dot(w, vh) + bt_ref[:, h:h + 1]
        oh = z[:, h * hw:(h + 1) * hw] * s
        ss = ss + jnp.sum(oh * oh, axis=-1, keepdims=True)
        outs.append(oh)
    scale = lax.rsqrt(ss * (1.0 / GW) + RMS_EPS)
    for h in range(A_HEADS):
        o_ref[:, h * hw:(h + 1) * hw] = outs[h] * scale * on_ref[:, h * hw:(h + 1) * hw]


def _gmlp_call(z, row0, nrows, L, v_norm, ws, b, out_norm):
    nb = nrows // L
    b0 = row0 // L
    ws_l = ws[:, :L, :L]
    bt = jnp.transpose(b[:, :L])
    row = lambda i: (i, 0)
    return pl.pallas_call(
        functools.partial(_gmlp_kernel, L=L),
        grid=(nb,),
        in_specs=[pl.BlockSpec((L, 2 * GW), lambda i: (b0 + i, 0)),
                  pl.BlockSpec((1, GW // A_HEADS), lambda i: (0, 0)),
                  pl.BlockSpec((A_HEADS, L, L), lambda i: (0, 0, 0)),
                  pl.BlockSpec((L, A_HEADS), lambda i: (0, 0)),
                  pl.BlockSpec((1, GW), lambda i: (0, 0))],
        out_specs=[pl.BlockSpec((L, GW), row), pl.BlockSpec((L, GW), row)],
        out_shape=[jax.ShapeDtypeStruct((nrows, GW), F32), jax.ShapeDtypeStruct((nrows, GW), F32)],
        compiler_params=_cparams("parallel"),
        name="gmlp",
    )(z, v_norm, ws_l, bt, out_norm)


def _s5_param_kernel(lr_ref, li_ref, ldt_ref, brt_ref, bit_ref, pw_re_ref, pw_im_ref, bb_re_ref, bb_im_ref):
    lr = lr_ref[...]
    li = li_ref[...]
    dt = jnp.exp(ldt_ref[...])
    n = (_iota2(8, S5_STATE, 0) + 1).astype(F32)
    mag = jnp.exp(n * (lr * dt))
    ang = n * (li * dt)
    pw_re = mag * jnp.cos(ang)
    pw_im = mag * jnp.sin(ang)
    pw_re_ref[...] = pw_re
    pw_im_ref[...] = pw_im
    ab_re = pw_re[0:1, :]
    ab_im = pw_im[0:1, :]
    den = lr * lr + li * li
    nr, ni = ab_re - 1.0, ab_im
    cf_re = (nr * lr + ni * li) / den
    cf_im = (ni * lr - nr * li) / den
    br = brt_ref[...]
    bi = bit_ref[...]
    bb_re_ref[...] = cf_re * br - cf_im * bi
    bb_im_ref[...] = cf_re * bi + cf_im * br


def _s5_params(lam_re, lam_im, log_dt, b_re, b_im):
    g, p = lam_re.shape
    s = g * p
    ch = b_re.shape[-1]
    ldt = jnp.broadcast_to(log_dt[:, None], (g, p)).reshape(1, s)
    brt = jnp.transpose(b_re.reshape(s, ch))
    bit = jnp.transpose(b_im.reshape(s, ch))
    return pl.pallas_call(
        _s5_param_kernel,
        out_shape=[jax.ShapeDtypeStruct((8, s), F32), jax.ShapeDtypeStruct((8, s), F32),
                   jax.ShapeDtypeStruct((ch, s), F32), jax.ShapeDtypeStruct((ch, s), F32)],
        name="s5_params",
    )(lam_re.reshape(1, s), lam_im.reshape(1, s), ldt, brt, bit)


def _cmul(ar, ai, xr, xi):
    return ar * xr - ai * xi, ar * xi + ai * xr


def _s5_kernel(u_ref, h0r_ref, h0i_ref, pwr_ref, pwi_ref, bblk_ref, cre_ref, cim_ref, d_ref, wglu_ref, bglu_ref,
               on_ref, o_ref, hr_out_ref, hi_out_ref, bu_ref, hre_ref, him_ref, cr_ref, ci_ref, *, tc, nct):
    c = pl.program_id(1)
    S = S5_STATE
    LW = 512

    @pl.when(c == 0)
    def _():
        cr_ref[...] = h0r_ref[0]
        ci_ref[...] = h0i_ref[0]

    u = u_ref[...]
    NK = GW // 128
    SW = S // NK
    for kc in range(NK):
        ukc = _bf(u[:, kc * 128:(kc + 1) * 128])
        for half in range(2):
            cs = slice(half * S + kc * SW, half * S + (kc + 1) * SW)
            bu_ref[:, cs] = jnp.dot(ukc, bblk_ref[kc * 128:(kc + 1) * 128, cs], preferred_element_type=F32)

    rows = _iota2(8, LW, 0)
    for lc in range(S // LW):
        sl = slice(lc * LW, (lc + 1) * LW)
        pr = pwr_ref[:, sl]
        pi = pwi_ref[:, sl]
        a1r = jnp.where(rows >= 1, pr[0:1, :], 0.0)
        a1i = jnp.where(rows >= 1, pi[0:1, :], 0.0)
        a2r = jnp.where(rows >= 2, pr[1:2, :], 0.0)
        a2i = jnp.where(rows >= 2, pi[1:2, :], 0.0)
        a4r = jnp.where(rows >= 4, pr[3:4, :], 0.0)
        a4i = jnp.where(rows >= 4, pi[3:4, :], 0.0)

        def tile(i, carry):
            kr, ki = carry
            r0 = pl.multiple_of(i * 8, 8)
            xr = bu_ref[pl.ds(r0, 8), lc * LW:(lc + 1) * LW]
            xi = bu_ref[pl.ds(r0, 8), S + lc * LW:S + (lc + 1) * LW]
            for (ar, ai, sh) in ((a1r, a1i, 1), (a2r, a2i, 2), (a4r, a4i, 4)):
                sr, si = _cmul(ar, ai, pltpu.roll(xr, sh, 0), pltpu.roll(xi, sh, 0))
                xr, xi = xr + sr, xi + si
            sr, si = _cmul(pr, pi, kr, ki)
            xr, xi = xr + sr, xi + si
            hre_ref[pl.ds(r0, 8), sl] = xr
            him_ref[pl.ds(r0, 8), sl] = xi
            return xr[7:8, :], xi[7:8, :]

        kr, ki = lax.fori_loop(0, tc // 8, tile, (cr_ref[:, sl], ci_ref[:, sl]))
        cr_ref[:, sl] = kr
        ci_ref[:, sl] = ki

    ys = []
    for kc in range(NK):
        ss, os_ = slice(kc * SW, (kc + 1) * SW), slice(kc * 128, (kc + 1) * 128)
        ys.append(jnp.dot(_bf(hre_ref[:, ss]), cre_ref[ss, os_], preferred_element_type=F32)
                  - jnp.dot(_bf(him_ref[:, ss]), cim_ref[ss, os_], preferred_element_type=F32))
    y = jnp.concatenate(ys, axis=1) + d_ref[...] * u
    g = jax.nn.gelu(y)
    out = g * jax.nn.sigmoid(_dot(g, wglu_ref[...]) + bglu_ref[...])
    o_ref[...] = _rms_rows(out, on_ref[...])

    @pl.when(c == nct - 1)
    def _():
        hr_out_ref[0] = cr_ref[...]
        hi_out_ref[0] = ci_ref[...]


def _s5_call(z, row0, nseq, t, h0_re, h0_im, pw_re, pw_im, bblk, c_re, c_im, d_skip, w_glu, b_glu, out_norm,
             col_blk, tc):
    nct = t // tc
    S = S5_STATE
    blk0 = row0 // tc
    const2 = lambda b, c: (0, 0)
    outs = pl.pallas_call(
        functools.partial(_s5_kernel, tc=tc, nct=nct),
        grid=(nseq, nct),
        in_specs=[pl.BlockSpec((tc, GW), lambda b, c: (blk0 + b * nct + c, col_blk)),
                  pl.BlockSpec((1, 1, S), lambda b, c: (b, 0, 0)),
                  pl.BlockSpec((1, 1, S), lambda b, c: (b, 0, 0)),
                  pl.BlockSpec((8, S), const2), pl.BlockSpec((8, S), const2),
                  pl.BlockSpec((GW, 2 * S), const2),
                  pl.BlockSpec((S, GW), const2), pl.BlockSpec((S, GW), const2),
                  pl.BlockSpec((1, GW), const2),
                  pl.BlockSpec((GW, GW), const2), pl.BlockSpec((1, GW), const2), pl.BlockSpec((1, GW), const2)],
        out_specs=[pl.BlockSpec((tc, GW), lambda b, c: (b * nct + c, 0)),
                   pl.BlockSpec((1, 1, S), lambda b, c: (b, 0, 0)),
                   pl.BlockSpec((1, 1, S), lambda b, c: (b, 0, 0))],
        out_shape=[jax.ShapeDtypeStruct((nseq * t, GW), F32),
                   jax.ShapeDtypeStruct((nseq, 1, S), F32), jax.ShapeDtypeStruct((nseq, 1, S), F32)],
        scratch_shapes=[pltpu.VMEM((tc, 2 * S), F32), pltpu.VMEM((tc, S), F32), pltpu.VMEM((tc, S), F32),
                        pltpu.VMEM((1, S), F32), pltpu.VMEM((1, S), F32)],
        compiler_params=_cparams("parallel", "arbitrary"),
        name="s5",
    )(z, h0_re.reshape(nseq, 1, S), h0_im.reshape(nseq, 1, S), pw_re, pw_im, bblk, c_re, c_im,
      d_skip, w_glu, b_glu, out_norm)
    return outs[0], outs[1].reshape(nseq, S), outs[2].reshape(nseq, S)


def _sbprep_kernel(q_ref, k_ref, v_ref, gmat_ref, qg_ref, kg_ref, qb_ref, kn_ref, kb_ref, vb_ref):
    gmat = gmat_ref[...]

    def head_rms(x, gain):
        ms = _dot2_exact_rhs(x * x, gmat) * (1.0 / HD)
        return x * lax.rsqrt(ms + RMS_EPS) * gain

    qb_ref[...] = _bf(head_rms(q_ref[...], qg_ref[...]) * (HD ** -0.5))
    kn = head_rms(k_ref[...], kg_ref[...])
    kn_ref[...] = kn
    kb_ref[...] = _bf(kn)
    vb_ref[...] = _bf(v_ref[...])


def _sbprep_call(z, gmat, q_gain, k_gain, tm=512):
    n = z.shape[0]
    row = pl.BlockSpec((tm, GW), lambda i: (i, 0))
    return pl.pallas_call(
        _sbprep_kernel,
        grid=(n // tm,),
        in_specs=[pl.BlockSpec((tm, GW), lambda i: (i, 3)), pl.BlockSpec((tm, GW), lambda i: (i, 4)),
                  pl.BlockSpec((tm, GW), lambda i: (i, 5)),
                  pl.BlockSpec((GW, GW), lambda i: (0, 0)),
                  pl.BlockSpec((1, GW), lambda i: (0, 0)), pl.BlockSpec((1, GW), lambda i: (0, 0))],
        out_specs=[row, row, row, row],
        out_shape=[jax.ShapeDtypeStruct((n, GW), BF16), jax.ShapeDtypeStruct((n, GW), F32),
                   jax.ShapeDtypeStruct((n, GW), BF16), jax.ShapeDtypeStruct((n, GW), BF16)],
        compiler_params=_cparams("parallel"),
        name="sbprep",
    )(z, z, z, gmat, q_gain, k_gain)


def _sb_tiles(qhs, k2s, v2s, carries, umat, mask, carry_w):
    d = functools.partial(jnp.dot, preferred_element_type=F32)
    n = len(qhs)
    zs = [lax.dot_general(qhs[p], k2s[p], _NT, preferred_element_type=F32) for p in range(n)]
    lks, afts, tots = [], [], []
    for p in range(n):
        z = zs[p]
        lk = -(jnp.maximum(z, 0.0) + jnp.log(1.0 + jnp.exp(-jnp.abs(z))))
        if mask is not None:
            lk = jnp.where(mask, lk, 0.0)
        lks.append(lk)
        aft = d(_bf(lk), umat)
        afts.append(aft)
        tots.append(jnp.broadcast_to(aft[:, 0:1] + lk[:, 0:1], (lk.shape[0], carry_w)))
    pvs = []
    for p in range(n):
        e = zs[p] + lks[p] + afts[p]
        if carries[p] is not None:
            e = e + carries[p]
        w = jnp.exp(e)
        if mask is not None:
            w = jnp.where(mask, w, 0.0)
        pvs.append(d(_bf(w), v2s[p]))
    return pvs, tots


def _sb_kernel(q_ref, kd_ref, vd_ref, kp_ref, vp_ref, on_ref, o_ref, qh_ref, acc_ref, carry_ref,
               *, tq, tk, npast, past_from_grid):
    PW = 2 * HD
    npairs = GW // PW
    m0 = _iota2(1, PW, 1) < HD
    row = _iota2(2 * tq, tq, 0)
    causal = _iota2(2 * tq, tq, 1) < jnp.where(row >= tq, row - tq, row)

    ud = jnp.where(_iota2(tq, tq, 0) > _iota2(tq, tq, 1), 1.0, 0.0).astype(BF16)
    up = jnp.where(_iota2(tk, tk, 0) > _iota2(tk, tk, 1), 1.0, 0.0).astype(BF16)
    pairs = [slice(p * PW, (p + 1) * PW) for p in range(npairs)]

    qhs = [_bf(_hat(q_ref[:, sl], m0)) for sl in pairs]
    pvs, tots = _sb_tiles(qhs, [_bf(kd_ref[:, sl]) for sl in pairs], [_bf(vd_ref[:, sl]) for sl in pairs],
                          [None] * npairs, ud, causal, tk)
    for p in range(npairs):
        qh_ref[p] = qhs[p]
        acc_ref[p] = pvs[p]
        carry_ref[p] = tots[p]

    nblk = pl.program_id(1) * (tq // tk) if past_from_grid else npast

    def body(i, _):
        j = nblk - 1 - i
        r0 = pl.multiple_of(j * tk, tk)
        pvs, tots = _sb_tiles([qh_ref[p] for p in range(npairs)],
                              [_bf(kp_ref[pl.ds(r0, tk), sl]) for sl in pairs],
                              [_bf(vp_ref[pl.ds(r0, tk), sl]) for sl in pairs],
                              [carry_ref[p] for p in range(npairs)], up, None, tk)
        for p in range(npairs):
            acc_ref[p] += pvs[p]
            carry_ref[p] += tots[p]
        return 0

    lax.fori_loop(0, nblk, body, 0)

    outs = []
    ss = jnp.zeros((tq, 1), F32)
    for p in range(npairs):
        a = acc_ref[p]
        o = jnp.where(m0, a[:tq], a[tq:])
        ss = ss + jnp.sum(o * o, axis=-1, keepdims=True)
        outs.append(o)
    scale = lax.rsqrt(ss * (1.0 / GW) + RMS_EPS)
    for p in range(npairs):
        sl = slice(p * PW, (p + 1) * PW)
        o_ref[:, sl] = outs[p] * scale * on_ref[:, sl]


def _sb_call(qn, kn, z, kpast, vpast, row0, nseq, t, tq, tk, past_len, v_col_blk, out_norm, layer=0):
    nq = t // tq
    blk0 = row0 // tq
    npairs = GW // (2 * HD)
    if kpast is None:
        kp_arr, vp_arr = kn, z
        sblk = row0 // t
        kp_spec = pl.BlockSpec((t, GW), lambda b, i: (sblk + b, 0))
        vp_spec = pl.BlockSpec((t, GW), lambda b, i: (sblk + b, v_col_blk))
        npast, from_grid = 0, True
    else:
        kp_arr, vp_arr = kpast, vpast
        kp_spec = pl.BlockSpec((None, None, past_len, GW), lambda b, i: (layer, b, 0, 0))
        vp_spec = kp_spec
        npast, from_grid = past_len // tk, False
    qrow = lambda b, i: (blk0 + b * nq + i, 0)
    return pl.pallas_call(
        functools.partial(_sb_kernel, tq=tq, tk=tk, npast=npast, past_from_grid=from_grid),
        grid=(nseq, nq),
        in_specs=[pl.BlockSpec((tq, GW), qrow), pl.BlockSpec((tq, GW), qrow),
                  pl.BlockSpec((tq, GW), lambda b, i: (blk0 + b * nq + i, v_col_blk)),
                  kp_spec, vp_spec, pl.BlockSpec((1, GW), lambda b, i: (0, 0))],
        out_specs=pl.BlockSpec((tq, GW), lambda b, i: (b * nq + i, 0)),
        out_shape=jax.ShapeDtypeStruct((nseq * t, GW), F32),
        scratch_shapes=[pltpu.VMEM((npairs, 2 * tq, 2 * HD), BF16), pltpu.VMEM((npairs, 2 * tq, 2 * HD), F32),
                        pltpu.VMEM((npairs, 2 * tq, tk), F32)],
        compiler_params=_cparams("parallel", "arbitrary"),
        name="sb_attn",
    )(qn, kn, z, kp_arr, vp_arr, out_norm)


def _rmsgain_kernel(x_ref, g_ref, o_ref):
    o_ref[...] = _rms_rows(x_ref[...], g_ref[...])


def _rmsgain_call(x, gain, tm=512):
    n, d = x.shape
    return pl.pallas_call(
        _rmsgain_kernel,
        grid=(n // tm,),
        in_specs=[pl.BlockSpec((tm, d), lambda i: (i, 0)), pl.BlockSpec((1, d), lambda i: (0, 0))],
        out_specs=pl.BlockSpec((tm, d), lambda i: (i, 0)),
        out_shape=jax.ShapeDtypeStruct((n, d), F32),
        compiler_params=_cparams("parallel"),
        name="rmsgain",
    )(x, gain)


def _rwkv_prep_kernel(zr_ref, zk_ref, zv_ref, zwa_ref, zg_ref, first_ref, mu_ref, w0_ref, w2_ref, a0_ref, a2_ref,
                      g2_ref, kk_ref, ka_ref, rk_ref, gmat_ref,
                      r_out, lw_out, k_out, v_out, kk_out, b_out, g_out, bonus_out, prev_ref, *, tm, wr):
    c = pl.program_id(1)
    cols = prev_ref.shape[1]

    @pl.when(c == 0)
    def _():
        prev_ref[...] = first_ref[0]

    first_row = _iota2(tm, 1, 0) == 0

    def shifted(ref, lo, width):
        x = ref[...]
        prev = jnp.where(first_row, prev_ref[:, lo:lo + width], pltpu.roll(x, 1, 0))
        return x + (prev - x) * mu_ref[:, lo:lo + width]

    r = shifted(zr_ref, 0, GW)
    k = shifted(zk_ref, GW, GW)
    v = shifted(zv_ref, 2 * GW, GW)
    wa = shifted(zwa_ref, 3 * GW, 2 * wr)
    gl = shifted(zg_ref, 3 * GW + 2 * wr, cols - 3 * GW - 2 * wr)
    for ref, lo in ((zr_ref, 0), (zk_ref, GW), (zv_ref, 2 * GW), (zwa_ref, 3 * GW), (zg_ref, 3 * GW + 2 * wr)):
        prev_ref[:, lo:lo + ref.shape[1]] = ref[tm - 1:tm, :]

    xw = w0_ref[...] + _dot(jnp.tanh(wa), w2_ref[...])
    w_log = -(jnp.maximum(-xw, 0.0) + jnp.log(1.0 + jnp.exp(-jnp.abs(xw)))) - 0.5
    a = jax.nn.sigmoid(a0_ref[...] + _dot(wa, a2_ref[...]))
    g_out[...] = _dot(jax.nn.sigmoid(gl), g2_ref[...])
    gmat = gmat_ref[...]
    kk = k * kk_ref[...]
    kk = kk / jnp.maximum(jnp.sqrt(_dot2_exact_rhs(kk * kk, gmat)), 1e-12)
    k = k * (1.0 + (a - 1.0) * ka_ref[...])
    r_out[...] = r
    lw_out[...] = -jnp.exp(w_log)
    k_out[...] = k
    v_out[...] = v
    kk_out[...] = kk
    b_out[...] = kk * a
    bonus_out[...] = _dot2_exact_rhs(r * k * rk_ref[...], gmat) * v


def _rwkv_prep_call(z, row0, nseq, t, tm, first, mu, w0, w2p, a0, a2p, g2, k_k, k_a, r_k, gmat, wr):
    nct = t // tm
    blk0 = row0 // tm
    cols = mu.shape[-1]
    zrow = lambda blkw, off: pl.BlockSpec((tm, blkw), lambda b, c: (blk0 + b * nct + c, off))
    c0 = (z.shape[1] - cols)
    const = lambda shp: pl.BlockSpec(shp, lambda b, c: (0,) * len(shp))
    orow = pl.BlockSpec((tm, GW), lambda b, c: (b * nct + c, 0))
    return pl.pallas_call(
        functools.partial(_rwkv_prep_kernel, tm=tm, wr=wr),
        grid=(nseq, nct),
        in_specs=[zrow(GW, c0 // GW), zrow(GW, c0 // GW + 1), zrow(GW, c0 // GW + 2),
                  zrow(2 * wr, (c0 + 3 * GW) // (2 * wr)), zrow(cols - 3 * GW - 2 * wr, (c0 + 3 * GW) // (2 * wr) + 1),
                  pl.BlockSpec((1, 1, cols), lambda b, c: (b, 0, 0)),
                  const((1, cols)), const((1, GW)), const((2 * wr, GW)), const((1, GW)), const((2 * wr, GW)),
                  const((cols - 3 * GW - 2 * wr, GW)), const((1, GW)), const((1, GW)), const((1, GW)),
                  const((GW, GW))],
        out_specs=[orow] * 8,
        out_shape=[jax.ShapeDtypeStruct((nseq * t, GW), F32)] * 8,
        scratch_shapes=[pltpu.VMEM((1, cols), F32)],
        compiler_params=_cparams("parallel", "arbitrary"),
        name="rwkv_prep",
    )(z, z, z, z, z, first.reshape(nseq, 1, cols), mu, w0, w2p, a0, a2p, g2, k_k, k_a, r_k, gmat)


def _hat(x, m0):
    return jnp.concatenate([jnp.where(m0, x, 0.0), jnp.where(m0, 0.0, x)], axis=0)


def _rwkv_chunk_kernel(r_ref, lw_ref, k_ref, v_ref, kk_ref, b_ref, rt_out, p3_out, m_out, n_out, *, group):
    C = RWKV_CHUNK
    C2 = 2 * C
    lane = _iota2(1, 2 * HD, 1)
    m0 = lane < HD
    ri = _iota2(C2, C2, 0)
    ci = _iota2(C2, C2, 1)
    same = (ri < C) == (ci < C)
    strict = jnp.logical_and(same, ri > ci)
    incl = jnp.logical_and(same, ri >= ci)
    ltri = jnp.where(_iota2(C, C, 0) >= _iota2(C, C, 1), 1.0, 0.0).astype(BF16)
    eye = _iota2(2 * HD, 2 * HD, 0) == _iota2(2 * HD, 2 * HD, 1)
    eye2 = ri == ci

    def pair_stages(p):
        sl = slice(p * 2 * HD, (p + 1) * 2 * HD)
        lw = lw_ref[:, sl]
        cl = _dot2_exact_lhs(ltri, lw)
        yield
        clast = cl[C - 1:C, :]
        kkt = _hat(kk_ref[:, sl] * jnp.exp(cl - lw), m0)
        rt = _hat(r_ref[:, sl] * jnp.exp(cl), m0)
        einv = jnp.exp(-cl)
        kb = _split(_hat(k_ref[:, sl] * einv, m0))
        bb = _split(_hat(b_ref[:, sl] * einv, m0))
        efin = jnp.exp(clast - cl)
        kh = _split(_hat(k_ref[:, sl] * efin, m0))
        bh = _split(_hat(b_ref[:, sl] * efin, m0))
        vh = _split(_hat(v_ref[:, sl], m0))
        lhs = _split(jnp.concatenate([kkt, rt], axis=0))
        gk = _dot3s(lhs, kb, _NT)
        gb = _dot3s(lhs, bb, _NT)
        yield
        a_kk = jnp.where(strict, gk[:C2], 0.0)
        a_rk = _split(jnp.where(incl, gk[C2:], 0.0))
        a_kb = jnp.where(strict, gb[:C2], 0.0)
        a_rb = _split(jnp.where(incl, gb[C2:], 0.0))
        p1 = _dot3s(_split(a_kk), vh)
        tinv = jnp.where(eye2, 1.0, 0.0) - a_kb
        lp = _split(a_kb)
        n = 2
        while n < C:
            yield
            lpf = _dot3s(lp, lp)
            lp = _split(lpf)
            tinv = tinv + _dot3s(_split(tinv), lp)
            n *= 2
        yield
        x = _dot3s(_split(tinv), _split(jnp.concatenate([kkt, p1], axis=1)))
        yield
        xs = _split(x)
        kt = (xs[0][:, :2 * HD], xs[1][:, :2 * HD])
        p2 = (xs[0][:, 2 * HD:], xs[1][:, 2 * HD:])
        y2 = _dot3s(a_rb, xs)
        p3 = _dot3s(a_rk, vh)
        mm = _dot3s(kt, bh, _TN)
        nn = _dot3s(vh, kh, _TN) - _dot3s(p2, bh, _TN)
        yield
        rt_out[p] = rt - y2[:, :2 * HD]
        p3_out[p] = p3 - y2[:, 2 * HD:]
        m_out[p] = jnp.where(eye, jnp.exp(clast), 0.0) - mm
        n_out[p] = nn

    npairs = GW // (2 * HD)
    for p0 in range(0, npairs, group):
        gens = [pair_stages(p) for p in range(p0, p0 + group)]
        while gens:
            for g in list(gens):
                try:
                    next(g)
                except StopIteration:
                    gens.remove(g)


def _rwkv_chunk_call(r, lw, k, v, kk, b):
    n = r.shape[0]
    C = RWKV_CHUNK
    nc = n // C
    npairs = GW // (2 * HD)
    row = pl.BlockSpec((C, GW), lambda i: (i, 0))
    blk = lambda rows: pl.BlockSpec((None, npairs, rows, 2 * HD), lambda i: (i, 0, 0, 0))
    shp = lambda rows: jax.ShapeDtypeStruct((nc, npairs, rows, 2 * HD), F32)
    return pl.pallas_call(
        functools.partial(_rwkv_chunk_kernel, group=4),
        grid=(nc,),
        in_specs=[row] * 6,
        out_specs=[blk(2 * C), blk(2 * C), blk(2 * HD), blk(2 * HD)],
        out_shape=[shp(2 * C), shp(2 * C), shp(2 * HD), shp(2 * HD)],
        compiler_params=_cparams("parallel"),
        name="rwkv_chunk",
    )(r, lw, k, v, kk, b)


def _rwkv_seq_kernel(rt_ref, p3_ref, m_ref, n_ref, s0_ref, g_ref, bonus_ref, lnw_ref, lnb_ref, gmat_ref,
                     o_ref, s_out_ref, s_ref, *, nct):
    c = pl.program_id(1)
    C = RWKV_CHUNK

    @pl.when(c == 0)
    def _():
        s_ref[...] = s0_ref[0]

    gm = gmat_ref[0:2 * HD, 0:2 * HD]
    npairs = GW // (2 * HD)
    pairs = [slice(p * 2 * HD, (p + 1) * 2 * HD) for p in range(npairs)]
    ss = [_split(s_ref[p]) for p in range(npairs)]
    yhs = [_dot3s(_split(rt_ref[p]), ss[p], _NT) + p3_ref[p] for p in range(npairs)]
    snew = [_dot3s(ss[p], _split(m_ref[p])) + n_ref[p] for p in range(npairs)]
    ys = [yh[:C] + yh[C:] for yh in yhs]
    means = [_dot2_exact_rhs(y, gm) * (1.0 / HD) for y in ys]
    ds = [y - m for y, m in zip(ys, means)]
    vrs = [_dot2_exact_rhs(d * d, gm) * (1.0 / HD) for d in ds]
    for p, sl in enumerate(pairs):
        yn = ds[p] * lax.rsqrt(vrs[p] + GN_EPS) * lnw_ref[:, sl] + lnb_ref[:, sl]
        o_ref[:, sl] = (yn + bonus_ref[:, sl]) * g_ref[:, sl]
        s_ref[p] = snew[p]

    @pl.when(c == nct - 1)
    def _():
        s_out_ref[0] = s_ref[...]


def _rwkv_seq_call(rt, p3, m, n, chunk0, nseq, nct, s0, g, bonus, row0, lnx_w, lnx_b, gmat):
    C = RWKV_CHUNK
    npairs = GW // (2 * HD)
    blk0 = row0 // C
    cblk = lambda rows: pl.BlockSpec((None, npairs, rows, 2 * HD), lambda b, c: (chunk0 + b * nct + c, 0, 0, 0))
    sblk = pl.BlockSpec((1, npairs, 2 * HD, 2 * HD), lambda b, c: (b, 0, 0, 0))
    rowin = pl.BlockSpec((C, GW), lambda b, c: (blk0 + b * nct + c, 0))
    const = lambda shp: pl.BlockSpec(shp, lambda b, c: (0,) * len(shp))
    return pl.pallas_call(
        functools.partial(_rwkv_seq_kernel, nct=nct),
        grid=(nseq, nct),
        in_specs=[cblk(2 * C), cblk(2 * C), cblk(2 * HD), cblk(2 * HD), sblk, rowin, rowin,
                  const((1, GW)), const((1, GW)), const((GW, GW))],
        out_specs=[pl.BlockSpec((C, GW), lambda b, c: (b * nct + c, 0)), sblk],
        out_shape=[jax.ShapeDtypeStruct((nseq * nct * C, GW), F32),
                   jax.ShapeDtypeStruct((nseq, npairs, 2 * HD, 2 * HD), F32)],
        scratch_shapes=[pltpu.VMEM((npairs, 2 * HD, 2 * HD), F32)],
        compiler_params=_cparams("parallel", "arbitrary"),
        name="rwkv_seq",
    )(rt, p3, m, n, s0, g, bonus, lnx_w, lnx_b, gmat)


def _pair_states(s):
    b, h = s.shape[:2]
    s = s.reshape(b, h // 2, 2, HD, HD)
    z = jnp.zeros_like(s[:, :, 0])
    top = jnp.concatenate([s[:, :, 0], z], axis=-1)
    bot = jnp.concatenate([z, s[:, :, 1]], axis=-1)
    return jnp.concatenate([top, bot], axis=-2)


def _unpair_states(s2):
    b, hp = s2.shape[:2]
    return jnp.stack([s2[:, :, :HD, :HD], s2[:, :, HD:, HD:]], axis=2).reshape(b, 2 * hp, HD, HD)


def _forward(xp, xs, cache_k, cache_v, s5_re0, s5_im0, wkv0, shift0, W):
    Bp, Tp, D = xp.shape
    Bs, Ts, _ = xs.shape
    depth = W['ln_ffn1'].shape[0]
    past_len = cache_k.shape[2]
    Np, Ns = Bp * Tp, Bs * Ts
    n_heads_d = GW // HD
    wr = W['rwkv_w2'].shape[1]
    dcols = W['rwkv_mu'].shape[-1]

    x = jnp.concatenate([xp.reshape(Np, D), xs.reshape(Ns, D)], axis=0)
    cache_kb = cache_k.reshape(depth, Bs, past_len, GW)
    cache_vb = cache_v.reshape(depth, Bs, past_len, GW)
    wbf = {k: _bf(W[k]) for k in ('w_ffn1_gate', 'w_ffn1_up', 'w_ffn1_down', 'w_in', 'w_out',
                                  'w_ffn2_gate', 'w_ffn2_up', 'w_ffn2_down')}
    hid = lax.broadcasted_iota(jnp.int32, (GW, GW), 0) // HD
    gmat = (hid == jnp.transpose(hid)).astype(BF16)
    grp = lax.broadcasted_iota(jnp.int32, (GW, S5_STATE), 0) // (GW // 32) == \
        lax.broadcasted_iota(jnp.int32, (GW, S5_STATE), 1) // (S5_STATE // 32)
    zeros_first = jnp.zeros((Bp, dcols), F32)
    zeros_s5 = jnp.zeros((Bp, S5_STATE), F32)
    zeros_wkv = jnp.zeros((Bp, n_heads_d // 2, 2 * HD, 2 * HD), F32)

    h = _rms_call(x, W['ln_ffn1'][0][None])
    outs = {k: [] for k in ('kp', 'vp', 'ks', 'vs', 's5rp', 's5ip', 's5rs', 's5is', 'wkvp', 'wkvs', 'shp', 'shs', 'gv')}
    for l in range(depth):
        g1 = lambda name: W[name][l][None]
        x, h = _ffn_call(x, h, wbf['w_ffn1_gate'], wbf['w_ffn1_up'], wbf['w_ffn1_down'], g1('ln_mix'), l)
        z = _inproj_call(h, wbf['w_in'], l)

        oa_p, _ = _gmlp_call(z, 0, Np, min(Tp, GMLP_CHUNK), g1('gmlp_v_norm'), W['gmlp_ws'][l], W['gmlp_b'][l],
                             g1('out_norm_a'))
        oa_s, gv_s = _gmlp_call(z, Np, Ns, min(Ts, GMLP_CHUNK), g1('gmlp_v_norm'), W['gmlp_ws'][l], W['gmlp_b'][l],
                                g1('out_norm_a'))

        pw_re, pw_im, bb_re, bb_im = _s5_params(W['s5_lam_re'][l], W['s5_lam_im'][l], W['s5_log_dt'][l],
                                                W['s5_b_re'][l], W['s5_b_im'][l])
        bblk = jnp.concatenate([jnp.where(grp, jnp.tile(bb_re, (32, 1)), 0.0),
                                jnp.where(grp, jnp.tile(bb_im, (32, 1)), 0.0)], axis=1).astype(BF16)
        grp_t = jnp.transpose(grp)
        cre = jnp.where(grp_t, jnp.tile(jnp.transpose(W['s5_c_re'][l], (0, 2, 1)).reshape(S5_STATE, -1), (1, 32)),
                        0.0).astype(BF16)
        cim = jnp.where(grp_t, jnp.tile(jnp.transpose(W['s5_c_im'][l], (0, 2, 1)).reshape(S5_STATE, -1), (1, 32)),
                        0.0).astype(BF16)
        s5_args = (pw_re, pw_im, bblk, cre, cim, g1('s5_d'), W['s5_w_glu'][l], g1('s5_b_glu'), g1('out_norm_b'))
        ob_p, s5rp, s5ip = _s5_call(z, 0, Bp, Tp, zeros_s5, zeros_s5, *s5_args, col_blk=2, tc=min(Tp, 256))
        ob_s, s5rs, s5is = _s5_call(z, Np, Bs, Ts, s5_re0[l].reshape(Bs, S5_STATE), s5_im0[l].reshape(Bs, S5_STATE),
                                    *s5_args, col_blk=2, tc=min(Ts, 256))

        qg = jnp.tile(W['sb_q_norm'][l], GW // HD)[None]
        kg = jnp.tile(W['sb_k_norm'][l], GW // HD)[None]
        qb, kn, kb, vb = _sbprep_call(z, gmat, qg, kg)
        oc_p = _sb_call(qb, kb, vb, None, None, 0, Bp, Tp, min(Tp, SB_TQ), min(Tp, SB_TK), 0, 0,
                        g1('out_norm_c'))
        oc_s = _sb_call(qb, kb, vb, cache_kb, cache_vb, Np, Bs, Ts, Ts, min(past_len, SB_TK_SAMPLE), past_len, 0,
                        g1('out_norm_c'), layer=l)
        vcol = z[:, 5 * GW:6 * GW]

        w2p = jnp.concatenate([W['rwkv_w2'][l], jnp.zeros_like(W['rwkv_a2'][l])], axis=0)
        a2p = jnp.concatenate([jnp.zeros_like(W['rwkv_w2'][l]), W['rwkv_a2'][l]], axis=0)
        rk = W['rwkv_r_k'][l].reshape(1, GW)
        prep_args = (g1('rwkv_mu'), g1('rwkv_w0'), w2p, g1('rwkv_a0'), a2p, W['rwkv_g2'][l], g1('rwkv_k_k'),
                     g1('rwkv_k_a'), rk, gmat, wr)
        pp = _rwkv_prep_call(z, 0, Bp, Tp, min(Tp, 512), zeros_first, *prep_args)
        ps = _rwkv_prep_call(z, Np, Bs, Ts, Ts, shift0[l], *prep_args)
        seq_args = (g1('rwkv_lnx_w'), g1('rwkv_lnx_b'), gmat)
        od_p, wkv_p = _rwkv_seq_call(*_rwkv_chunk_call(*pp[:6]), 0, Bp, Tp // RWKV_CHUNK, zeros_wkv, pp[6], pp[7], 0,
                                     *seq_args)
        od_s, wkv_s = _rwkv_seq_call(*_rwkv_chunk_call(*ps[:6]), 0, Bs, Ts // RWKV_CHUNK, _pair_states(wkv0[l]),
                                     ps[6], ps[7], 0, *seq_args)

        gain_next = g1('ln_ffn2')
        x, h = _outproj_call(x, (oa_p, ob_p, oc_p, od_p), (oa_s, ob_s, oc_s, od_s), wbf['w_out'], gain_next, l)
        gain_next = W['ln_ffn1'][l + 1][None] if l + 1 < depth else g1('ln_ffn2')
        x, h = _ffn_call(x, h, wbf['w_ffn2_gate'], wbf['w_ffn2_up'], wbf['w_ffn2_down'], gain_next, l)

        nh = GW // HD
        outs['kp'].append(kn[:Np].reshape(Bp, Tp, nh, HD))
        outs['vp'].append(vcol[:Np].reshape(Bp, Tp, nh, HD))
        outs['ks'].append(kn[Np:].reshape(Bs, Ts, nh, HD))
        outs['vs'].append(vcol[Np:].reshape(Bs, Ts, nh, HD))
        outs['s5rp'].append(s5rp.reshape(Bp, 32, -1))
        outs['s5ip'].append(s5ip.reshape(Bp, 32, -1))
        outs['s5rs'].append(s5rs.reshape(Bs, 32, -1))
        outs['s5is'].append(s5is.reshape(Bs, 32, -1))
        outs['wkvp'].append(_unpair_states(wkv_p))
        outs['wkvs'].append(_unpair_states(wkv_s))
        zd = z[:, z.shape[1] - dcols:]
        outs['shp'].append(zd[:Np].reshape(Bp, Tp, dcols)[:, -1])
        outs['shs'].append(zd[Np:].reshape(Bs, Ts, dcols)[:, -1])
        outs['gv'].append(gv_s.reshape(Bs, Ts, GW))

    st = lambda k: jnp.stack(outs[k], axis=0)
    return (x[:Np].reshape(Bp, Tp, D), x[Np:].reshape(Bs, Ts, D),
            st('kp'), st('vp'), st('ks'), st('vs'),
            st('s5rp'), st('s5ip'), st('s5rs'), st('s5is'),
            st('wkvp'), st('wkvs'), st('shp'), st('shs'), st('gv'))


_WEIGHT_NAMES = ('ln_ffn1', 'w_ffn1_gate', 'w_ffn1_up', 'w_ffn1_down', 'ln_mix', 'w_in',
                 'gmlp_v_norm', 'gmlp_ws', 'gmlp_b', 'out_norm_a',
                 's5_lam_re', 's5_lam_im', 's5_log_dt', 's5_b_re', 's5_b_im', 's5_c_re', 's5_c_im', 's5_d',
                 's5_w_glu', 's5_b_glu', 'out_norm_b',
                 'sb_q_norm', 'sb_k_norm', 'out_norm_c',
                 'rwkv_mu', 'rwkv_w0', 'rwkv_w2', 'rwkv_a0', 'rwkv_a2', 'rwkv_g2', 'rwkv_k_k', 'rwkv_k_a', 'rwkv_r_k',
                 'rwkv_lnx_w', 'rwkv_lnx_b',
                 'w_out', 'ln_ffn2', 'w_ffn2_gate', 'w_ffn2_up', 'w_ffn2_down')


def kernel(x_prompt, x_sample, cache_sb_k, cache_sb_v, state_s5_re, state_s5_im, state_rwkv_wkv, state_rwkv_shift,
           ln_ffn1, w_ffn1_gate, w_ffn1_up, w_ffn1_down, ln_mix, w_in,
           gmlp_v_norm, gmlp_ws, gmlp_b, out_norm_a,
           s5_lam_re, s5_lam_im, s5_log_dt, s5_b_re, s5_b_im, s5_c_re, s5_c_im, s5_d, s5_w_glu, s5_b_glu, out_norm_b,
           sb_q_norm, sb_k_norm, out_norm_c,
           rwkv_mu, rwkv_w0, rwkv_w2, rwkv_a0, rwkv_a2, rwkv_g2, rwkv_k_k, rwkv_k_a, rwkv_r_k, rwkv_lnx_w, rwkv_lnx_b,
           w_out, ln_ffn2, w_ffn2_gate, w_ffn2_up, w_ffn2_down):
    weights = (ln_ffn1, w_ffn1_gate, w_ffn1_up, w_ffn1_down, ln_mix, w_in,
               gmlp_v_norm, gmlp_ws, gmlp_b, out_norm_a,
               s5_lam_re, s5_lam_im, s5_log_dt, s5_b_re, s5_b_im, s5_c_re, s5_c_im, s5_d, s5_w_glu, s5_b_glu,
               out_norm_b, sb_q_norm, sb_k_norm, out_norm_c,
               rwkv_mu, rwkv_w0, rwkv_w2, rwkv_a0, rwkv_a2, rwkv_g2, rwkv_k_k, rwkv_k_a, rwkv_r_k, rwkv_lnx_w,
               rwkv_lnx_b, w_out, ln_ffn2, w_ffn2_gate, w_ffn2_up, w_ffn2_down)
    W = dict(zip(_WEIGHT_NAMES, weights))
    return _forward(x_prompt, x_sample, cache_sb_k, cache_sb_v, state_s5_re, state_s5_im, state_rwkv_wkv,
                    state_rwkv_shift, W)
```

```python
import functools

import jax
import jax.numpy as jnp
from jax import lax
from jax.experimental import pallas as pl
from jax.experimental.pallas import tpu as pltpu

F32 = jnp.float32
BF16 = jnp.bfloat16

RMS_EPS = 1e-6
GN_EPS = 64e-5
A_HEADS = 4
GMLP_CHUNK = 128
HD = 64
GW = 512
S5_STATE = 2048
RWKV_CHUNK = 64
SB_TQ, SB_TK = 256, 128
SB_TK_SAMPLE = 256
VMEM_LIMIT = 56 * 1024 * 1024


def _cparams(*sem):
    return pltpu.CompilerParams(dimension_semantics=sem, vmem_limit_bytes=VMEM_LIMIT)


def _bf(x):
    return x.astype(BF16)


_NN = (((1,), (0,)), ((), ()))
_NT = (((1,), (1,)), ((), ()))
_TN = (((0,), (0,)), ((), ()))


def _dot(a, b, dims=_NN):
    return lax.dot_general(_bf(a), _bf(b), dims, preferred_element_type=F32)


def _split(x):
    hi = _bf(x)
    lo = _bf(x - hi.astype(F32))
    return hi, lo


def _dot3(a, b, dims=_NN):
    ah, al = _split(a)
    bh, bl = _split(b)
    d = functools.partial(lax.dot_general, dimension_numbers=dims, preferred_element_type=F32)
    return d(ah, bh) + (d(ah, bl) + d(al, bh))


def _dot3s(a, b, dims=_NN):
    d = functools.partial(lax.dot_general, dimension_numbers=dims, preferred_element_type=F32)
    return d(a[0], b[0]) + (d(a[0], b[1]) + d(a[1], b[0]))


def _dot2_exact_rhs(a, b_bf, dims=_NN):
    ah, al = _split(a)
    d = functools.partial(lax.dot_general, dimension_numbers=dims, preferred_element_type=F32)
    return d(ah, b_bf) + d(al, b_bf)


def _dot2_exact_lhs(a_bf, b, dims=_NN):
    bh, bl = _split(b)
    d = functools.partial(lax.dot_general, dimension_numbers=dims, preferred_element_type=F32)
    return d(a_bf, bh) + d(a_bf, bl)


def _rms_rows(x, gain):
    ms = jnp.mean(x * x, axis=-1, keepdims=True)
    return x * lax.rsqrt(ms + RMS_EPS) * gain


def _iota2(n, m, axis):
    return lax.broadcasted_iota(jnp.int32, (n, m), axis)


def _rms_kernel(x_ref, g_ref, h_ref):
    h_ref[...] = _bf(_rms_rows(x_ref[...], g_ref[...]))


def _rms_call(x, gain, tm=512):
    n, d = x.shape
    return pl.pallas_call(
        _rms_kernel,
        grid=(n // tm,),
        in_specs=[pl.BlockSpec((tm, d), lambda i: (i, 0)), pl.BlockSpec((1, d), lambda i: (0, 0))],
        out_specs=pl.BlockSpec((tm, d), lambda i: (i, 0)),
        out_shape=jax.ShapeDtypeStruct((n, d), BF16),
        compiler_params=_cparams("parallel"),
        name="rms",
    )(x, gain)


def _ffn_kernel(x_ref, h_ref, wg_ref, wu_ref, wd_ref, gn_ref, o_ref, hn_ref, *, nj):
    j = pl.program_id(1)

    @pl.when(j == 0)
    def _():
        o_ref[...] = jnp.zeros_like(o_ref)

    h = h_ref[...]
    g = jnp.dot(h, wg_ref[...], preferred_element_type=F32)
    u = jnp.dot(h, wu_ref[...], preferred_element_type=F32)
    a = _bf(g * jax.nn.sigmoid(g) * u)
    o_ref[...] += jnp.dot(a, wd_ref[...], preferred_element_type=F32)

    @pl.when(j == nj - 1)
    def _():
        y = x_ref[...] + 0.5 * o_ref[...]
        o_ref[...] = y
        hn_ref[...] = _bf(_rms_rows(y, gn_ref[...]))


def _ffn_call(x, h, wg, wu, wd, gain_next, layer, tm=512, tf=512):
    n, d = x.shape
    ff = wg.shape[-1]
    nj = ff // tf
    return pl.pallas_call(
        functools.partial(_ffn_kernel, nj=nj),
        grid=(n // tm, nj),
        in_specs=[
            pl.BlockSpec((tm, d), lambda i, j: (i, 0)),
            pl.BlockSpec((tm, d), lambda i, j: (i, 0)),
            pl.BlockSpec((None, d, tf), lambda i, j: (layer, 0, j)),
            pl.BlockSpec((None, d, tf), lambda i, j: (layer, 0, j)),
            pl.BlockSpec((None, tf, d), lambda i, j: (layer, j, 0)),
            pl.BlockSpec((1, d), lambda i, j: (0, 0)),
        ],
        out_specs=[pl.BlockSpec((tm, d), lambda i, j: (i, 0)), pl.BlockSpec((tm, d), lambda i, j: (i, 0))],
        out_shape=[jax.ShapeDtypeStruct((n, d), F32), jax.ShapeDtypeStruct((n, d), BF16)],
        compiler_params=_cparams("parallel", "arbitrary"),
        name="ffn",
    )(x, h, wg, wu, wd, gain_next)


def _ffn_gu_kernel(h_ref, wg_ref, wu_ref, a_ref):
    h = h_ref[...]
    g = jnp.dot(h, wg_ref[...], preferred_element_type=F32)
    u = jnp.dot(h, wu_ref[...], preferred_element_type=F32)
    a_ref[...] = _bf(g * jax.nn.sigmoid(g) * u)


def _ffn_down_kernel(x_ref, a_ref, wd_ref, gn_ref, o_ref, hn_ref, *, nk):
    k = pl.program_id(1)
    part = jnp.dot(a_ref[...], wd_ref[...], preferred_element_type=F32)

    @pl.when(k == 0)
    def _():
        o_ref[...] = part

    if nk > 2:
        @pl.when(jnp.logical_and(k > 0, k < nk - 1))
        def _():
            o_ref[...] += part

    @pl.when(k == nk - 1)
    def _():
        y = x_ref[...] + 0.5 * (o_ref[...] + part)
        o_ref[...] = y
        hn_ref[...] = _bf(_rms_rows(y, gn_ref[...]))


def _ffn2_call(x, h, wg, wu, wd, gain_next, layer, tm=512, tf=512, nk=2):
    n, d = x.shape
    ff = wg.shape[-1]
    a = pl.pallas_call(
        _ffn_gu_kernel,
        grid=(ff // tf, n // tm),
        in_specs=[pl.BlockSpec((tm, d), lambda j, i: (i, 0)),
                  pl.BlockSpec((None, d, tf), lambda j, i: (layer, 0, j)),
                  pl.BlockSpec((None, d, tf), lambda j, i: (layer, 0, j))],
        out_specs=pl.BlockSpec((tm, tf), lambda j, i: (i, j)),
        out_shape=jax.ShapeDtypeStruct((n, ff), BF16),
        compiler_params=_cparams("parallel", "parallel"),
        name="ffn_gu",
    )(h, wg, wu)
    tk = ff // nk
    return pl.pallas_call(
        functools.partial(_ffn_down_kernel, nk=nk),
        grid=(n // tm, nk),
        in_specs=[pl.BlockSpec((tm, d), lambda i, k: (i, 0)),
                  pl.BlockSpec((tm, tk), lambda i, k: (i, k)),
                  pl.BlockSpec((None, tk, d), lambda i, k: (layer, k, 0)),
                  pl.BlockSpec((1, d), lambda i, k: (0, 0))],
        out_specs=[pl.BlockSpec((tm, d), lambda i, k: (i, 0)), pl.BlockSpec((tm, d), lambda i, k: (i, 0))],
        out_shape=[jax.ShapeDtypeStruct((n, d), F32), jax.ShapeDtypeStruct((n, d), BF16)],
        compiler_params=_cparams("parallel", "arbitrary"),
        name="ffn_down",
    )(x, a, wd, gain_next)


def _inproj_kernel(h_ref, w_ref, z_ref):
    z_ref[...] = jnp.dot(h_ref[...], w_ref[...], preferred_element_type=F32)


def _inproj_call(h, w_in, layer, tm=512):
    n, d = h.shape
    cols = w_in.shape[-1]
    tn = cols // 2
    return pl.pallas_call(
        _inproj_kernel,
        grid=(2, n // tm),
        in_specs=[pl.BlockSpec((tm, d), lambda j, i: (i, 0)),
                  pl.BlockSpec((None, d, tn), lambda j, i: (layer, 0, j))],
        out_specs=pl.BlockSpec((tm, tn), lambda j, i: (i, j)),
        out_shape=jax.ShapeDtypeStruct((n, cols), F32),
        compiler_params=_cparams("parallel", "parallel"),
        name="inproj",
    )(h, w_in)


def _outproj_kernel(x_ref, *refs, n_first):
    first, second = refs[0:4], refs[4:8]
    w_ref, gn_ref, o_ref, hn_ref = refs[8:]

    def run(mix_refs):
        acc = x_ref[...]
        for i, r in enumerate(mix_refs):
            acc = acc + jnp.dot(_bf(r[...]), w_ref[i * GW:(i + 1) * GW, :], preferred_element_type=F32)
        o_ref[...] = acc
        hn_ref[...] = _bf(_rms_rows(acc, gn_ref[...]))

    i = pl.program_id(0)
    pl.when(i < n_first)(lambda: run(first))
    pl.when(i >= n_first)(lambda: run(second))


def _outproj_call(x, mix_first, mix_second, w_out, gain_next, layer, tm=256):
    n, d = x.shape
    n_first = mix_first[0].shape[0] // tm
    n_second = mix_second[0].shape[0] // tm
    row = lambda i: (i, 0)
    first_row = lambda i: (jnp.minimum(i, n_first - 1), 0)
    second_row = lambda i: (jnp.maximum(i - n_first, 0), 0)
    return pl.pallas_call(
        functools.partial(_outproj_kernel, n_first=n_first),
        grid=(n_first + n_second,),
        in_specs=[pl.BlockSpec((tm, d), row)] + [pl.BlockSpec((tm, GW), first_row)] * 4
        + [pl.BlockSpec((tm, GW), second_row)] * 4
        + [pl.BlockSpec((None, d, d), lambda i: (layer, 0, 0)), pl.BlockSpec((1, d), lambda i: (0, 0))],
        out_specs=[pl.BlockSpec((tm, d), row), pl.BlockSpec((tm, d), row)],
        out_shape=[jax.ShapeDtypeStruct((n, d), F32), jax.ShapeDtypeStruct((n, d), BF16)],
        compiler_params=_cparams("parallel"),
        name="outproj",
    )(x, *mix_first, *mix_second, w_out, gain_next)


def _gmlp_kernel(z_ref, vn_ref, ws_ref, bt_ref, on_ref, o_ref, v_ref, *, L):
    z = jax.nn.gelu(z_ref[...])
    causal = _iota2(L, L, 1) <= _iota2(L, L, 0)
    hw = GW // A_HEADS
    ss = jnp.zeros((L, 1), F32)
    outs = []
    for h in range(A_HEADS):
        vh = _rms_rows(z[:, GW + h * hw:GW + (h + 1) * hw], vn_ref[...])
        v_ref[:, h * hw:(h + 1) * hw] = vh
        w = jnp.where(causal, ws_ref[h], 0.0)
        s = _dot(w, vh) + bt_ref[:, h:h + 1]
        oh = z[:, h * hw:(h + 1) * hw] * s
        ss = ss + jnp.sum(oh * oh, axis=-1, keepdims=True)
        outs.append(oh)
    scale = lax.rsqrt(ss * (1.0 / GW) + RMS_EPS)
    for h in range(A_HEADS):
        o_ref[:, h * hw:(h + 1) * hw] = outs[h] * scale * on_ref[:, h * hw:(h + 1) * hw]


def _gmlp_call(z, row0, nrows, L, v_norm, ws, b, out_norm):
    nb = nrows // L
    b0 = row0 // L
    ws_l = ws[:, :L, :L]
    bt = jnp.transpose(b[:, :L])
    row = lambda i: (i, 0)
    return pl.pallas_call(
        functools.partial(_gmlp_kernel, L=L),
        grid=(nb,),
        in_specs=[pl.BlockSpec((L, 2 * GW), lambda i: (b0 + i, 0)),
                  pl.BlockSpec((1, GW // A_HEADS), lambda i: (0, 0)),
                  pl.BlockSpec((A_HEADS, L, L), lambda i: (0, 0, 0)),
                  pl.BlockSpec((L, A_HEADS), lambda i: (0, 0)),
                  pl.BlockSpec((1, GW), lambda i: (0, 0))],
        out_specs=[pl.BlockSpec((L, GW), row), pl.BlockSpec((L, GW), row)],
        out_shape=[jax.ShapeDtypeStruct((nrows, GW), F32), jax.ShapeDtypeStruct((nrows, GW), F32)],
        compiler_params=_cparams("parallel"),
        name="gmlp",
    )(z, v_norm, ws_l, bt, out_norm)


def _s5_param_kernel(lr_ref, li_ref, ldt_ref, brt_ref, bit_ref, pw_re_ref, pw_im_ref, bb_re_ref, bb_im_ref):
    lr = lr_ref[...]
    li = li_ref[...]
    dt = jnp.exp(ldt_ref[...])
    n = (_iota2(8, S5_STATE, 0) + 1).astype(F32)
    mag = jnp.exp(n * (lr * dt))
    ang = n * (li * dt)
    pw_re = mag * jnp.cos(ang)
    pw_im = mag * jnp.sin(ang)
    pw_re_ref[...] = pw_re
    pw_im_ref[...] = pw_im
    ab_re = pw_re[0:1, :]
    ab_im = pw_im[0:1, :]
    den = lr * lr + li * li
    nr, ni = ab_re - 1.0, ab_im
    cf_re = (nr * lr + ni * li) / den
    cf_im = (ni * lr - nr * li) / den
    br = brt_ref[...]
    bi = bit_ref[...]
    bb_re_ref[...] = cf_re * br - cf_im * bi
    bb_im_ref[...] = cf_re * bi + cf_im * br


def _s5_params(lam_re, lam_im, log_dt, b_re, b_im):
    g, p = lam_re.shape
    s = g * p
    ch = b_re.shape[-1]
    ldt = jnp.broadcast_to(log_dt[:, None], (g, p)).reshape(1, s)
    brt = jnp.transpose(b_re.reshape(s, ch))
    bit = jnp.transpose(b_im.reshape(s, ch))
    return pl.pallas_call(
        _s5_param_kernel,
        out_shape=[jax.ShapeDtypeStruct((8, s), F32), jax.ShapeDtypeStruct((8, s), F32),
                   jax.ShapeDtypeStruct((ch, s), F32), jax.ShapeDtypeStruct((ch, s), F32)],
        name="s5_params",
    )(lam_re.reshape(1, s), lam_im.reshape(1, s), ldt, brt, bit)


def _cmul(ar, ai, xr, xi):
    return ar * xr - ai * xi, ar * xi + ai * xr


def _s5_kernel(u_ref, h0r_ref, h0i_ref, pwr_ref, pwi_ref, bblk_ref, cre_ref, cim_ref, d_ref, wglu_ref, bglu_ref,
               on_ref, o_ref, hr_out_ref, hi_out_ref, bu_ref, hre_ref, him_ref, cr_ref, ci_ref, *, tc, nct):
    c = pl.program_id(1)
    S = S5_STATE
    LW = 512

    @pl.when(c == 0)
    def _():
        cr_ref[...] = h0r_ref[0]
        ci_ref[...] = h0i_ref[0]

    u = u_ref[...]
    NK = GW // 128
    SW = S // NK
    for kc in range(NK):
        ukc = _bf(u[:, kc * 128:(kc + 1) * 128])
        for half in range(2):
            cs = slice(half * S + kc * SW, half * S + (kc + 1) * SW)
            bu_ref[:, cs] = jnp.dot(ukc, bblk_ref[kc * 128:(kc + 1) * 128, cs], preferred_element_type=F32)

    rows = _iota2(8, LW, 0)
    for lc in range(S // LW):
        sl = slice(lc * LW, (lc + 1) * LW)
        pr = pwr_ref[:, sl]
        pi = pwi_ref[:, sl]
        a1r = jnp.where(rows >= 1, pr[0:1, :], 0.0)
        a1i = jnp.where(rows >= 1, pi[0:1, :], 0.0)
        a2r = jnp.where(rows >= 2, pr[1:2, :], 0.0)
        a2i = jnp.where(rows >= 2, pi[1:2, :], 0.0)
        a4r = jnp.where(rows >= 4, pr[3:4, :], 0.0)
        a4i = jnp.where(rows >= 4, pi[3:4, :], 0.0)

        def tile(i, carry):
            kr, ki = carry
            r0 = pl.multiple_of(i * 8, 8)
            xr = bu_ref[pl.ds(r0, 8), lc * LW:(lc + 1) * LW]
            xi = bu_ref[pl.ds(r0, 8), S + lc * LW:S + (lc + 1) * LW]
            for (ar, ai, sh) in ((a1r, a1i, 1), (a2r, a2i, 2), (a4r, a4i, 4)):
                sr, si = _cmul(ar, ai, pltpu.roll(xr, sh, 0), pltpu.roll(xi, sh, 0))
                xr, xi = xr + sr, xi + si
            sr, si = _cmul(pr, pi, kr, ki)
            xr, xi = xr + sr, xi + si
            hre_ref[pl.ds(r0, 8), sl] = xr
            him_ref[pl.ds(r0, 8), sl] = xi
            return xr[7:8, :], xi[7:8, :]

        kr, ki = lax.fori_loop(0, tc // 8, tile, (cr_ref[:, sl], ci_ref[:, sl]))
        cr_ref[:, sl] = kr
        ci_ref[:, sl] = ki

    ys = []
    for kc in range(NK):
        ss, os_ = slice(kc * SW, (kc + 1) * SW), slice(kc * 128, (kc + 1) * 128)
        ys.append(jnp.dot(_bf(hre_ref[:, ss]), cre_ref[ss, os_], preferred_element_type=F32)
                  - jnp.dot(_bf(him_ref[:, ss]), cim_ref[ss, os_], preferred_element_type=F32))
    y = jnp.concatenate(ys, axis=1) + d_ref[...] * u
    g = jax.nn.gelu(y)
    out = g * jax.nn.sigmoid(_dot(g, wglu_ref[...]) + bglu_ref[...])
    o_ref[...] = _rms_rows(out, on_ref[...])

    @pl.when(c == nct - 1)
    def _():
        hr_out_ref[0] = cr_ref[...]
        hi_out_ref[0] = ci_ref[...]


def _s5_call(z, row0, nseq, t, h0_re, h0_im, pw_re, pw_im, bblk, c_re, c_im, d_skip, w_glu, b_glu, out_norm,
             col_blk, tc):
    nct = t // tc
    S = S5_STATE
    blk0 = row0 // tc
    const2 = lambda b, c: (0, 0)
    outs = pl.pallas_call(
        functools.partial(_s5_kernel, tc=tc, nct=nct),
        grid=(nseq, nct),
        in_specs=[pl.BlockSpec((tc, GW), lambda b, c: (blk0 + b * nct + c, col_blk)),
                  pl.BlockSpec((1, 1, S), lambda b, c: (b, 0, 0)),
                  pl.BlockSpec((1, 1, S), lambda b, c: (b, 0, 0)),
                  pl.BlockSpec((8, S), const2), pl.BlockSpec((8, S), const2),
                  pl.BlockSpec((GW, 2 * S), const2),
                  pl.BlockSpec((S, GW), const2), pl.BlockSpec((S, GW), const2),
                  pl.BlockSpec((1, GW), const2),
                  pl.BlockSpec((GW, GW), const2), pl.BlockSpec((1, GW), const2), pl.BlockSpec((1, GW), const2)],
        out_specs=[pl.BlockSpec((tc, GW), lambda b, c: (b * nct + c, 0)),
                   pl.BlockSpec((1, 1, S), lambda b, c: (b, 0, 0)),
                   pl.BlockSpec((1, 1, S), lambda b, c: (b, 0, 0))],
        out_shape=[jax.ShapeDtypeStruct((nseq * t, GW), F32),
                   jax.ShapeDtypeStruct((nseq, 1, S), F32), jax.ShapeDtypeStruct((nseq, 1, S), F32)],
        scratch_shapes=[pltpu.VMEM((tc, 2 * S), F32), pltpu.VMEM((tc, S), F32), pltpu.VMEM((tc, S), F32),
                        pltpu.VMEM((1, S), F32), pltpu.VMEM((1, S), F32)],
        compiler_params=_cparams("parallel", "arbitrary"),
        name="s5",
    )(z, h0_re.reshape(nseq, 1, S), h0_im.reshape(nseq, 1, S), pw_re, pw_im, bblk, c_re, c_im,
      d_skip, w_glu, b_glu, out_norm)
    return outs[0], outs[1].reshape(nseq, S), outs[2].reshape(nseq, S)


def _sbprep_kernel(q_ref, k_ref, v_ref, gmat_ref, qg_ref, kg_ref, qb_ref, kn_ref, kb_ref, vb_ref):
    gmat = gmat_ref[...]

    def head_rms(x, gain):
        ms = _dot2_exact_rhs(x * x, gmat) * (1.0 / HD)
        return x * lax.rsqrt(ms + RMS_EPS) * gain

    qb_ref[...] = _bf(head_rms(q_ref[...], qg_ref[...]) * (HD ** -0.5))
    kn = head_rms(k_ref[...], kg_ref[...])
    kn_ref[...] = kn
    kb_ref[...] = _bf(kn)
    vb_ref[...] = _bf(v_ref[...])


def _sbprep_call(z, gmat, q_gain, k_gain, tm=512):
    n = z.shape[0]
    row = pl.BlockSpec((tm, GW), lambda i: (i, 0))
    return pl.pallas_call(
        _sbprep_kernel,
        grid=(n // tm,),
        in_specs=[pl.BlockSpec((tm, GW), lambda i: (i, 3)), pl.BlockSpec((tm, GW), lambda i: (i, 4)),
                  pl.BlockSpec((tm, GW), lambda i: (i, 5)),
                  pl.BlockSpec((GW, GW), lambda i: (0, 0)),
                  pl.BlockSpec((1, GW), lambda i: (0, 0)), pl.BlockSpec((1, GW), lambda i: (0, 0))],
        out_specs=[row, row, row, row],
        out_shape=[jax.ShapeDtypeStruct((n, GW), BF16), jax.ShapeDtypeStruct((n, GW), F32),
                   jax.ShapeDtypeStruct((n, GW), BF16), jax.ShapeDtypeStruct((n, GW), BF16)],
        compiler_params=_cparams("parallel"),
        name="sbprep",
    )(z, z, z, gmat, q_gain, k_gain)


def _sb_tiles(qhs, k2s, v2s, carries, umat, mask, carry_w):
    d = functools.partial(jnp.dot, preferred_element_type=F32)
    n = len(qhs)
    zs = [lax.dot_general(qhs[p], k2s[p], _NT, preferred_element_type=F32) for p in range(n)]
    lks, afts, tots = [], [], []
    for p in range(n):
        z = zs[p]
        lk = -(jnp.maximum(z, 0.0) + jnp.log(1.0 + jnp.exp(-jnp.abs(z))))
        if mask is not None:
            lk = jnp.where(mask, lk, 0.0)
        lks.append(lk)
        aft = d(_bf(lk), umat)
        afts.append(aft)
        tots.append(jnp.broadcast_to(aft[:, 0:1] + lk[:, 0:1], (lk.shape[0], carry_w)))
    pvs = []
    for p in range(n):
        e = zs[p] + lks[p] + afts[p]
        if carries[p] is not None:
            e = e + carries[p]
        w = jnp.exp(e)
        if mask is not None:
            w = jnp.where(mask, w, 0.0)
        pvs.append(d(_bf(w), v2s[p]))
    return pvs, tots


def _sb_kernel(q_ref, kd_ref, vd_ref, kp_ref, vp_ref, on_ref, o_ref, qh_ref, acc_ref, carry_ref,
               *, tq, tk, npast, past_from_grid):
    PW = 2 * HD
    npairs = GW // PW
    m0 = _iota2(1, PW, 1) < HD
    row = _iota2(2 * tq, tq, 0)
    causal = _iota2(2 * tq, tq, 1) < jnp.where(row >= tq, row - tq, row)

    ud = jnp.where(_iota2(tq, tq, 0) > _iota2(tq, tq, 1), 1.0, 0.0).astype(BF16)
    up = jnp.where(_iota2(tk, tk, 0) > _iota2(tk, tk, 1), 1.0, 0.0).astype(BF16)
    pairs = [slice(p * PW, (p + 1) * PW) for p in range(npairs)]

    qhs = [_bf(_hat(q_ref[:, sl], m0)) for sl in pairs]
    pvs, tots = _sb_tiles(qhs, [_bf(kd_ref[:, sl]) for sl in pairs], [_bf(vd_ref[:, sl]) for sl in pairs],
                          [None] * npairs, ud, causal, tk)
    for p in range(npairs):
        qh_ref[p] = qhs[p]
        acc_ref[p] = pvs[p]
        carry_ref[p] = tots[p]

    nblk = pl.program_id(1) * (tq // tk) if past_from_grid else npast

    def body(i, _):
        j = nblk - 1 - i
        r0 = pl.multiple_of(j * tk, tk)
        pvs, tots = _sb_tiles([qh_ref[p] for p in range(npairs)],
                              [_bf(kp_ref[pl.ds(r0, tk), sl]) for sl in pairs],
                              [_bf(vp_ref[pl.ds(r0, tk), sl]) for sl in pairs],
                              [carry_ref[p] for p in range(npairs)], up, None, tk)
        for p in range(npairs):
            acc_ref[p] += pvs[p]
            carry_ref[p] += tots[p]
        return 0

    lax.fori_loop(0, nblk, body, 0)

    outs = []
    ss = jnp.zeros((tq, 1), F32)
    for p in range(npairs):
        a = acc_ref[p]
        o = jnp.where(m0, a[:tq], a[tq:])
        ss = ss + jnp.sum(o * o, axis=-1, keepdims=True)
        outs.append(o)
    scale = lax.rsqrt(ss * (1.0 / GW) + RMS_EPS)
    for p in range(npairs):
        sl = slice(p * PW, (p + 1) * PW)
        o_ref[:, sl] = outs[p] * scale * on_ref[:, sl]


def _sb_call(qn, kn, z, kpast, vpast, row0, nseq, t, tq, tk, past_len, v_col_blk, out_norm, layer=0):
    nq = t // tq
    blk0 = row0 // tq
    npairs = GW // (2 * HD)
    if kpast is None:
        kp_arr, vp_arr = kn, z
        sblk = row0 // t
        kp_spec = pl.BlockSpec((t, GW), lambda b, i: (sblk + b, 0))
        vp_spec = pl.BlockSpec((t, GW), lambda b, i: (sblk + b, v_col_blk))
        npast, from_grid = 0, True
    else:
        kp_arr, vp_arr = kpast, vpast
        kp_spec = pl.BlockSpec((None, None, past_len, GW), lambda b, i: (layer, b, 0, 0))
        vp_spec = kp_spec
        npast, from_grid = past_len // tk, False
    qrow = lambda b, i: (blk0 + b * nq + i, 0)
    return pl.pallas_call(
        functools.partial(_sb_kernel, tq=tq, tk=tk, npast=npast, past_from_grid=from_grid),
        grid=(nseq, nq),
        in_specs=[pl.BlockSpec((tq, GW), qrow), pl.BlockSpec((tq, GW), qrow),
                  pl.BlockSpec((tq, GW), lambda b, i: (blk0 + b * nq + i, v_col_blk)),
                  kp_spec, vp_spec, pl.BlockSpec((1, GW), lambda b, i: (0, 0))],
        out_specs=pl.BlockSpec((tq, GW), lambda b, i: (b * nq + i, 0)),
        out_shape=jax.ShapeDtypeStruct((nseq * t, GW), F32),
        scratch_shapes=[pltpu.VMEM((npairs, 2 * tq, 2 * HD), BF16), pltpu.VMEM((npairs, 2 * tq, 2 * HD), F32),
                        pltpu.VMEM((npairs, 2 * tq, tk), F32)],
        compiler_params=_cparams("parallel", "arbitrary"),
        name="sb_attn",
    )(qn, kn, z, kp_arr, vp_arr, out_norm)


def _rmsgain_kernel(x_ref, g_ref, o_ref):
    o_ref[...] = _rms_rows(x_ref[...], g_ref[...])


def _rmsgain_call(x, gain, tm=512):
    n, d = x.shape
    return pl.pallas_call(
        _rmsgain_kernel,
        grid=(n // tm,),
        in_specs=[pl.BlockSpec((tm, d), lambda i: (i, 0)), pl.BlockSpec((1, d), lambda i: (0, 0))],
        out_specs=pl.BlockSpec((tm, d), lambda i: (i, 0)),
        out_shape=jax.ShapeDtypeStruct((n, d), F32),
        compiler_params=_cparams("parallel"),
        name="rmsgain",
    )(x, gain)


def _rwkv_prep_kernel(zr_ref, zk_ref, zv_ref, zwa_ref, zg_ref, first_ref, mu_ref, w0_ref, w2_ref, a0_ref, a2_ref,
                      g2_ref, kk_ref, ka_ref, rk_ref, gmat_ref,
                      r_out, lw_out, k_out, v_out, kk_out, b_out, g_out, bonus_out, prev_ref, *, tm, wr):
    c = pl.program_id(1)
    cols = prev_ref.shape[1]

    @pl.when(c == 0)
    def _():
        prev_ref[...] = first_ref[0]

    first_row = _iota2(tm, 1, 0) == 0

    def shifted(ref, lo, width):
        x = ref[...]
        prev = jnp.where(first_row, prev_ref[:, lo:lo + width], pltpu.roll(x, 1, 0))
        return x + (prev - x) * mu_ref[:, lo:lo + width]

    r = shifted(zr_ref, 0, GW)
    k = shifted(zk_ref, GW, GW)
    v = shifted(zv_ref, 2 * GW, GW)
    wa = shifted(zwa_ref, 3 * GW, 2 * wr)
    gl = shifted(zg_ref, 3 * GW + 2 * wr, cols - 3 * GW - 2 * wr)
    for ref, lo in ((zr_ref, 0), (zk_ref, GW), (zv_ref, 2 * GW), (zwa_ref, 3 * GW), (zg_ref, 3 * GW + 2 * wr)):
        prev_ref[:, lo:lo + ref.shape[1]] = ref[tm - 1:tm, :]

    xw = w0_ref[...] + _dot(jnp.tanh(wa), w2_ref[...])
    w_log = -(jnp.maximum(-xw, 0.0) + jnp.log(1.0 + jnp.exp(-jnp.abs(xw)))) - 0.5
    a = jax.nn.sigmoid(a0_ref[...] + _dot(wa, a2_ref[...]))
    g_out[...] = _dot(jax.nn.sigmoid(gl), g2_ref[...])
    gmat = gmat_ref[...]
    kk = k * kk_ref[...]
    kk = kk / jnp.maximum(jnp.sqrt(_dot2_exact_rhs(kk * kk, gmat)), 1e-12)
    k = k * (1.0 + (a - 1.0) * ka_ref[...])
    r_out[...] = r
    lw_out[...] = -jnp.exp(w_log)
    k_out[...] = k
    v_out[...] = v
    kk_out[...] = kk
    b_out[...] = kk * a
    bonus_out[...] = _dot2_exact_rhs(r * k * rk_ref[...], gmat) * v


def _rwkv_prep_call(z, row0, nseq, t, tm, first, mu, w0, w2p, a0, a2p, g2, k_k, k_a, r_k, gmat, wr):
    nct = t // tm
    blk0 = row0 // tm
    cols = mu.shape[-1]
    zrow = lambda blkw, off: pl.BlockSpec((tm, blkw), lambda b, c: (blk0 + b * nct + c, off))
    c0 = (z.shape[1] - cols)
    const = lambda shp: pl.BlockSpec(shp, lambda b, c: (0,) * len(shp))
    orow = pl.BlockSpec((tm, GW), lambda b, c: (b * nct + c, 0))
    return pl.pallas_call(
        functools.partial(_rwkv_prep_kernel, tm=tm, wr=wr),
        grid=(nseq, nct),
        in_specs=[zrow(GW, c0 // GW), zrow(GW, c0 // GW + 1), zrow(GW, c0 // GW + 2),
                  zrow(2 * wr, (c0 + 3 * GW) // (2 * wr)), zrow(cols - 3 * GW - 2 * wr, (c0 + 3 * GW) // (2 * wr) + 1),
                  pl.BlockSpec((1, 1, cols), lambda b, c: (b, 0, 0)),
                  const((1, cols)), const((1, GW)), const((2 * wr, GW)), const((1, GW)), const((2 * wr, GW)),
                  const((cols - 3 * GW - 2 * wr, GW)), const((1, GW)), const((1, GW)), const((1, GW)),
                  const((GW, GW))],
        out_specs=[orow] * 8,
        out_shape=[jax.ShapeDtypeStruct((nseq * t, GW), F32)] * 8,
        scratch_shapes=[pltpu.VMEM((1, cols), F32)],
        compiler_params=_cparams("parallel", "arbitrary"),
        name="rwkv_prep",
    )(z, z, z, z, z, first.reshape(nseq, 1, cols), mu, w0, w2p, a0, a2p, g2, k_k, k_a, r_k, gmat)


def _hat(x, m0):
    return jnp.concatenate([jnp.where(m0, x, 0.0), jnp.where(m0, 0.0, x)], axis=0)


def _rwkv_chunk_kernel(r_ref, lw_ref, k_ref, v_ref, kk_ref, b_ref, rt_out, p3_out, m_out, n_out, *, group):
    C = RWKV_CHUNK
    C2 = 2 * C
    lane = _iota2(1, 2 * HD, 1)
    m0 = lane < HD
    ri = _iota2(C2, C2, 0)
    ci = _iota2(C2, C2, 1)
    same = (ri < C) == (ci < C)
    strict = jnp.logical_and(same, ri > ci)
    incl = jnp.logical_and(same, ri >= ci)
    ltri = jnp.where(_iota2(C, C, 0) >= _iota2(C, C, 1), 1.0, 0.0).astype(BF16)
    eye = _iota2(2 * HD, 2 * HD, 0) == _iota2(2 * HD, 2 * HD, 1)
    eye2 = ri == ci

    def pair_stages(p):
        sl = slice(p * 2 * HD, (p + 1) * 2 * HD)
        lw = lw_ref[:, sl]
        cl = _dot2_exact_lhs(ltri, lw)
        yield
        clast = cl[C - 1:C, :]
        kkt = _hat(kk_ref[:, sl] * jnp.exp(cl - lw), m0)
        rt = _hat(r_ref[:, sl] * jnp.exp(cl), m0)
        einv = jnp.exp(-cl)
        kb = _split(_hat(k_ref[:, sl] * einv, m0))
        bb = _split(_hat(b_ref[:, sl] * einv, m0))
        efin = jnp.exp(clast - cl)
        kh = _split(_hat(k_ref[:, sl] * efin, m0))
        bh = _split(_hat(b_ref[:, sl] * efin, m0))
        vh = _split(_hat(v_ref[:, sl], m0))
        lhs = _split(jnp.concatenate([kkt, rt], axis=0))
        gk = _dot3s(lhs, kb, _NT)
        gb = _dot3s(lhs, bb, _NT)
        yield
        a_kk = jnp.where(strict, gk[:C2], 0.0)
        a_rk = _split(jnp.where(incl, gk[C2:], 0.0))
        a_kb = jnp.where(strict, gb[:C2], 0.0)
        a_rb = _split(jnp.where(incl, gb[C2:], 0.0))
        p1 = _dot3s(_split(a_kk), vh)
        tinv = jnp.where(eye2, 1.0, 0.0) - a_kb
        lp = _split(a_kb)
        n = 2
        while n < C:
            yield
            lpf = _dot3s(lp, lp)
            lp = _split(lpf)
            tinv = tinv + _dot3s(_split(tinv), lp)
            n *= 2
        yield
        x = _dot3s(_split(tinv), _split(jnp.concatenate([kkt, p1], axis=1)))
        yield
        xs = _split(x)
        kt = (xs[0][:, :2 * HD], xs[1][:, :2 * HD])
        p2 = (xs[0][:, 2 * HD:], xs[1][:, 2 * HD:])
        y2 = _dot3s(a_rb, xs)
        p3 = _dot3s(a_rk, vh)
        mm = _dot3s(kt, bh, _TN)
        nn = _dot3s(vh, kh, _TN) - _dot3s(p2, bh, _TN)
        yield
        rt_out[p] = rt - y2[:, :2 * HD]
        p3_out[p] = p3 - y2[:, 2 * HD:]
        m_out[p] = jnp.where(eye, jnp.exp(clast), 0.0) - mm
        n_out[p] = nn

    npairs = GW // (2 * HD)
    for p0 in range(0, npairs, group):
        gens = [pair_stages(p) for p in range(p0, p0 + group)]
        while gens:
            for g in list(gens):
                try:
                    next(g)
                except StopIteration:
                    gens.remove(g)


def _rwkv_chunk_call(r, lw, k, v, kk, b):
    n = r.shape[0]
    C = RWKV_CHUNK
    nc = n // C
    npairs = GW // (2 * HD)
    row = pl.BlockSpec((C, GW), lambda i: (i, 0))
    blk = lambda rows: pl.BlockSpec((None, npairs, rows, 2 * HD), lambda i: (i, 0, 0, 0))
    shp = lambda rows: jax.ShapeDtypeStruct((nc, npairs, rows, 2 * HD), F32)
    return pl.pallas_call(
        functools.partial(_rwkv_chunk_kernel, group=4),
        grid=(nc,),
        in_specs=[row] * 6,
        out_specs=[blk(2 * C), blk(2 * C), blk(2 * HD), blk(2 * HD)],
        out_shape=[shp(2 * C), shp(2 * C), shp(2 * HD), shp(2 * HD)],
        compiler_params=_cparams("parallel"),
        name="rwkv_chunk",
    )(r, lw, k, v, kk, b)


def _rwkv_seq_kernel(rt_ref, p3_ref, m_ref, n_ref, s0_ref, g_ref, bonus_ref, lnw_ref, lnb_ref, gmat_ref,
                     o_ref, s_out_ref, s_ref, *, nct):
    c = pl.program_id(1)
    C = RWKV_CHUNK

    @pl.when(c == 0)
    def _():
        s_ref[...] = s0_ref[0]

    gm = gmat_ref[0:2 * HD, 0:2 * HD]
    npairs = GW // (2 * HD)
    pairs = [slice(p * 2 * HD, (p + 1) * 2 * HD) for p in range(npairs)]
    ss = [_split(s_ref[p]) for p in range(npairs)]
    yhs = [_dot3s(_split(rt_ref[p]), ss[p], _NT) + p3_ref[p] for p in range(npairs)]
    snew = [_dot3s(ss[p], _split(m_ref[p])) + n_ref[p] for p in range(npairs)]
    ys = [yh[:C] + yh[C:] for yh in yhs]
    means = [_dot2_exact_rhs(y, gm) * (1.0 / HD) for y in ys]
    ds = [y - m for y, m in zip(ys, means)]
    vrs = [_dot2_exact_rhs(d * d, gm) * (1.0 / HD) for d in ds]
    for p, sl in enumerate(pairs):
        yn = ds[p] * lax.rsqrt(vrs[p] + GN_EPS) * lnw_ref[:, sl] + lnb_ref[:, sl]
        o_ref[:, sl] = (yn + bonus_ref[:, sl]) * g_ref[:, sl]
        s_ref[p] = snew[p]

    @pl.when(c == nct - 1)
    def _():
        s_out_ref[0] = s_ref[...]


def _rwkv_seq_call(rt, p3, m, n, chunk0, nseq, nct, s0, g, bonus, row0, lnx_w, lnx_b, gmat):
    C = RWKV_CHUNK
    npairs = GW // (2 * HD)
    blk0 = row0 // C
    cblk = lambda rows: pl.BlockSpec((None, npairs, rows, 2 * HD), lambda b, c: (chunk0 + b * nct + c, 0, 0, 0))
    sblk = pl.BlockSpec((1, npairs, 2 * HD, 2 * HD), lambda b, c: (b, 0, 0, 0))
    rowin = pl.BlockSpec((C, GW), lambda b, c: (blk0 + b * nct + c, 0))
    const = lambda shp: pl.BlockSpec(shp, lambda b, c: (0,) * len(shp))
    return pl.pallas_call(
        functools.partial(_rwkv_seq_kernel, nct=nct),
        grid=(nseq, nct),
        in_specs=[cblk(2 * C), cblk(2 * C), cblk(2 * HD), cblk(2 * HD), sblk, rowin, rowin,
                  const((1, GW)), const((1, GW)), const((GW, GW))],
        out_specs=[pl.BlockSpec((C, GW), lambda b, c: (b * nct + c, 0)), sblk],
        out_shape=[jax.ShapeDtypeStruct((nseq * nct * C, GW), F32),
                   jax.ShapeDtypeStruct((nseq, npairs, 2 * HD, 2 * HD), F32)],
        scratch_shapes=[pltpu.VMEM((npairs, 2 * HD, 2 * HD), F32)],
        compiler_params=_cparams("parallel", "arbitrary"),
        name="rwkv_seq",
    )(rt, p3, m, n, s0, g, bonus, lnx_w, lnx_b, gmat)


def _pair_states(s):
    b, h = s.shape[:2]
    s = s.reshape(b, h // 2, 2, HD, HD)
    z = jnp.zeros_like(s[:, :, 0])
    top = jnp.concatenate([s[:, :, 0], z], axis=-1)
    bot = jnp.concatenate([z, s[:, :, 1]], axis=-1)
    return jnp.concatenate([top, bot], axis=-2)


def _unpair_states(s2):
    b, hp = s2.shape[:2]
    return jnp.stack([s2[:, :, :HD, :HD], s2[:, :, HD:, HD:]], axis=2).reshape(b, 2 * hp, HD, HD)


def _forward(xp, xs, cache_k, cache_v, s5_re0, s5_im0, wkv0, shift0, W):
    Bp, Tp, D = xp.shape
    Bs, Ts, _ = xs.shape
    depth = W['ln_ffn1'].shape[0]
    past_len = cache_k.shape[2]
    Np, Ns = Bp * Tp, Bs * Ts
    n_heads_d = GW // HD
    wr = W['rwkv_w2'].shape[1]
    dcols = W['rwkv_mu'].shape[-1]

    x = jnp.concatenate([xp.reshape(Np, D), xs.reshape(Ns, D)], axis=0)
    cache_kb = cache_k.reshape(depth, Bs, past_len, GW)
    cache_vb = cache_v.reshape(depth, Bs, past_len, GW)
    wbf = {k: _bf(W[k]) for k in ('w_ffn1_gate', 'w_ffn1_up', 'w_ffn1_down', 'w_in', 'w_out',
                                  'w_ffn2_gate', 'w_ffn2_up', 'w_ffn2_down')}
    hid = lax.broadcasted_iota(jnp.int32, (GW, GW), 0) // HD
    gmat = (hid == jnp.transpose(hid)).astype(BF16)
    grp = lax.broadcasted_iota(jnp.int32, (GW, S5_STATE), 0) // (GW // 32) == \
        lax.broadcasted_iota(jnp.int32, (GW, S5_STATE), 1) // (S5_STATE // 32)
    zeros_first = jnp.zeros((Bp, dcols), F32)
    zeros_s5 = jnp.zeros((Bp, S5_STATE), F32)
    zeros_wkv = jnp.zeros((Bp, n_heads_d // 2, 2 * HD, 2 * HD), F32)

    h = _rms_call(x, W['ln_ffn1'][0][None])
    outs = {k: [] for k in ('kp', 'vp', 'ks', 'vs', 's5rp', 's5ip', 's5rs', 's5is', 'wkvp', 'wkvs', 'shp', 'shs', 'gv')}
    for l in range(depth):
        g1 = lambda name: W[name][l][None]
        x, h = _ffn2_call(x, h, wbf['w_ffn1_gate'], wbf['w_ffn1_up'], wbf['w_ffn1_down'], g1('ln_mix'), l)
        z = _inproj_call(h, wbf['w_in'], l)

        oa_p, _ = _gmlp_call(z, 0, Np, min(Tp, GMLP_CHUNK), g1('gmlp_v_norm'), W['gmlp_ws'][l], W['gmlp_b'][l],
                             g1('out_norm_a'))
        oa_s, gv_s = _gmlp_call(z, Np, Ns, min(Ts, GMLP_CHUNK), g1('gmlp_v_norm'), W['gmlp_ws'][l], W['gmlp_b'][l],
                                g1('out_norm_a'))

        pw_re, pw_im, bb_re, bb_im = _s5_params(W['s5_lam_re'][l], W['s5_lam_im'][l], W['s5_log_dt'][l],
                                                W['s5_b_re'][l], W['s5_b_im'][l])
        bblk = jnp.concatenate([jnp.where(grp, jnp.tile(bb_re, (32, 1)), 0.0),
                                jnp.where(grp, jnp.tile(bb_im, (32, 1)), 0.0)], axis=1).astype(BF16)
        grp_t = jnp.transpose(grp)
        cre = jnp.where(grp_t, jnp.tile(jnp.transpose(W['s5_c_re'][l], (0, 2, 1)).reshape(S5_STATE, -1), (1, 32)),
                        0.0).astype(BF16)
        cim = jnp.where(grp_t, jnp.tile(jnp.transpose(W['s5_c_im'][l], (0, 2, 1)).reshape(S5_STATE, -1), (1, 32)),
                        0.0).astype(BF16)
        s5_args = (pw_re, pw_im, bblk, cre, cim, g1('s5_d'), W['s5_w_glu'][l], g1('s5_b_glu'), g1('out_norm_b'))
        ob_p, s5rp, s5ip = _s5_call(z, 0, Bp, Tp, zeros_s5, zeros_s5, *s5_args, col_blk=2, tc=min(Tp, 256))
        ob_s, s5rs, s5is = _s5_call(z, Np, Bs, Ts, s5_re0[l].reshape(Bs, S5_STATE), s5_im0[l].reshape(Bs, S5_STATE),
                                    *s5_args, col_blk=2, tc=min(Ts, 256))

        qg = jnp.tile(W['sb_q_norm'][l], GW // HD)[None]
        kg = jnp.tile(W['sb_k_norm'][l], GW // HD)[None]
        qb, kn, kb, vb = _sbprep_call(z, gmat, qg, kg)
        oc_p = _sb_call(qb, kb, vb, None, None, 0, Bp, Tp, min(Tp, SB_TQ), min(Tp, SB_TK), 0, 0,
                        g1('out_norm_c'))
        oc_s = _sb_call(qb, kb, vb, cache_kb, cache_vb, Np, Bs, Ts, Ts, min(past_len, SB_TK_SAMPLE), past_len, 0,
                        g1('out_norm_c'), layer=l)
        vcol = z[:, 5 * GW:6 * GW]

        w2p = jnp.concatenate([W['rwkv_w2'][l], jnp.zeros_like(W['rwkv_a2'][l])], axis=0)
        a2p = jnp.concatenate([jnp.zeros_like(W['rwkv_w2'][l]), W['rwkv_a2'][l]], axis=0)
        rk = W['rwkv_r_k'][l].reshape(1, GW)
        prep_args = (g1('rwkv_mu'), g1('rwkv_w0'), w2p, g1('rwkv_a0'), a2p, W['rwkv_g2'][l], g1('rwkv_k_k'),
                     g1('rwkv_k_a'), rk, gmat, wr)
        pp = _rwkv_prep_call(z, 0, Bp, Tp, min(Tp, 512), zeros_first, *prep_args)
        ps = _rwkv_prep_call(z, Np, Bs, Ts, Ts, shift0[l], *prep_args)
        seq_args = (g1('rwkv_lnx_w'), g1('rwkv_lnx_b'), gmat)
        od_p, wkv_p = _rwkv_seq_call(*_rwkv_chunk_call(*pp[:6]), 0, Bp, Tp // RWKV_CHUNK, zeros_wkv, pp[6], pp[7], 0,
                                     *seq_args)
        od_s, wkv_s = _rwkv_seq_call(*_rwkv_chunk_call(*ps[:6]), 0, Bs, Ts // RWKV_CHUNK, _pair_states(wkv0[l]),
                                     ps[6], ps[7], 0, *seq_args)

        gain_next = g1('ln_ffn2')
        x, h = _outproj_call(x, (oa_p, ob_p, oc_p, od_p), (oa_s, ob_s, oc_s, od_s), wbf['w_out'], gain_next, l)
        gain_next = W['ln_ffn1'][l + 1][None] if l + 1 < depth else g1('ln_ffn2')
        x, h = _ffn2_call(x, h, wbf['w_ffn2_gate'], wbf['w_ffn2_up'], wbf['w_ffn2_down'], gain_next, l)

        nh = GW // HD
        outs['kp'].append(kn[:Np].reshape(Bp, Tp, nh, HD))
        outs['vp'].append(vcol[:Np].reshape(Bp, Tp, nh, HD))
        outs['ks'].append(kn[Np:].reshape(Bs, Ts, nh, HD))
        outs['vs'].append(vcol[Np:].reshape(Bs, Ts, nh, HD))
        outs['s5rp'].append(s5rp.reshape(Bp, 32, -1))
        outs['s5ip'].append(s5ip.reshape(Bp, 32, -1))
        outs['s5rs'].append(s5rs.reshape(Bs, 32, -1))
        outs['s5is'].append(s5is.reshape(Bs, 32, -1))
        outs['wkvp'].append(_unpair_states(wkv_p))
        outs['wkvs'].append(_unpair_states(wkv_s))
        zd = z[:, z.shape[1] - dcols:]
        outs['shp'].append(zd[:Np].reshape(Bp, Tp, dcols)[:, -1])
        outs['shs'].append(zd[Np:].reshape(Bs, Ts, dcols)[:, -1])
        outs['gv'].append(gv_s.reshape(Bs, Ts, GW))

    st = lambda k: jnp.stack(outs[k], axis=0)
    return (x[:Np].reshape(Bp, Tp, D), x[Np:].reshape(Bs, Ts, D),
            st('kp'), st('vp'), st('ks'), st('vs'),
            st('s5rp'), st('s5ip'), st('s5rs'), st('s5is'),
            st('wkvp'), st('wkvs'), st('shp'), st('shs'), st('gv'))


_WEIGHT_NAMES = ('ln_ffn1', 'w_ffn1_gate', 'w_ffn1_up', 'w_ffn1_down', 'ln_mix', 'w_in',
                 'gmlp_v_norm', 'gmlp_ws', 'gmlp_b', 'out_norm_a',
                 's5_lam_re', 's5_lam_im', 's5_log_dt', 's5_b_re', 's5_b_im', 's5_c_re', 's5_c_im', 's5_d',
                 's5_w_glu', 's5_b_glu', 'out_norm_b',
                 'sb_q_norm', 'sb_k_norm', 'out_norm_c',
                 'rwkv_mu', 'rwkv_w0', 'rwkv_w2', 'rwkv_a0', 'rwkv_a2', 'rwkv_g2', 'rwkv_k_k', 'rwkv_k_a', 'rwkv_r_k',
                 'rwkv_lnx_w', 'rwkv_lnx_b',
                 'w_out', 'ln_ffn2', 'w_ffn2_gate', 'w_ffn2_up', 'w_ffn2_down')


def kernel(x_prompt, x_sample, cache_sb_k, cache_sb_v, state_s5_re, state_s5_im, state_rwkv_wkv, state_rwkv_shift,
           ln_ffn1, w_ffn1_gate, w_ffn1_up, w_ffn1_down, ln_mix, w_in,
           gmlp_v_norm, gmlp_ws, gmlp_b, out_norm_a,
           s5_lam_re, s5_lam_im, s5_log_dt, s5_b_re, s5_b_im, s5_c_re, s5_c_im, s5_d, s5_w_glu, s5_b_glu, out_norm_b,
           sb_q_norm, sb_k_norm, out_norm_c,
           rwkv_mu, rwkv_w0, rwkv_w2, rwkv_a0, rwkv_a2, rwkv_g2, rwkv_k_k, rwkv_k_a, rwkv_r_k, rwkv_lnx_w, rwkv_lnx_b,
           w_out, ln_ffn2, w_ffn2_gate, w_ffn2_up, w_ffn2_down):
    weights = (ln_ffn1, w_ffn1_gate, w_ffn1_up, w_ffn1_down, ln_mix, w_in,
               gmlp_v_norm, gmlp_ws, gmlp_b, out_norm_a,
               s5_lam_re, s5_lam_im, s5_log_dt, s5_b_re, s5_b_im, s5_c_re, s5_c_im, s5_d, s5_w_glu, s5_b_glu,
               out_norm_b, sb_q_norm, sb_k_norm, out_norm_c,
               rwkv_mu, rwkv_w0, rwkv_w2, rwkv_a0, rwkv_a2, rwkv_g2, rwkv_k_k, rwkv_k_a, rwkv_r_k, rwkv_lnx_w,
               rwkv_lnx_b, w_out, ln_ffn2, w_ffn2_gate, w_ffn2_up, w_ffn2_down)
    W = dict(zip(_WEIGHT_NAMES, weights))
    return _forward(x_prompt, x_sample, cache_sb_k, cache_sb_v, state_s5_re, state_s5_im, state_rwkv_wkv,
                    state_rwkv_shift, W)
```

```python
import functools

import jax
import jax.numpy as jnp
from jax import lax
from jax.experimental import pallas as pl
from jax.experimental.pallas import tpu as pltpu

F32 = jnp.float32
BF16 = jnp.bfloat16

RMS_EPS = 1e-6
GN_EPS = 64e-5
A_HEADS = 4
GMLP_CHUNK = 128
HD = 64
GW = 512
S5_STATE = 2048
RWKV_CHUNK = 64
SB_TQ, SB_TK = 256, 256
SB_TK_SAMPLE = 256
SB_CACHE_CHUNK = 1024
VMEM_LIMIT = 56 * 1024 * 1024


def _cparams(*sem):
    return pltpu.CompilerParams(dimension_semantics=sem, vmem_limit_bytes=VMEM_LIMIT)


def _bf(x):
    return x.astype(BF16)


_NN = (((1,), (0,)), ((), ()))
_NT = (((1,), (1,)), ((), ()))
_TN = (((0,), (0,)), ((), ()))


def _dot(a, b, dims=_NN):
    return lax.dot_general(_bf(a), _bf(b), dims, preferred_element_type=F32)


def _split(x):
    hi = _bf(x)
    lo = _bf(x - hi.astype(F32))
    return hi, lo


def _dot3(a, b, dims=_NN):
    ah, al = _split(a)
    bh, bl = _split(b)
    d = functools.partial(lax.dot_general, dimension_numbers=dims, preferred_element_type=F32)
    return d(ah, bh) + (d(ah, bl) + d(al, bh))


def _dot3s(a, b, dims=_NN):
    d = functools.partial(lax.dot_general, dimension_numbers=dims, preferred_element_type=F32)
    return d(a[0], b[0]) + (d(a[0], b[1]) + d(a[1], b[0]))


def _dot2_exact_rhs(a, b_bf, dims=_NN):
    ah, al = _split(a)
    d = functools.partial(lax.dot_general, dimension_numbers=dims, preferred_element_type=F32)
    return d(ah, b_bf) + d(al, b_bf)


def _dot2_exact_lhs(a_bf, b, dims=_NN):
    bh, bl = _split(b)
    d = functools.partial(lax.dot_general, dimension_numbers=dims, preferred_element_type=F32)
    return d(a_bf, bh) + d(a_bf, bl)


def _rms_rows(x, gain):
    ms = jnp.mean(x * x, axis=-1, keepdims=True)
    return x * lax.rsqrt(ms + RMS_EPS) * gain


def _iota2(n, m, axis):
    return lax.broadcasted_iota(jnp.int32, (n, m), axis)


def _rms_kernel(x_ref, g_ref, h_ref):
    h_ref[...] = _bf(_rms_rows(x_ref[...], g_ref[...]))


def _rms_call(x, gain, tm=512):
    n, d = x.shape
    return pl.pallas_call(
        _rms_kernel,
        grid=(n // tm,),
        in_specs=[pl.BlockSpec((tm, d), lambda i: (i, 0)), pl.BlockSpec((1, d), lambda i: (0, 0))],
        out_specs=pl.BlockSpec((tm, d), lambda i: (i, 0)),
        out_shape=jax.ShapeDtypeStruct((n, d), BF16),
        compiler_params=_cparams("parallel"),
        name="rms",
    )(x, gain)


def _ffn_kernel(x_ref, h_ref, wg_ref, wu_ref, wd_ref, gn_ref, o_ref, hn_ref, *, nj):
    j = pl.program_id(1)

    @pl.when(j == 0)
    def _():
        o_ref[...] = jnp.zeros_like(o_ref)

    h = h_ref[...]
    g = jnp.dot(h, wg_ref[...], preferred_element_type=F32)
    u = jnp.dot(h, wu_ref[...], preferred_element_type=F32)
    a = _bf(g * jax.nn.sigmoid(g) * u)
    o_ref[...] += jnp.dot(a, wd_ref[...], preferred_element_type=F32)

    @pl.when(j == nj - 1)
    def _():
        y = x_ref[...] + 0.5 * o_ref[...]
        o_ref[...] = y
        hn_ref[...] = _bf(_rms_rows(y, gn_ref[...]))


def _ffn_call(x, h, wg, wu, wd, gain_next, layer, tm=512, tf=512):
    n, d = x.shape
    ff = wg.shape[-1]
    nj = ff // tf
    return pl.pallas_call(
        functools.partial(_ffn_kernel, nj=nj),
        grid=(n // tm, nj),
        in_specs=[
            pl.BlockSpec((tm, d), lambda i, j: (i, 0)),
            pl.BlockSpec((tm, d), lambda i, j: (i, 0)),
            pl.BlockSpec((None, d, tf), lambda i, j: (layer, 0, j)),
            pl.BlockSpec((None, d, tf), lambda i, j: (layer, 0, j)),
            pl.BlockSpec((None, tf, d), lambda i, j: (layer, j, 0)),
            pl.BlockSpec((1, d), lambda i, j: (0, 0)),
        ],
        out_specs=[pl.BlockSpec((tm, d), lambda i, j: (i, 0)), pl.BlockSpec((tm, d), lambda i, j: (i, 0))],
        out_shape=[jax.ShapeDtypeStruct((n, d), F32), jax.ShapeDtypeStruct((n, d), BF16)],
        compiler_params=_cparams("parallel", "arbitrary"),
        name="ffn",
    )(x, h, wg, wu, wd, gain_next)


def _inproj_kernel(h_ref, w_ref, z_ref):
    z_ref[...] = jnp.dot(h_ref[...], w_ref[...], preferred_element_type=F32)


def _inproj_call(h, w_in, layer, tm=512):
    n, d = h.shape
    cols = w_in.shape[-1]
    tn = cols // 2
    return pl.pallas_call(
        _inproj_kernel,
        grid=(2, n // tm),
        in_specs=[pl.BlockSpec((tm, d), lambda j, i: (i, 0)),
                  pl.BlockSpec((None, d, tn), lambda j, i: (layer, 0, j))],
        out_specs=pl.BlockSpec((tm, tn), lambda j, i: (i, j)),
        out_shape=jax.ShapeDtypeStruct((n, cols), F32),
        compiler_params=_cparams("parallel", "parallel"),
        name="inproj",
    )(h, w_in)


def _outproj_kernel(x_ref, *refs, n_first):
    first, second = refs[0:4], refs[4:8]
    w_ref, gn_ref, o_ref, hn_ref = refs[8:]

    def run(mix_refs):
        acc = x_ref[...]
        for i, r in enumerate(mix_refs):
            acc = acc + jnp.dot(_bf(r[...]), w_ref[i * GW:(i + 1) * GW, :], preferred_element_type=F32)
        o_ref[...] = acc
        hn_ref[...] = _bf(_rms_rows(acc, gn_ref[...]))

    i = pl.program_id(0)
    pl.when(i < n_first)(lambda: run(first))
    pl.when(i >= n_first)(lambda: run(second))


def _outproj_call(x, mix_first, mix_second, w_out, gain_next, layer, tm=256):
    n, d = x.shape
    n_first = mix_first[0].shape[0] // tm
    n_second = mix_second[0].shape[0] // tm
    row = lambda i: (i, 0)
    first_row = lambda i: (jnp.minimum(i, n_first - 1), 0)
    second_row = lambda i: (jnp.maximum(i - n_first, 0), 0)
    return pl.pallas_call(
        functools.partial(_outproj_kernel, n_first=n_first),
        grid=(n_first + n_second,),
        in_specs=[pl.BlockSpec((tm, d), row)] + [pl.BlockSpec((tm, GW), first_row)] * 4
        + [pl.BlockSpec((tm, GW), second_row)] * 4
        + [pl.BlockSpec((None, d, d), lambda i: (layer, 0, 0)), pl.BlockSpec((1, d), lambda i: (0, 0))],
        out_specs=[pl.BlockSpec((tm, d), row), pl.BlockSpec((tm, d), row)],
        out_shape=[jax.ShapeDtypeStruct((n, d), F32), jax.ShapeDtypeStruct((n, d), BF16)],
        compiler_params=_cparams("parallel"),
        name="outproj",
    )(x, *mix_first, *mix_second, w_out, gain_next)


def _gmlp_kernel(z_ref, vn_ref, ws_ref, bt_ref, on_ref, o_ref, v_ref, *, L):
    z = jax.nn.gelu(z_ref[...])
    causal = _iota2(L, L, 1) <= _iota2(L, L, 0)
    hw = GW // A_HEADS
    ss = jnp.zeros((L, 1), F32)
    outs = []
    for h in range(A_HEADS):
        vh = _rms_rows(z[:, GW + h * hw:GW + (h + 1) * hw], vn_ref[...])
        v_ref[:, h * hw:(h + 1) * hw] = vh
        w = jnp.where(causal, ws_ref[h], 0.0)
        s = _dot(w, vh) + bt_ref[:, h:h + 1]
        oh = z[:, h * hw:(h + 1) * hw] * s
        ss = ss + jnp.sum(oh * oh, axis=-1, keepdims=True)
        outs.append(oh)
    scale = lax.rsqrt(ss * (1.0 / GW) + RMS_EPS)
    for h in range(A_HEADS):
        o_ref[:, h * hw:(h + 1) * hw] = outs[h] * scale * on_ref[:, h * hw:(h + 1) * hw]


def _gmlp_call(z, row0, nrows, L, v_norm, ws, b, out_norm):
    nb = nrows // L
    b0 = row0 // L
    ws_l = ws[:, :L, :L]
    bt = jnp.transpose(b[:, :L])
    row = lambda i: (i, 0)
    return pl.pallas_call(
        functools.partial(_gmlp_kernel, L=L),
        grid=(nb,),
        in_specs=[pl.BlockSpec((L, 2 * GW), lambda i: (b0 + i, 0)),
                  pl.BlockSpec((1, GW // A_HEADS), lambda i: (0, 0)),
                  pl.BlockSpec((A_HEADS, L, L), lambda i: (0, 0, 0)),
                  pl.BlockSpec((L, A_HEADS), lambda i: (0, 0)),
                  pl.BlockSpec((1, GW), lambda i: (0, 0))],
        out_specs=[pl.BlockSpec((L, GW), row), pl.BlockSpec((L, GW), row)],
        out_shape=[jax.ShapeDtypeStruct((nrows, GW), F32), jax.ShapeDtypeStruct((nrows, GW), F32)],
        compiler_params=_cparams("parallel"),
        name="gmlp",
    )(z, v_norm, ws_l, bt, out_norm)


def _s5_param_kernel(lr_ref, li_ref, ldt_ref, brt_ref, bit_ref, pw_re_ref, pw_im_ref, bb_re_ref, bb_im_ref):
    lr = lr_ref[...]
    li = li_ref[...]
    dt = jnp.exp(ldt_ref[...])
    n = (_iota2(8, S5_STATE, 0) + 1).astype(F32)
    mag = jnp.exp(n * (lr * dt))
    ang = n * (li * dt)
    pw_re = mag * jnp.cos(ang)
    pw_im = mag * jnp.sin(ang)
    pw_re_ref[...] = pw_re
    pw_im_ref[...] = pw_im
    ab_re = pw_re[0:1, :]
    ab_im = pw_im[0:1, :]
    den = lr * lr + li * li
    nr, ni = ab_re - 1.0, ab_im
    cf_re = (nr * lr + ni * li) / den
    cf_im = (ni * lr - nr * li) / den
    br = brt_ref[...]
    bi = bit_ref[...]
    bb_re_ref[...] = cf_re * br - cf_im * bi
    bb_im_ref[...] = cf_re * bi + cf_im * br


def _s5_params(lam_re, lam_im, log_dt, b_re, b_im):
    g, p = lam_re.shape
    s = g * p
    ch = b_re.shape[-1]
    ldt = jnp.broadcast_to(log_dt[:, None], (g, p)).reshape(1, s)
    brt = jnp.transpose(b_re.reshape(s, ch))
    bit = jnp.transpose(b_im.reshape(s, ch))
    return pl.pallas_call(
        _s5_param_kernel,
        out_shape=[jax.ShapeDtypeStruct((8, s), F32), jax.ShapeDtypeStruct((8, s), F32),
                   jax.ShapeDtypeStruct((ch, s), F32), jax.ShapeDtypeStruct((ch, s), F32)],
        name="s5_params",
    )(lam_re.reshape(1, s), lam_im.reshape(1, s), ldt, brt, bit)


def _cmul(ar, ai, xr, xi):
    return ar * xr - ai * xi, ar * xi + ai * xr


def _s5_kernel(u_ref, h0r_ref, h0i_ref, pwr_ref, pwi_ref, bblk_ref, cre_ref, cim_ref, d_ref, wglu_ref, bglu_ref,
               on_ref, o_ref, hr_out_ref, hi_out_ref, bu_ref, hre_ref, him_ref, cr_ref, ci_ref, *, tc, nct):
    c = pl.program_id(1)
    S = S5_STATE
    LW = 512

    @pl.when(c == 0)
    def _():
        cr_ref[...] = h0r_ref[0]
        ci_ref[...] = h0i_ref[0]

    u = u_ref[...]
    NK = GW // 128
    SW = S // NK
    for kc in range(NK):
        ukc = _bf(u[:, kc * 128:(kc + 1) * 128])
        for half in range(2):
            cs = slice(half * S + kc * SW, half * S + (kc + 1) * SW)
            bu_ref[:, cs] = jnp.dot(ukc, bblk_ref[kc * 128:(kc + 1) * 128, cs], preferred_element_type=F32)

    rows = _iota2(8, LW, 0)
    for lc in range(S // LW):
        sl = slice(lc * LW, (lc + 1) * LW)
        pr = pwr_ref[:, sl]
        pi = pwi_ref[:, sl]
        a1r = jnp.where(rows >= 1, pr[0:1, :], 0.0)
        a1i = jnp.where(rows >= 1, pi[0:1, :], 0.0)
        a2r = jnp.where(rows >= 2, pr[1:2, :], 0.0)
        a2i = jnp.where(rows >= 2, pi[1:2, :], 0.0)
        a4r = jnp.where(rows >= 4, pr[3:4, :], 0.0)
        a4i = jnp.where(rows >= 4, pi[3:4, :], 0.0)

        def tile(i, carry):
            kr, ki = carry
            r0 = pl.multiple_of(i * 8, 8)
            xr = bu_ref[pl.ds(r0, 8), lc * LW:(lc + 1) * LW]
            xi = bu_ref[pl.ds(r0, 8), S + lc * LW:S + (lc + 1) * LW]
            for (ar, ai, sh) in ((a1r, a1i, 1), (a2r, a2i, 2), (a4r, a4i, 4)):
                sr, si = _cmul(ar, ai, pltpu.roll(xr, sh, 0), pltpu.roll(xi, sh, 0))
                xr, xi = xr + sr, xi + si
            sr, si = _cmul(pr, pi, kr, ki)
            xr, xi = xr + sr, xi + si
            hre_ref[pl.ds(r0, 8), sl] = xr
            him_ref[pl.ds(r0, 8), sl] = xi
            return xr[7:8, :], xi[7:8, :]

        kr, ki = lax.fori_loop(0, tc // 8, tile, (cr_ref[:, sl], ci_ref[:, sl]))
        cr_ref[:, sl] = kr
        ci_ref[:, sl] = ki

    ys = []
    for kc in range(NK):
        ss, os_ = slice(kc * SW, (kc + 1) * SW), slice(kc * 128, (kc + 1) * 128)
        ys.append(jnp.dot(_bf(hre_ref[:, ss]), cre_ref[ss, os_], preferred_element_type=F32)
                  - jnp.dot(_bf(him_ref[:, ss]), cim_ref[ss, os_], preferred_element_type=F32))
    y = jnp.concatenate(ys, axis=1) + d_ref[...] * u
    g = jax.nn.gelu(y)
    out = g * jax.nn.sigmoid(_dot(g, wglu_ref[...]) + bglu_ref[...])
    o_ref[...] = _rms_rows(out, on_ref[...])

    @pl.when(c == nct - 1)
    def _():
        hr_out_ref[0] = cr_ref[...]
        hi_out_ref[0] = ci_ref[...]


def _s5_call(z, row0, nseq, t, h0_re, h0_im, pw_re, pw_im, bblk, c_re, c_im, d_skip, w_glu, b_glu, out_norm,
             col_blk, tc):
    nct = t // tc
    S = S5_STATE
    blk0 = row0 // tc
    const2 = lambda b, c: (0, 0)
    outs = pl.pallas_call(
        functools.partial(_s5_kernel, tc=tc, nct=nct),
        grid=(nseq, nct),
        in_specs=[pl.BlockSpec((tc, GW), lambda b, c: (blk0 + b * nct + c, col_blk)),
                  pl.BlockSpec((1, 1, S), lambda b, c: (b, 0, 0)),
                  pl.BlockSpec((1, 1, S), lambda b, c: (b, 0, 0)),
                  pl.BlockSpec((8, S), const2), pl.BlockSpec((8, S), const2),
                  pl.BlockSpec((GW, 2 * S), const2),
                  pl.BlockSpec((S, GW), const2), pl.BlockSpec((S, GW), const2),
                  pl.BlockSpec((1, GW), const2),
                  pl.BlockSpec((GW, GW), const2), pl.BlockSpec((1, GW), const2), pl.BlockSpec((1, GW), const2)],
        out_specs=[pl.BlockSpec((tc, GW), lambda b, c: (b * nct + c, 0)),
                   pl.BlockSpec((1, 1, S), lambda b, c: (b, 0, 0)),
                   pl.BlockSpec((1, 1, S), lambda b, c: (b, 0, 0))],
        out_shape=[jax.ShapeDtypeStruct((nseq * t, GW), F32),
                   jax.ShapeDtypeStruct((nseq, 1, S), F32), jax.ShapeDtypeStruct((nseq, 1, S), F32)],
        scratch_shapes=[pltpu.VMEM((tc, 2 * S), F32), pltpu.VMEM((tc, S), F32), pltpu.VMEM((tc, S), F32),
                        pltpu.VMEM((1, S), F32), pltpu.VMEM((1, S), F32)],
        compiler_params=_cparams("parallel", "arbitrary"),
        name="s5",
    )(z, h0_re.reshape(nseq, 1, S), h0_im.reshape(nseq, 1, S), pw_re, pw_im, bblk, c_re, c_im,
      d_skip, w_glu, b_glu, out_norm)
    return outs[0], outs[1].reshape(nseq, S), outs[2].reshape(nseq, S)


def _sbprep_kernel(q_ref, k_ref, v_ref, gmat_ref, qg_ref, kg_ref, qb_ref, kn_ref, kb_ref, vb_ref):
    gmat = gmat_ref[...]

    def head_rms(x, gain):
        ms = _dot2_exact_rhs(x * x, gmat) * (1.0 / HD)
        return x * lax.rsqrt(ms + RMS_EPS) * gain

    qb_ref[...] = _bf(head_rms(q_ref[...], qg_ref[...]) * (HD ** -0.5))
    kn = head_rms(k_ref[...], kg_ref[...])
    kn_ref[...] = kn
    kb_ref[...] = _bf(kn)
    vb_ref[...] = _bf(v_ref[...])


def _sbprep_call(z, gmat, q_gain, k_gain, tm=512):
    n = z.shape[0]
    row = pl.BlockSpec((tm, GW), lambda i: (i, 0))
    return pl.pallas_call(
        _sbprep_kernel,
        grid=(n // tm,),
        in_specs=[pl.BlockSpec((tm, GW), lambda i: (i, 3)), pl.BlockSpec((tm, GW), lambda i: (i, 4)),
                  pl.BlockSpec((tm, GW), lambda i: (i, 5)),
                  pl.BlockSpec((GW, GW), lambda i: (0, 0)),
                  pl.BlockSpec((1, GW), lambda i: (0, 0)), pl.BlockSpec((1, GW), lambda i: (0, 0))],
        out_specs=[row, row, row, row],
        out_shape=[jax.ShapeDtypeStruct((n, GW), BF16), jax.ShapeDtypeStruct((n, GW), F32),
                   jax.ShapeDtypeStruct((n, GW), BF16), jax.ShapeDtypeStruct((n, GW), BF16)],
        compiler_params=_cparams("parallel"),
        name="sbprep",
    )(z, z, z, gmat, q_gain, k_gain)


def _sb_tiles(qhs, k2s, v2s, carries, umat, mask, carry_w):
    qk = [functools.partial(lax.dot_general, qhs[p], k2s[p], _NT, preferred_element_type=F32)
          for p in range(len(qhs))]
    pv = [functools.partial(lambda w, v2: jnp.dot(w, v2, preferred_element_type=F32), v2=v2s[p])
          for p in range(len(qhs))]
    return _sb_stages(qk, pv, carries, umat, mask, carry_w)


def _sb_stages(qk, pv, carries, umat, mask, carry_w):
    d = functools.partial(jnp.dot, preferred_element_type=F32)
    n = len(qk)
    zs = [qk[p]() for p in range(n)]
    lks, afts, tots = [], [], []
    for p in range(n):
        z = zs[p]
        lk = -(jnp.maximum(z, 0.0) + jnp.log(1.0 + jnp.exp(-jnp.abs(z))))
        if mask is not None:
            lk = jnp.where(mask, lk, 0.0)
        lks.append(lk)
        aft = d(_bf(lk), umat)
        afts.append(aft)
        tots.append(jnp.broadcast_to(aft[:, 0:1] + lk[:, 0:1], (lk.shape[0], carry_w)))
    pvs = []
    for p in range(n):
        e = zs[p] + lks[p] + afts[p]
        if carries[p] is not None:
            e = e + carries[p]
        w = jnp.exp(e)
        if mask is not None:
            w = jnp.where(mask, w, 0.0)
        pvs.append(pv[p](_bf(w)))
    return pvs, tots


def _sb_kernel(q_ref, kd_ref, vd_ref, kp_ref, vp_ref, on_ref, o_ref, qh_ref, acc_ref, carry_ref,
               *, tq, tk, npast, past_from_grid):
    PW = 2 * HD
    npairs = GW // PW
    m0 = _iota2(1, PW, 1) < HD
    row = _iota2(2 * tq, tq, 0)
    causal = _iota2(2 * tq, tq, 1) < jnp.where(row >= tq, row - tq, row)

    ud = jnp.where(_iota2(tq, tq, 0) > _iota2(tq, tq, 1), 1.0, 0.0).astype(BF16)
    up = jnp.where(_iota2(tk, tk, 0) > _iota2(tk, tk, 1), 1.0, 0.0).astype(BF16)
    pairs = [slice(p * PW, (p + 1) * PW) for p in range(npairs)]

    qhs = [_bf(_hat(q_ref[:, sl], m0)) for sl in pairs]
    pvs, tots = _sb_tiles(qhs, [_bf(kd_ref[:, sl]) for sl in pairs], [_bf(vd_ref[:, sl]) for sl in pairs],
                          [None] * npairs, ud, causal, tk)
    for p in range(npairs):
        qh_ref[p] = qhs[p]
        acc_ref[p] = pvs[p]
        carry_ref[p] = tots[p]

    nblk = pl.program_id(1) * (tq // tk) if past_from_grid else npast

    def body(i, _):
        j = nblk - 1 - i
        r0 = pl.multiple_of(j * tk, tk)
        pvs, tots = _sb_tiles([qh_ref[p] for p in range(npairs)],
                              [_bf(kp_ref[pl.ds(r0, tk), sl]) for sl in pairs],
                              [_bf(vp_ref[pl.ds(r0, tk), sl]) for sl in pairs],
                              [carry_ref[p] for p in range(npairs)], up, None, tk)
        for p in range(npairs):
            acc_ref[p] += pvs[p]
            carry_ref[p] += tots[p]
        return 0

    lax.fori_loop(0, nblk, body, 0)

    outs = []
    ss = jnp.zeros((tq, 1), F32)
    for p in range(npairs):
        a = acc_ref[p]
        o = jnp.where(m0, a[:tq], a[tq:])
        ss = ss + jnp.sum(o * o, axis=-1, keepdims=True)
        outs.append(o)
    scale = lax.rsqrt(ss * (1.0 / GW) + RMS_EPS)
    for p in range(npairs):
        sl = slice(p * PW, (p + 1) * PW)
        o_ref[:, sl] = outs[p] * scale * on_ref[:, sl]


def _sb_call(qn, kn, z, kpast, vpast, row0, nseq, t, tq, tk, past_len, v_col_blk, out_norm, layer=0):
    nq = t // tq
    blk0 = row0 // tq
    npairs = GW // (2 * HD)
    if kpast is None:
        kp_arr, vp_arr = kn, z
        sblk = row0 // t
        kp_spec = pl.BlockSpec((t, GW), lambda b, i: (sblk + b, 0))
        vp_spec = pl.BlockSpec((t, GW), lambda b, i: (sblk + b, v_col_blk))
        npast, from_grid = 0, True
    else:
        kp_arr, vp_arr = kpast, vpast
        kp_spec = pl.BlockSpec((None, None, past_len, GW), lambda b, i: (layer, b, 0, 0))
        vp_spec = kp_spec
        npast, from_grid = past_len // tk, False
    qrow = lambda b, i: (blk0 + b * nq + i, 0)
    return pl.pallas_call(
        functools.partial(_sb_kernel, tq=tq, tk=tk, npast=npast, past_from_grid=from_grid),
        grid=(nseq, nq),
        in_specs=[pl.BlockSpec((tq, GW), qrow), pl.BlockSpec((tq, GW), qrow),
                  pl.BlockSpec((tq, GW), lambda b, i: (blk0 + b * nq + i, v_col_blk)),
                  kp_spec, vp_spec, pl.BlockSpec((1, GW), lambda b, i: (0, 0))],
        out_specs=pl.BlockSpec((tq, GW), lambda b, i: (b * nq + i, 0)),
        out_shape=jax.ShapeDtypeStruct((nseq * t, GW), F32),
        scratch_shapes=[pltpu.VMEM((npairs, 2 * tq, 2 * HD), BF16), pltpu.VMEM((npairs, 2 * tq, 2 * HD), F32),
                        pltpu.VMEM((npairs, 2 * tq, tk), F32)],
        compiler_params=_cparams("parallel", "arbitrary"),
        name="sb_attn",
    )(qn, kn, z, kp_arr, vp_arr, out_norm)


def _sbs_kernel(q_ref, kd_ref, vd_ref, kt_ref, vt_ref, on_ref, o_ref, qh_ref, acc_ref, carry_ref,
                *, tq, tk, nblk, nchunks):
    c = pl.program_id(1)
    nh = GW // HD
    npairs = nh // 2
    hs = [slice(h * HD, (h + 1) * HD) for h in range(nh)]
    stack = functools.partial(jnp.concatenate, axis=0)
    mm = functools.partial(lax.dot_general, preferred_element_type=F32)

    @pl.when(c == 0)
    def _():
        row = _iota2(2 * tq, tq, 0)
        causal = _iota2(2 * tq, tq, 1) < jnp.where(row >= tq, row - tq, row)
        ud = jnp.where(_iota2(tq, tq, 0) > _iota2(tq, tq, 1), 1.0, 0.0).astype(BF16)
        qs = [_bf(q_ref[:, s]) for s in hs]
        kd = [_bf(kd_ref[:, s]) for s in hs]
        vd = [_bf(vd_ref[:, s]) for s in hs]
        for h in range(nh):
            qh_ref[h] = qs[h]
        qk = [lambda p=p: stack([mm(qs[2 * p], kd[2 * p], _NT), mm(qs[2 * p + 1], kd[2 * p + 1], _NT)])
              for p in range(npairs)]
        pv = [lambda w, p=p: stack([mm(w[:tq], vd[2 * p], _NN), mm(w[tq:], vd[2 * p + 1], _NN)])
              for p in range(npairs)]
        pvs, tots = _sb_stages(qk, pv, [None] * npairs, ud, causal, tk)
        for p in range(npairs):
            acc_ref[p] = pvs[p]
            carry_ref[p] = tots[p]

    up = jnp.where(_iota2(tk, tk, 0) > _iota2(tk, tk, 1), 1.0, 0.0).astype(BF16)
    for jb in reversed(range(nblk)):
        ks = slice(jb * tk, (jb + 1) * tk)
        qk = [lambda p=p: stack([mm(qh_ref[2 * p], _bf(kt_ref[2 * p, :, ks]), _NN),
                                 mm(qh_ref[2 * p + 1], _bf(kt_ref[2 * p + 1, :, ks]), _NN)])
              for p in range(npairs)]
        pv = [lambda w, p=p: stack([mm(w[:tq], _bf(vt_ref[2 * p, :, ks]), _NT),
                                    mm(w[tq:], _bf(vt_ref[2 * p + 1, :, ks]), _NT)])
              for p in range(npairs)]
        pvs, tots = _sb_stages(qk, pv, [carry_ref[p] for p in range(npairs)], up, None, tk)
        for p in range(npairs):
            acc_ref[p] += pvs[p]
            carry_ref[p] += tots[p]

    @pl.when(c == nchunks - 1)
    def _():
        heads = []
        for p in range(npairs):
            a = acc_ref[p]
            heads += [a[:tq], a[tq:]]
        o_ref[...] = _rms_rows(jnp.concatenate(heads, axis=1), on_ref[...])


def _sbs_call(qb, kb, vb, cache_kt, cache_vt, row0, nseq, tq, tk, tkc, out_norm, layer):
    past_len = cache_kt.shape[-1]
    nchunks = past_len // tkc
    blk0 = row0 // tq
    nh = GW // HD
    qrow = pl.BlockSpec((tq, GW), lambda b, c: (blk0 + b, 0))
    cspec = pl.BlockSpec((None, None, nh, HD, tkc), lambda b, c: (layer, b, 0, 0, nchunks - 1 - c))
    return pl.pallas_call(
        functools.partial(_sbs_kernel, tq=tq, tk=tk, nblk=tkc // tk, nchunks=nchunks),
        grid=(nseq, nchunks),
        in_specs=[qrow, qrow, qrow, cspec, cspec, pl.BlockSpec((1, GW), lambda b, c: (0, 0))],
        out_specs=pl.BlockSpec((tq, GW), lambda b, c: (b, 0)),
        out_shape=jax.ShapeDtypeStruct((nseq * tq, GW), F32),
        scratch_shapes=[pltpu.VMEM((nh, tq, HD), BF16), pltpu.VMEM((nh // 2, 2 * tq, HD), F32),
                        pltpu.VMEM((nh // 2, 2 * tq, tk), F32)],
        compiler_params=_cparams("parallel", "arbitrary"),
        name="sb_attn_cache",
    )(qb, kb, vb, cache_kt, cache_vt, out_norm)


def _rwkv_prep_kernel(zr_ref, zk_ref, zv_ref, zwa_ref, zg_ref, first_ref, mu_ref, w0_ref, w2_ref, a0_ref, a2_ref,
                      g2_ref, kk_ref, ka_ref, rk_ref, gmat_ref,
                      r_out, lw_out, k_out, v_out, kk_out, b_out, g_out, bonus_out, prev_ref, *, tm, wr):
    c = pl.program_id(1)
    cols = prev_ref.shape[1]

    @pl.when(c == 0)
    def _():
        prev_ref[...] = first_ref[0]

    first_row = _iota2(tm, 1, 0) == 0

    def shifted(ref, lo, width):
        x = ref[...]
        prev = jnp.where(first_row, prev_ref[:, lo:lo + width], pltpu.roll(x, 1, 0))
        return x + (prev - x) * mu_ref[:, lo:lo + width]

    r = shifted(zr_ref, 0, GW)
    k = shifted(zk_ref, GW, GW)
    v = shifted(zv_ref, 2 * GW, GW)
    wa = shifted(zwa_ref, 3 * GW, 2 * wr)
    gl = shifted(zg_ref, 3 * GW + 2 * wr, cols - 3 * GW - 2 * wr)
    for ref, lo in ((zr_ref, 0), (zk_ref, GW), (zv_ref, 2 * GW), (zwa_ref, 3 * GW), (zg_ref, 3 * GW + 2 * wr)):
        prev_ref[:, lo:lo + ref.shape[1]] = ref[tm - 1:tm, :]

    xw = w0_ref[...] + _dot(jnp.tanh(wa), w2_ref[...])
    w_log = -(jnp.maximum(-xw, 0.0) + jnp.log(1.0 + jnp.exp(-jnp.abs(xw)))) - 0.5
    a = jax.nn.sigmoid(a0_ref[...] + _dot(wa, a2_ref[...]))
    g_out[...] = _dot(jax.nn.sigmoid(gl), g2_ref[...])
    gmat = gmat_ref[...]
    kk = k * kk_ref[...]
    kk = kk / jnp.maximum(jnp.sqrt(_dot2_exact_rhs(kk * kk, gmat)), 1e-12)
    k = k * (1.0 + (a - 1.0) * ka_ref[...])
    r_out[...] = r
    lw_out[...] = -jnp.exp(w_log)
    k_out[...] = k
    v_out[...] = v
    kk_out[...] = kk
    b_out[...] = kk * a
    bonus_out[...] = _dot2_exact_rhs(r * k * rk_ref[...], gmat) * v


def _rwkv_prep_call(z, row0, nseq, t, tm, first, mu, w0, w2p, a0, a2p, g2, k_k, k_a, r_k, gmat, wr):
    nct = t // tm
    blk0 = row0 // tm
    cols = mu.shape[-1]
    zrow = lambda blkw, off: pl.BlockSpec((tm, blkw), lambda b, c: (blk0 + b * nct + c, off))
    c0 = (z.shape[1] - cols)
    const = lambda shp: pl.BlockSpec(shp, lambda b, c: (0,) * len(shp))
    orow = pl.BlockSpec((tm, GW), lambda b, c: (b * nct + c, 0))
    return pl.pallas_call(
        functools.partial(_rwkv_prep_kernel, tm=tm, wr=wr),
        grid=(nseq, nct),
        in_specs=[zrow(GW, c0 // GW), zrow(GW, c0 // GW + 1), zrow(GW, c0 // GW + 2),
                  zrow(2 * wr, (c0 + 3 * GW) // (2 * wr)), zrow(cols - 3 * GW - 2 * wr, (c0 + 3 * GW) // (2 * wr) + 1),
                  pl.BlockSpec((1, 1, cols), lambda b, c: (b, 0, 0)),
                  const((1, cols)), const((1, GW)), const((2 * wr, GW)), const((1, GW)), const((2 * wr, GW)),
                  const((cols - 3 * GW - 2 * wr, GW)), const((1, GW)), const((1, GW)), const((1, GW)),
                  const((GW, GW))],
        out_specs=[orow] * 8,
        out_shape=[jax.ShapeDtypeStruct((nseq * t, GW), F32)] * 8,
        scratch_shapes=[pltpu.VMEM((1, cols), F32)],
        compiler_params=_cparams("parallel", "arbitrary"),
        name="rwkv_prep",
    )(z, z, z, z, z, first.reshape(nseq, 1, cols), mu, w0, w2p, a0, a2p, g2, k_k, k_a, r_k, gmat)


def _hat(x, m0):
    return jnp.concatenate([jnp.where(m0, x, 0.0), jnp.where(m0, 0.0, x)], axis=0)


def _rwkv_chunk_kernel(r_ref, lw_ref, k_ref, v_ref, kk_ref, b_ref, rt_out, p3_out, m_out, n_out, *, group):
    C = RWKV_CHUNK
    C2 = 2 * C
    lane = _iota2(1, 2 * HD, 1)
    m0 = lane < HD
    ri = _iota2(C2, C2, 0)
    ci = _iota2(C2, C2, 1)
    same = (ri < C) == (ci < C)
    strict = jnp.logical_and(same, ri > ci)
    incl = jnp.logical_and(same, ri >= ci)
    ltri = jnp.where(_iota2(C, C, 0) >= _iota2(C, C, 1), 1.0, 0.0).astype(BF16)
    eye = _iota2(2 * HD, 2 * HD, 0) == _iota2(2 * HD, 2 * HD, 1)
    eye2 = ri == ci

    def pair_stages(p):
        sl = slice(p * 2 * HD, (p + 1) * 2 * HD)
        lw = lw_ref[:, sl]
        cl = _dot2_exact_lhs(ltri, lw)
        yield
        clast = cl[C - 1:C, :]
        kkt = _hat(kk_ref[:, sl] * jnp.exp(cl - lw), m0)
        rt = _hat(r_ref[:, sl] * jnp.exp(cl), m0)
        einv = jnp.exp(-cl)
        kb = _split(_hat(k_ref[:, sl] * einv, m0))
        bb = _split(_hat(b_ref[:, sl] * einv, m0))
        efin = jnp.exp(clast - cl)
        kh = _split(_hat(k_ref[:, sl] * efin, m0))
        bh = _split(_hat(b_ref[:, sl] * efin, m0))
        vh = _split(_hat(v_ref[:, sl], m0))
        lhs = _split(jnp.concatenate([kkt, rt], axis=0))
        gk = _dot3s(lhs, kb, _NT)
        gb = _dot3s(lhs, bb, _NT)
        yield
        a_kk = jnp.where(strict, gk[:C2], 0.0)
        a_rk = _split(jnp.where(incl, gk[C2:], 0.0))
        a_kb = jnp.where(strict, gb[:C2], 0.0)
        a_rb = _split(jnp.where(incl, gb[C2:], 0.0))
        p1 = _dot3s(_split(a_kk), vh)
        tinv = jnp.where(eye2, 1.0, 0.0) - a_kb
        lp = _split(a_kb)
        n = 2
        while n < C:
            yield
            lpf = _dot3s(lp, lp)
            lp = _split(lpf)
            tinv = tinv + _dot3s(_split(tinv), lp)
            n *= 2
        yield
        x = _dot3s(_split(tinv), _split(jnp.concatenate([kkt, p1], axis=1)))
        yield
        xs = _split(x)
        kt = (xs[0][:, :2 * HD], xs[1][:, :2 * HD])
        p2 = (xs[0][:, 2 * HD:], xs[1][:, 2 * HD:])
        y2 = _dot3s(a_rb, xs)
        p3 = _dot3s(a_rk, vh)
        mm = _dot3s(kt, bh, _TN)
        nn = _dot3s(vh, kh, _TN) - _dot3s(p2, bh, _TN)
        yield
        rt_out[p] = rt - y2[:, :2 * HD]
        p3_out[p] = p3 - y2[:, 2 * HD:]
        m_out[p] = jnp.where(eye, jnp.exp(clast), 0.0) - mm
        n_out[p] = nn

    npairs = GW // (2 * HD)
    for p0 in range(0, npairs, group):
        gens = [pair_stages(p) for p in range(p0, p0 + group)]
        while gens:
            for g in list(gens):
                try:
                    next(g)
                except StopIteration:
                    gens.remove(g)


def _rwkv_chunk_call(r, lw, k, v, kk, b):
    n = r.shape[0]
    C = RWKV_CHUNK
    nc = n // C
    npairs = GW // (2 * HD)
    row = pl.BlockSpec((C, GW), lambda i: (i, 0))
    blk = lambda rows: pl.BlockSpec((None, npairs, rows, 2 * HD), lambda i: (i, 0, 0, 0))
    shp = lambda rows: jax.ShapeDtypeStruct((nc, npairs, rows, 2 * HD), F32)
    return pl.pallas_call(
        functools.partial(_rwkv_chunk_kernel, group=4),
        grid=(nc,),
        in_specs=[row] * 6,
        out_specs=[blk(2 * C), blk(2 * C), blk(2 * HD), blk(2 * HD)],
        out_shape=[shp(2 * C), shp(2 * C), shp(2 * HD), shp(2 * HD)],
        compiler_params=_cparams("parallel"),
        name="rwkv_chunk",
    )(r, lw, k, v, kk, b)


def _rwkv_seq_kernel(rt_ref, p3_ref, m_ref, n_ref, s0_ref, g_ref, bonus_ref, lnw_ref, lnb_ref, gmat_ref,
                     o_ref, s_out_ref, s_ref, *, nct, gsz):
    c = pl.program_id(1)
    C = RWKV_CHUNK

    @pl.when(c == 0)
    def _():
        s_ref[...] = s0_ref[...]

    gm = gmat_ref[0:2 * HD, 0:2 * HD]
    npairs = GW // (2 * HD)
    pairs = [slice(p * 2 * HD, (p + 1) * 2 * HD) for p in range(npairs)]
    sp = [(s, p) for s in range(gsz) for p in range(npairs)]
    ss = [_split(s_ref[s, p]) for s, p in sp]
    yhs = [_dot3s(_split(rt_ref[s, p]), ss[i], _NT) + p3_ref[s, p] for i, (s, p) in enumerate(sp)]
    snew = [_dot3s(ss[i], _split(m_ref[s, p])) + n_ref[s, p] for i, (s, p) in enumerate(sp)]
    ys = [yh[:C] + yh[C:] for yh in yhs]
    means = [_dot2_exact_rhs(y, gm) * (1.0 / HD) for y in ys]
    ds = [y - m for y, m in zip(ys, means)]
    vrs = [_dot2_exact_rhs(d * d, gm) * (1.0 / HD) for d in ds]
    for i, (s, p) in enumerate(sp):
        sl = pairs[p]
        yn = ds[i] * lax.rsqrt(vrs[i] + GN_EPS) * lnw_ref[:, sl] + lnb_ref[:, sl]
        o_ref[s, :, sl] = (yn + bonus_ref[s, :, sl]) * g_ref[s, :, sl]
        s_ref[s, p] = snew[i]

    @pl.when(c == nct - 1)
    def _():
        s_out_ref[...] = s_ref[...]


def _rwkv_seq_call(rt, p3, m, n, nseq, nct, gsz, s0, g, bonus, lnx_w, lnx_b, gmat):
    C = RWKV_CHUNK
    npairs = GW // (2 * HD)
    c5 = lambda a: a.reshape(nseq, nct, *a.shape[1:])
    cblk = lambda rows: pl.BlockSpec((gsz, None, npairs, rows, 2 * HD), lambda b, c: (b, c, 0, 0, 0))
    sblk = pl.BlockSpec((gsz, npairs, 2 * HD, 2 * HD), lambda b, c: (b, 0, 0, 0))
    rowblk = pl.BlockSpec((gsz, C, GW), lambda b, c: (b, c, 0))
    const = lambda shp: pl.BlockSpec(shp, lambda b, c: (0,) * len(shp))
    od, s_out = pl.pallas_call(
        functools.partial(_rwkv_seq_kernel, nct=nct, gsz=gsz),
        grid=(nseq // gsz, nct),
        in_specs=[cblk(2 * C), cblk(2 * C), cblk(2 * HD), cblk(2 * HD), sblk, rowblk, rowblk,
                  const((1, GW)), const((1, GW)), const((GW, GW))],
        out_specs=[rowblk, sblk],
        out_shape=[jax.ShapeDtypeStruct((nseq, nct * C, GW), F32),
                   jax.ShapeDtypeStruct((nseq, npairs, 2 * HD, 2 * HD), F32)],
        scratch_shapes=[pltpu.VMEM((gsz, npairs, 2 * HD, 2 * HD), F32)],
        compiler_params=_cparams("parallel", "arbitrary"),
        name="rwkv_seq",
    )(c5(rt), c5(p3), c5(m), c5(n), s0, g.reshape(nseq, nct * C, GW), bonus.reshape(nseq, nct * C, GW),
      lnx_w, lnx_b, gmat)
    return od.reshape(nseq * nct * C, GW), s_out


def _pair_states(s):
    b, h = s.shape[:2]
    s = s.reshape(b, h // 2, 2, HD, HD)
    z = jnp.zeros_like(s[:, :, 0])
    top = jnp.concatenate([s[:, :, 0], z], axis=-1)
    bot = jnp.concatenate([z, s[:, :, 1]], axis=-1)
    return jnp.concatenate([top, bot], axis=-2)


def _unpair_states(s2):
    b, hp = s2.shape[:2]
    return jnp.stack([s2[:, :, :HD, :HD], s2[:, :, HD:, HD:]], axis=2).reshape(b, 2 * hp, HD, HD)


def _forward(xp, xs, cache_k, cache_v, s5_re0, s5_im0, wkv0, shift0, W):
    Bp, Tp, D = xp.shape
    Bs, Ts, _ = xs.shape
    depth = W['ln_ffn1'].shape[0]
    past_len = cache_k.shape[2]
    Np, Ns = Bp * Tp, Bs * Ts
    n_heads_d = GW // HD
    wr = W['rwkv_w2'].shape[1]
    dcols = W['rwkv_mu'].shape[-1]

    x = jnp.concatenate([xp.reshape(Np, D), xs.reshape(Ns, D)], axis=0)
    cache_kt = jnp.transpose(cache_k, (0, 1, 3, 4, 2))
    cache_vt = jnp.transpose(cache_v, (0, 1, 3, 4, 2))
    wbf = {k: _bf(W[k]) for k in ('w_ffn1_gate', 'w_ffn1_up', 'w_ffn1_down', 'w_in', 'w_out',
                                  'w_ffn2_gate', 'w_ffn2_up', 'w_ffn2_down')}
    hid = lax.broadcasted_iota(jnp.int32, (GW, GW), 0) // HD
    gmat = (hid == jnp.transpose(hid)).astype(BF16)
    grp = lax.broadcasted_iota(jnp.int32, (GW, S5_STATE), 0) // (GW // 32) == \
        lax.broadcasted_iota(jnp.int32, (GW, S5_STATE), 1) // (S5_STATE // 32)
    zeros_first = jnp.zeros((Bp, dcols), F32)
    zeros_s5 = jnp.zeros((Bp, S5_STATE), F32)
    zeros_wkv = jnp.zeros((Bp, n_heads_d // 2, 2 * HD, 2 * HD), F32)

    h = _rms_call(x, W['ln_ffn1'][0][None])
    outs = {k: [] for k in ('kp', 'vp', 'ks', 'vs', 's5rp', 's5ip', 's5rs', 's5is', 'wkvp', 'wkvs', 'shp', 'shs', 'gv')}
    for l in range(depth):
        g1 = lambda name: W[name][l][None]
        x, h = _ffn_call(x, h, wbf['w_ffn1_gate'], wbf['w_ffn1_up'], wbf['w_ffn1_down'], g1('ln_mix'), l)
        z = _inproj_call(h, wbf['w_in'], l)

        oa_p, _ = _gmlp_call(z, 0, Np, min(Tp, GMLP_CHUNK), g1('gmlp_v_norm'), W['gmlp_ws'][l], W['gmlp_b'][l],
                             g1('out_norm_a'))
        oa_s, gv_s = _gmlp_call(z, Np, Ns, min(Ts, GMLP_CHUNK), g1('gmlp_v_norm'), W['gmlp_ws'][l], W['gmlp_b'][l],
                                g1('out_norm_a'))

        pw_re, pw_im, bb_re, bb_im = _s5_params(W['s5_lam_re'][l], W['s5_lam_im'][l], W['s5_log_dt'][l],
                                                W['s5_b_re'][l], W['s5_b_im'][l])
        bblk = jnp.concatenate([jnp.where(grp, jnp.tile(bb_re, (32, 1)), 0.0),
                                jnp.where(grp, jnp.tile(bb_im, (32, 1)), 0.0)], axis=1).astype(BF16)
        grp_t = jnp.transpose(grp)
        cre = jnp.where(grp_t, jnp.tile(jnp.transpose(W['s5_c_re'][l], (0, 2, 1)).reshape(S5_STATE, -1), (1, 32)),
                        0.0).astype(BF16)
        cim = jnp.where(grp_t, jnp.tile(jnp.transpose(W['s5_c_im'][l], (0, 2, 1)).reshape(S5_STATE, -1), (1, 32)),
                        0.0).astype(BF16)
        s5_args = (pw_re, pw_im, bblk, cre, cim, g1('s5_d'), W['s5_w_glu'][l], g1('s5_b_glu'), g1('out_norm_b'))
        ob_p, s5rp, s5ip = _s5_call(z, 0, Bp, Tp, zeros_s5, zeros_s5, *s5_args, col_blk=2, tc=min(Tp, 256))
        ob_s, s5rs, s5is = _s5_call(z, Np, Bs, Ts, s5_re0[l].reshape(Bs, S5_STATE), s5_im0[l].reshape(Bs, S5_STATE),
                                    *s5_args, col_blk=2, tc=min(Ts, 256))

        qg = jnp.tile(W['sb_q_norm'][l], GW // HD)[None]
        kg = jnp.tile(W['sb_k_norm'][l], GW // HD)[None]
        qb, kn, kb, vb = _sbprep_call(z, gmat, qg, kg)
        oc_p = _sb_call(qb, kb, vb, None, None, 0, Bp, Tp, min(Tp, SB_TQ), min(Tp, SB_TK), 0, 0,
                        g1('out_norm_c'))
        oc_s = _sbs_call(qb, kb, vb, cache_kt, cache_vt, Np, Bs, Ts, min(past_len, SB_TK_SAMPLE),
                         min(past_len, SB_CACHE_CHUNK), g1('out_norm_c'), l)
        vcol = z[:, 5 * GW:6 * GW]

        w2p = jnp.concatenate([W['rwkv_w2'][l], jnp.zeros_like(W['rwkv_a2'][l])], axis=0)
        a2p = jnp.concatenate([jnp.zeros_like(W['rwkv_w2'][l]), W['rwkv_a2'][l]], axis=0)
        rk = W['rwkv_r_k'][l].reshape(1, GW)
        prep_args = (g1('rwkv_mu'), g1('rwkv_w0'), w2p, g1('rwkv_a0'), a2p, W['rwkv_g2'][l], g1('rwkv_k_k'),
                     g1('rwkv_k_a'), rk, gmat, wr)
        pp = _rwkv_prep_call(z, 0, Bp, Tp, min(Tp, 512), zeros_first, *prep_args)
        ps = _rwkv_prep_call(z, Np, Bs, Ts, Ts, shift0[l], *prep_args)
        seq_args = (g1('rwkv_lnx_w'), g1('rwkv_lnx_b'), gmat)
        od_p, wkv_p = _rwkv_seq_call(*_rwkv_chunk_call(*pp[:6]), Bp, Tp // RWKV_CHUNK, min(Bp, 2), zeros_wkv,
                                     pp[6], pp[7], *seq_args)
        od_s, wkv_s = _rwkv_seq_call(*_rwkv_chunk_call(*ps[:6]), Bs, Ts // RWKV_CHUNK, min(Bs, 4),
                                     _pair_states(wkv0[l]), ps[6], ps[7], *seq_args)

        gain_next = g1('ln_ffn2')
        x, h = _outproj_call(x, (oa_p, ob_p, oc_p, od_p), (oa_s, ob_s, oc_s, od_s), wbf['w_out'], gain_next, l)
        gain_next = W['ln_ffn1'][l + 1][None] if l + 1 < depth else g1('ln_ffn2')
        x, h = _ffn_call(x, h, wbf['w_ffn2_gate'], wbf['w_ffn2_up'], wbf['w_ffn2_down'], gain_next, l)

        nh = GW // HD
        outs['kp'].append(kn[:Np].reshape(Bp, Tp, nh, HD))
        outs['vp'].append(vcol[:Np].reshape(Bp, Tp, nh, HD))
        outs['ks'].append(kn[Np:].reshape(Bs, Ts, nh, HD))
        outs['vs'].append(vcol[Np:].reshape(Bs, Ts, nh, HD))
        outs['s5rp'].append(s5rp.reshape(Bp, 32, -1))
        outs['s5ip'].append(s5ip.reshape(Bp, 32, -1))
        outs['s5rs'].append(s5rs.reshape(Bs, 32, -1))
        outs['s5is'].append(s5is.reshape(Bs, 32, -1))
        outs['wkvp'].append(_unpair_states(wkv_p))
        outs['wkvs'].append(_unpair_states(wkv_s))
        zd = z[:, z.shape[1] - dcols:]
        outs['shp'].append(zd[:Np].reshape(Bp, Tp, dcols)[:, -1])
        outs['shs'].append(zd[Np:].reshape(Bs, Ts, dcols)[:, -1])
        outs['gv'].append(gv_s.reshape(Bs, Ts, GW))

    st = lambda k: jnp.stack(outs[k], axis=0)
    return (x[:Np].reshape(Bp, Tp, D), x[Np:].reshape(Bs, Ts, D),
            st('kp'), st('vp'), st('ks'), st('vs'),
            st('s5rp'), st('s5ip'), st('s5rs'), st('s5is'),
            st('wkvp'), st('wkvs'), st('shp'), st('shs'), st('gv'))


_WEIGHT_NAMES = ('ln_ffn1', 'w_ffn1_gate', 'w_ffn1_up', 'w_ffn1_down', 'ln_mix', 'w_in',
                 'gmlp_v_norm', 'gmlp_ws', 'gmlp_b', 'out_norm_a',
                 's5_lam_re', 's5_lam_im', 's5_log_dt', 's5_b_re', 's5_b_im', 's5_c_re', 's5_c_im', 's5_d',
                 's5_w_glu', 's5_b_glu', 'out_norm_b',
                 'sb_q_norm', 'sb_k_norm', 'out_norm_c',
                 'rwkv_mu', 'rwkv_w0', 'rwkv_w2', 'rwkv_a0', 'rwkv_a2', 'rwkv_g2', 'rwkv_k_k', 'rwkv_k_a', 'rwkv_r_k',
                 'rwkv_lnx_w', 'rwkv_lnx_b',
                 'w_out', 'ln_ffn2', 'w_ffn2_gate', 'w_ffn2_up', 'w_ffn2_down')


def kernel(x_prompt, x_sample, cache_sb_k, cache_sb_v, state_s5_re, state_s5_im, state_rwkv_wkv, state_rwkv_shift,
           ln_ffn1, w_ffn1_gate, w_ffn1_up, w_ffn1_down, ln_mix, w_in,
           gmlp_v_norm, gmlp_ws, gmlp_b, out_norm_a,
           s5_lam_re, s5_lam_im, s5_log_dt, s5_b_re, s5_b_im, s5_c_re, s5_c_im, s5_d, s5_w_glu, s5_b_glu, out_norm_b,
           sb_q_norm, sb_k_norm, out_norm_c,
           rwkv_mu, rwkv_w0, rwkv_w2, rwkv_a0, rwkv_a2, rwkv_g2, rwkv_k_k, rwkv_k_a, rwkv_r_k, rwkv_lnx_w, rwkv_lnx_b,
           w_out, ln_ffn2, w_ffn2_gate, w_ffn2_up, w_ffn2_down):
    weights = (ln_ffn1, w_ffn1_gate, w_ffn1_up, w_ffn1_down, ln_mix, w_in,
               gmlp_v_norm, gmlp_ws, gmlp_b, out_norm_a,
               s5_lam_re, s5_lam_im, s5_log_dt, s5_b_re, s5_b_im, s5_c_re, s5_c_im, s5_d, s5_w_glu, s5_b_glu,
               out_norm_b, sb_q_norm, sb_k_norm, out_norm_c,
               rwkv_mu, rwkv_w0, rwkv_w2, rwkv_a0, rwkv_a2, rwkv_g2, rwkv_k_k, rwkv_k_a, rwkv_r_k, rwkv_lnx_w,
               rwkv_lnx_b, w_out, ln_ffn2, w_ffn2_gate, w_ffn2_up, w_ffn2_down)
    W = dict(zip(_WEIGHT_NAMES, weights))
    return _forward(x_prompt, x_sample, cache_sb_k, cache_sb_v, state_s5_re, state_s5_im, state_rwkv_wkv,
                    state_rwkv_shift, W)
```

```python
import functools

import jax
import jax.numpy as jnp
from jax import lax
from jax.experimental import pallas as pl
from jax.experimental.pallas import tpu as pltpu

F32 = jnp.float32
BF16 = jnp.bfloat16

RMS_EPS = 1e-6
GN_EPS = 64e-5
A_HEADS = 4
GMLP_CHUNK = 128
HD = 64
GW = 512
S5_STATE = 2048
RWKV_CHUNK = 64
SB_TQ, SB_TK = 256, 256
SB_TK_SAMPLE = 256
SB_CACHE_CHUNK = 1024
VMEM_LIMIT = 56 * 1024 * 1024


def _cparams(*sem):
    return pltpu.CompilerParams(dimension_semantics=sem, vmem_limit_bytes=VMEM_LIMIT)


def _bf(x):
    return x.astype(BF16)


_NN = (((1,), (0,)), ((), ()))
_NT = (((1,), (1,)), ((), ()))
_TN = (((0,), (0,)), ((), ()))


def _dot(a, b, dims=_NN):
    return lax.dot_general(_bf(a), _bf(b), dims, preferred_element_type=F32)


def _split(x):
    hi = _bf(x)
    lo = _bf(x - hi.astype(F32))
    return hi, lo


def _dot3(a, b, dims=_NN):
    ah, al = _split(a)
    bh, bl = _split(b)
    d = functools.partial(lax.dot_general, dimension_numbers=dims, preferred_element_type=F32)
    return d(ah, bh) + (d(ah, bl) + d(al, bh))


def _dot3s(a, b, dims=_NN):
    d = functools.partial(lax.dot_general, dimension_numbers=dims, preferred_element_type=F32)
    return d(a[0], b[0]) + (d(a[0], b[1]) + d(a[1], b[0]))


def _dot2_exact_rhs(a, b_bf, dims=_NN):
    ah, al = _split(a)
    d = functools.partial(lax.dot_general, dimension_numbers=dims, preferred_element_type=F32)
    return d(ah, b_bf) + d(al, b_bf)


def _dot2_exact_lhs(a_bf, b, dims=_NN):
    bh, bl = _split(b)
    d = functools.partial(lax.dot_general, dimension_numbers=dims, preferred_element_type=F32)
    return d(a_bf, bh) + d(a_bf, bl)


def _rms_rows(x, gain):
    ms = jnp.mean(x * x, axis=-1, keepdims=True)
    return x * lax.rsqrt(ms + RMS_EPS) * gain


def _iota2(n, m, axis):
    return lax.broadcasted_iota(jnp.int32, (n, m), axis)


def _rms_kernel(x_ref, g_ref, h_ref):
    h_ref[...] = _bf(_rms_rows(x_ref[...], g_ref[...]))


def _rms_call(x, gain, tm=512):
    n, d = x.shape
    return pl.pallas_call(
        _rms_kernel,
        grid=(n // tm,),
        in_specs=[pl.BlockSpec((tm, d), lambda i: (i, 0)), pl.BlockSpec((1, d), lambda i: (0, 0))],
        out_specs=pl.BlockSpec((tm, d), lambda i: (i, 0)),
        out_shape=jax.ShapeDtypeStruct((n, d), BF16),
        compiler_params=_cparams("parallel"),
        name="rms",
    )(x, gain)


def _ffn_kernel(x_ref, h_ref, wg_ref, wu_ref, wd_ref, gn_ref, o_ref, hn_ref, *, nj):
    j = pl.program_id(1)

    @pl.when(j == 0)
    def _():
        o_ref[...] = jnp.zeros_like(o_ref)

    h = h_ref[...]
    g = jnp.dot(h, wg_ref[...], preferred_element_type=F32)
    u = jnp.dot(h, wu_ref[...], preferred_element_type=F32)
    a = _bf(g * jax.nn.sigmoid(g) * u)
    o_ref[...] += jnp.dot(a, wd_ref[...], preferred_element_type=F32)

    @pl.when(j == nj - 1)
    def _():
        y = x_ref[...] + 0.5 * o_ref[...]
        o_ref[...] = y
        hn_ref[...] = _bf(_rms_rows(y, gn_ref[...]))


def _ffn_call(x, h, wg, wu, wd, gain_next, layer, tm=512, tf=512):
    n, d = x.shape
    ff = wg.shape[-1]
    nj = ff // tf
    return pl.pallas_call(
        functools.partial(_ffn_kernel, nj=nj),
        grid=(n // tm, nj),
        in_specs=[
            pl.BlockSpec((tm, d), lambda i, j: (i, 0)),
            pl.BlockSpec((tm, d), lambda i, j: (i, 0)),
            pl.BlockSpec((None, d, tf), lambda i, j: (layer, 0, j)),
            pl.BlockSpec((None, d, tf), lambda i, j: (layer, 0, j)),
            pl.BlockSpec((None, tf, d), lambda i, j: (layer, j, 0)),
            pl.BlockSpec((1, d), lambda i, j: (0, 0)),
        ],
        out_specs=[pl.BlockSpec((tm, d), lambda i, j: (i, 0)), pl.BlockSpec((tm, d), lambda i, j: (i, 0))],
        out_shape=[jax.ShapeDtypeStruct((n, d), F32), jax.ShapeDtypeStruct((n, d), BF16)],
        compiler_params=_cparams("parallel", "arbitrary"),
        name="ffn",
    )(x, h, wg, wu, wd, gain_next)


def _inproj_kernel(h_ref, w_ref, z_ref):
    z_ref[...] = jnp.dot(h_ref[...], w_ref[...], preferred_element_type=F32)


def _inproj_call(h, w_in, layer, tm=512):
    n, d = h.shape
    cols = w_in.shape[-1]
    tn = cols // 2
    return pl.pallas_call(
        _inproj_kernel,
        grid=(2, n // tm),
        in_specs=[pl.BlockSpec((tm, d), lambda j, i: (i, 0)),
                  pl.BlockSpec((None, d, tn), lambda j, i: (layer, 0, j))],
        out_specs=pl.BlockSpec((tm, tn), lambda j, i: (i, j)),
        out_shape=jax.ShapeDtypeStruct((n, cols), F32),
        compiler_params=_cparams("parallel", "parallel"),
        name="inproj",
    )(h, w_in)


def _outproj_kernel(x_ref, *refs, n_first):
    first, second = refs[0:4], refs[4:8]
    w_ref, gn_ref, o_ref, hn_ref = refs[8:]

    def run(mix_refs):
        acc = x_ref[...]
        for i, r in enumerate(mix_refs):
            acc = acc + jnp.dot(_bf(r[...]), w_ref[i * GW:(i + 1) * GW, :], preferred_element_type=F32)
        o_ref[...] = acc
        hn_ref[...] = _bf(_rms_rows(acc, gn_ref[...]))

    i = pl.program_id(0)
    pl.when(i < n_first)(lambda: run(first))
    pl.when(i >= n_first)(lambda: run(second))


def _outproj_call(x, mix_first, mix_second, w_out, gain_next, layer, tm=256):
    n, d = x.shape
    n_first = mix_first[0].shape[0] // tm
    n_second = mix_second[0].shape[0] // tm
    row = lambda i: (i, 0)
    first_row = lambda i: (jnp.minimum(i, n_first - 1), 0)
    second_row = lambda i: (jnp.maximum(i - n_first, 0), 0)
    return pl.pallas_call(
        functools.partial(_outproj_kernel, n_first=n_first),
        grid=(n_first + n_second,),
        in_specs=[pl.BlockSpec((tm, d), row)] + [pl.BlockSpec((tm, GW), first_row)] * 4
        + [pl.BlockSpec((tm, GW), second_row)] * 4
        + [pl.BlockSpec((None, d, d), lambda i: (layer, 0, 0)), pl.BlockSpec((1, d), lambda i: (0, 0))],
        out_specs=[pl.BlockSpec((tm, d), row), pl.BlockSpec((tm, d), row)],
        out_shape=[jax.ShapeDtypeStruct((n, d), F32), jax.ShapeDtypeStruct((n, d), BF16)],
        compiler_params=_cparams("parallel"),
        name="outproj",
    )(x, *mix_first, *mix_second, w_out, gain_next)


def _gmlp_kernel(z_ref, vn_ref, ws_ref, bt_ref, on_ref, o_ref, v_ref, *, L):
    z = jax.nn.gelu(z_ref[...])
    causal = _iota2(L, L, 1) <= _iota2(L, L, 0)
    hw = GW // A_HEADS
    ss = jnp.zeros((L, 1), F32)
    outs = []
    for h in range(A_HEADS):
        vh = _rms_rows(z[:, GW + h * hw:GW + (h + 1) * hw], vn_ref[...])
        v_ref[:, h * hw:(h + 1) * hw] = vh
        w = jnp.where(causal, ws_ref[h], 0.0)
        s = _dot(w, vh) + bt_ref[:, h:h + 1]
        oh = z[:, h * hw:(h + 1) * hw] * s
        ss = ss + jnp.sum(oh * oh, axis=-1, keepdims=True)
        outs.append(oh)
    scale = lax.rsqrt(ss * (1.0 / GW) + RMS_EPS)
    for h in range(A_HEADS):
        o_ref[:, h * hw:(h + 1) * hw] = outs[h] * scale * on_ref[:, h * hw:(h + 1) * hw]


def _gmlp_call(z, row0, nrows, L, v_norm, ws, b, out_norm):
    nb = nrows // L
    b0 = row0 // L
    ws_l = ws[:, :L, :L]
    bt = jnp.transpose(b[:, :L])
    row = lambda i: (i, 0)
    return pl.pallas_call(
        functools.partial(_gmlp_kernel, L=L),
        grid=(nb,),
        in_specs=[pl.BlockSpec((L, 2 * GW), lambda i: (b0 + i, 0)),
                  pl.BlockSpec((1, GW // A_HEADS), lambda i: (0, 0)),
                  pl.BlockSpec((A_HEADS, L, L), lambda i: (0, 0, 0)),
                  pl.BlockSpec((L, A_HEADS), lambda i: (0, 0)),
                  pl.BlockSpec((1, GW), lambda i: (0, 0))],
        out_specs=[pl.BlockSpec((L, GW), row), pl.BlockSpec((L, GW), row)],
        out_shape=[jax.ShapeDtypeStruct((nrows, GW), F32), jax.ShapeDtypeStruct((nrows, GW), F32)],
        compiler_params=_cparams("parallel"),
        name="gmlp",
    )(z, v_norm, ws_l, bt, out_norm)


def _s5_param_kernel(lr_ref, li_ref, ldt_ref, brt_ref, bit_ref, pw_re_ref, pw_im_ref, bb_re_ref, bb_im_ref):
    lr = lr_ref[...]
    li = li_ref[...]
    dt = jnp.exp(ldt_ref[...])
    n = (_iota2(8, S5_STATE, 0) + 1).astype(F32)
    mag = jnp.exp(n * (lr * dt))
    ang = n * (li * dt)
    pw_re = mag * jnp.cos(ang)
    pw_im = mag * jnp.sin(ang)
    pw_re_ref[...] = pw_re
    pw_im_ref[...] = pw_im
    ab_re = pw_re[0:1, :]
    ab_im = pw_im[0:1, :]
    den = lr * lr + li * li
    nr, ni = ab_re - 1.0, ab_im
    cf_re = (nr * lr + ni * li) / den
    cf_im = (ni * lr - nr * li) / den
    br = brt_ref[...]
    bi = bit_ref[...]
    bb_re_ref[...] = cf_re * br - cf_im * bi
    bb_im_ref[...] = cf_re * bi + cf_im * br


def _s5_params(lam_re, lam_im, log_dt, b_re, b_im):
    g, p = lam_re.shape
    s = g * p
    ch = b_re.shape[-1]
    ldt = jnp.broadcast_to(log_dt[:, None], (g, p)).reshape(1, s)
    brt = jnp.transpose(b_re.reshape(s, ch))
    bit = jnp.transpose(b_im.reshape(s, ch))
    return pl.pallas_call(
        _s5_param_kernel,
        out_shape=[jax.ShapeDtypeStruct((8, s), F32), jax.ShapeDtypeStruct((8, s), F32),
                   jax.ShapeDtypeStruct((ch, s), F32), jax.ShapeDtypeStruct((ch, s), F32)],
        name="s5_params",
    )(lam_re.reshape(1, s), lam_im.reshape(1, s), ldt, brt, bit)


def _cmul(ar, ai, xr, xi):
    return ar * xr - ai * xi, ar * xi + ai * xr


def _s5_kernel(u_ref, h0r_ref, h0i_ref, pwr_ref, pwi_ref, bblk_ref, cre_ref, cim_ref, d_ref, wglu_ref, bglu_ref,
               on_ref, o_ref, hr_out_ref, hi_out_ref, bu_ref, hre_ref, him_ref, cr_ref, ci_ref, *, tc, nct):
    c = pl.program_id(1)
    S = S5_STATE
    LW = 512

    @pl.when(c == 0)
    def _():
        cr_ref[...] = h0r_ref[0]
        ci_ref[...] = h0i_ref[0]

    u = u_ref[...]
    NK = GW // 128
    SW = S // NK
    for kc in range(NK):
        ukc = _bf(u[:, kc * 128:(kc + 1) * 128])
        for half in range(2):
            cs = slice(half * S + kc * SW, half * S + (kc + 1) * SW)
            bu_ref[:, cs] = jnp.dot(ukc, bblk_ref[kc * 128:(kc + 1) * 128, cs], preferred_element_type=F32)

    rows = _iota2(8, LW, 0)
    for lc in range(S // LW):
        sl = slice(lc * LW, (lc + 1) * LW)
        pr = pwr_ref[:, sl]
        pi = pwi_ref[:, sl]
        a1r = jnp.where(rows >= 1, pr[0:1, :], 0.0)
        a1i = jnp.where(rows >= 1, pi[0:1, :], 0.0)
        a2r = jnp.where(rows >= 2, pr[1:2, :], 0.0)
        a2i = jnp.where(rows >= 2, pi[1:2, :], 0.0)
        a4r = jnp.where(rows >= 4, pr[3:4, :], 0.0)
        a4i = jnp.where(rows >= 4, pi[3:4, :], 0.0)

        def tile(i, carry):
            kr, ki = carry
            r0 = pl.multiple_of(i * 8, 8)
            xr = bu_ref[pl.ds(r0, 8), lc * LW:(lc + 1) * LW]
            xi = bu_ref[pl.ds(r0, 8), S + lc * LW:S + (lc + 1) * LW]
            for (ar, ai, sh) in ((a1r, a1i, 1), (a2r, a2i, 2), (a4r, a4i, 4)):
                sr, si = _cmul(ar, ai, pltpu.roll(xr, sh, 0), pltpu.roll(xi, sh, 0))
                xr, xi = xr + sr, xi + si
            sr, si = _cmul(pr, pi, kr, ki)
            xr, xi = xr + sr, xi + si
            hre_ref[pl.ds(r0, 8), sl] = xr
            him_ref[pl.ds(r0, 8), sl] = xi
            return xr[7:8, :], xi[7:8, :]

        kr, ki = lax.fori_loop(0, tc // 8, tile, (cr_ref[:, sl], ci_ref[:, sl]))
        cr_ref[:, sl] = kr
        ci_ref[:, sl] = ki

    ys = []
    for kc in range(NK):
        ss, os_ = slice(kc * SW, (kc + 1) * SW), slice(kc * 128, (kc + 1) * 128)
        ys.append(jnp.dot(_bf(hre_ref[:, ss]), cre_ref[ss, os_], preferred_element_type=F32)
                  - jnp.dot(_bf(him_ref[:, ss]), cim_ref[ss, os_], preferred_element_type=F32))
    y = jnp.concatenate(ys, axis=1) + d_ref[...] * u
    g = jax.nn.gelu(y)
    out = g * jax.nn.sigmoid(_dot(g, wglu_ref[...]) + bglu_ref[...])
    o_ref[...] = _rms_rows(out, on_ref[...])

    @pl.when(c == nct - 1)
    def _():
        hr_out_ref[0] = cr_ref[...]
        hi_out_ref[0] = ci_ref[...]


def _s5_call(z, row0, nseq, t, h0_re, h0_im, pw_re, pw_im, bblk, c_re, c_im, d_skip, w_glu, b_glu, out_norm,
             col_blk, tc):
    nct = t // tc
    S = S5_STATE
    blk0 = row0 // tc
    const2 = lambda b, c: (0, 0)
    outs = pl.pallas_call(
        functools.partial(_s5_kernel, tc=tc, nct=nct),
        grid=(nseq, nct),
        in_specs=[pl.BlockSpec((tc, GW), lambda b, c: (blk0 + b * nct + c, col_blk)),
                  pl.BlockSpec((1, 1, S), lambda b, c: (b, 0, 0)),
                  pl.BlockSpec((1, 1, S), lambda b, c: (b, 0, 0)),
                  pl.BlockSpec((8, S), const2), pl.BlockSpec((8, S), const2),
                  pl.BlockSpec((GW, 2 * S), const2),
                  pl.BlockSpec((S, GW), const2), pl.BlockSpec((S, GW), const2),
                  pl.BlockSpec((1, GW), const2),
                  pl.BlockSpec((GW, GW), const2), pl.BlockSpec((1, GW), const2), pl.BlockSpec((1, GW), const2)],
        out_specs=[pl.BlockSpec((tc, GW), lambda b, c: (b * nct + c, 0)),
                   pl.BlockSpec((1, 1, S), lambda b, c: (b, 0, 0)),
                   pl.BlockSpec((1, 1, S), lambda b, c: (b, 0, 0))],
        out_shape=[jax.ShapeDtypeStruct((nseq * t, GW), F32),
                   jax.ShapeDtypeStruct((nseq, 1, S), F32), jax.ShapeDtypeStruct((nseq, 1, S), F32)],
        scratch_shapes=[pltpu.VMEM((tc, 2 * S), F32), pltpu.VMEM((tc, S), F32), pltpu.VMEM((tc, S), F32),
                        pltpu.VMEM((1, S), F32), pltpu.VMEM((1, S), F32)],
        compiler_params=_cparams("parallel", "arbitrary"),
        name="s5",
    )(z, h0_re.reshape(nseq, 1, S), h0_im.reshape(nseq, 1, S), pw_re, pw_im, bblk, c_re, c_im,
      d_skip, w_glu, b_glu, out_norm)
    return outs[0], outs[1].reshape(nseq, S), outs[2].reshape(nseq, S)


def _sbprep_kernel(q_ref, k_ref, v_ref, gmat_ref, qg_ref, kg_ref, *refs, transposed):
    gmat = gmat_ref[...]

    def head_rms(x, gain):
        ms = _dot2_exact_rhs(x * x, gmat) * (1.0 / HD)
        return x * lax.rsqrt(ms + RMS_EPS) * gain

    if transposed:
        qb_ref, kb_ref, vb_ref, kt_ref, vt_ref = refs[-5:]
    else:
        qb_ref, kb_ref, vb_ref, kn_ref = refs
    qb_ref[...] = _bf(head_rms(q_ref[...], qg_ref[...]) * (HD ** -0.5))
    kn = head_rms(k_ref[...], kg_ref[...])
    v = v_ref[...]
    kb_ref[...] = _bf(kn)
    vb_ref[...] = _bf(v)
    if transposed:
        kt_ref[...] = kn.T
        vt_ref[...] = v.T
    else:
        kn_ref[...] = kn


def _sbprep_call(z, row0, nseq, t, gmat, q_gain, k_gain, layer=0, depth=0, kt_prev=None, vt_prev=None):
    tm = min(t, 512)
    nct = t // tm
    blk0 = row0 // tm
    zcol = lambda j: pl.BlockSpec((tm, GW), lambda b, c: (blk0 + b * nct + c, j))
    const = lambda shp: pl.BlockSpec(shp, lambda b, c: (0,) * len(shp))
    row = pl.BlockSpec((tm, GW), lambda b, c: (b * nct + c, 0))
    bf_shape = jax.ShapeDtypeStruct((nseq * t, GW), BF16)
    in_specs = [zcol(3), zcol(4), zcol(5), const((GW, GW)), const((1, GW)), const((1, GW))]
    args = [z, z, z, gmat, q_gain, k_gain]
    aliases = {}
    if depth:
        tspec = pl.BlockSpec((None, None, GW, tm), lambda b, c: (layer, b, 0, c))
        tshape = jax.ShapeDtypeStruct((depth, nseq, GW, t), F32)
        out_specs, out_shape = [row, row, row, tspec, tspec], [bf_shape, bf_shape, bf_shape, tshape, tshape]
        if kt_prev is not None:
            in_specs += [pl.BlockSpec(memory_space=pl.ANY)] * 2
            args += [kt_prev, vt_prev]
            aliases = {6: 3, 7: 4}
    else:
        out_specs = [row, row, row, row]
        out_shape = [bf_shape, bf_shape, bf_shape, jax.ShapeDtypeStruct((nseq * t, GW), F32)]
    return pl.pallas_call(
        functools.partial(_sbprep_kernel, transposed=bool(depth)),
        grid=(nseq, nct),
        in_specs=in_specs,
        out_specs=out_specs,
        out_shape=out_shape,
        input_output_aliases=aliases,
        compiler_params=_cparams("parallel", "parallel"),
        name="sbprep",
    )(*args)


def _sb_tiles(qhs, k2s, v2s, carries, umat, mask):
    qk = [functools.partial(lax.dot_general, qhs[p], k2s[p], _NT, preferred_element_type=F32)
          for p in range(len(qhs))]
    pv = [functools.partial(lambda w, v2: jnp.dot(w, v2, preferred_element_type=F32), v2=v2s[p])
          for p in range(len(qhs))]
    return _sb_stages(qk, pv, carries, umat, mask)


def _sb_stages(qk, pv, carries, umat, mask, skew=1):
    d = functools.partial(jnp.dot, preferred_element_type=F32)
    n = len(qk)
    pvs, tots = [None] * n, [None] * n

    def pair_stages(p):
        z = qk[p]()
        yield
        lk = -(jnp.maximum(z, 0.0) + jnp.log(1.0 + jnp.exp(-jnp.abs(z))))
        if mask is not None:
            lk = jnp.where(mask, lk, 0.0)
        aft = d(_bf(lk), umat)
        tots[p] = jnp.broadcast_to(aft[:, 0:1] + lk[:, 0:1], (lk.shape[0], 2 * HD))
        yield
        e = z + lk + aft
        if carries[p] is not None:
            reps = e.shape[1] // (2 * HD)
            e = e + (carries[p] if reps == 1 else jnp.concatenate([carries[p]] * reps, axis=1))
        w = jnp.exp(e)
        if mask is not None:
            w = jnp.where(mask, w, 0.0)
        pvs[p] = pv[p](_bf(w))

    gens = [pair_stages(p) for p in range(n)]
    done = [False] * n
    tick = 0
    while not all(done):
        for p in range(n):
            if not done[p] and tick >= p * skew:
                try:
                    next(gens[p])
                except StopIteration:
                    done[p] = True
        tick += 1
    return pvs, tots


def _sb_kernel(q_ref, kd_ref, vd_ref, kp_ref, vp_ref, on_ref, o_ref, qh_ref, acc_ref, carry_ref,
               *, tq, tk, npast, past_from_grid):
    PW = 2 * HD
    npairs = GW // PW
    m0 = _iota2(1, PW, 1) < HD
    row = _iota2(2 * tq, tq, 0)
    causal = _iota2(2 * tq, tq, 1) < jnp.where(row >= tq, row - tq, row)

    ud = jnp.where(_iota2(tq, tq, 0) > _iota2(tq, tq, 1), 1.0, 0.0).astype(BF16)
    up = jnp.where(_iota2(tk, tk, 0) > _iota2(tk, tk, 1), 1.0, 0.0).astype(BF16)
    pairs = [slice(p * PW, (p + 1) * PW) for p in range(npairs)]

    qhs = [_bf(_hat(q_ref[:, sl], m0)) for sl in pairs]
    pvs, tots = _sb_tiles(qhs, [_bf(kd_ref[:, sl]) for sl in pairs], [_bf(vd_ref[:, sl]) for sl in pairs],
                          [None] * npairs, ud, causal)
    for p in range(npairs):
        qh_ref[p] = qhs[p]
        acc_ref[p] = pvs[p]
        carry_ref[p] = tots[p]

    nblk = pl.program_id(1) * (tq // tk) if past_from_grid else npast

    def body(i, _):
        j = nblk - 1 - i
        r0 = pl.multiple_of(j * tk, tk)
        pvs, tots = _sb_tiles([qh_ref[p] for p in range(npairs)],
                              [_bf(kp_ref[pl.ds(r0, tk), sl]) for sl in pairs],
                              [_bf(vp_ref[pl.ds(r0, tk), sl]) for sl in pairs],
                              [carry_ref[p] for p in range(npairs)], up, None)
        for p in range(npairs):
            acc_ref[p] += pvs[p]
            carry_ref[p] += tots[p]
        return 0

    lax.fori_loop(0, nblk, body, 0)

    outs = []
    ss = jnp.zeros((tq, 1), F32)
    for p in range(npairs):
        a = acc_ref[p]
        o = jnp.where(m0, a[:tq], a[tq:])
        ss = ss + jnp.sum(o * o, axis=-1, keepdims=True)
        outs.append(o)
    scale = lax.rsqrt(ss * (1.0 / GW) + RMS_EPS)
    for p in range(npairs):
        sl = slice(p * PW, (p + 1) * PW)
        o_ref[:, sl] = outs[p] * scale * on_ref[:, sl]


def _sb_call(qn, kn, z, kpast, vpast, row0, nseq, t, tq, tk, past_len, v_col_blk, out_norm, layer=0):
    nq = t // tq
    blk0 = row0 // tq
    npairs = GW // (2 * HD)
    if kpast is None:
        kp_arr, vp_arr = kn, z
        sblk = row0 // t
        kp_spec = pl.BlockSpec((t, GW), lambda b, i: (sblk + b, 0))
        vp_spec = pl.BlockSpec((t, GW), lambda b, i: (sblk + b, v_col_blk))
        npast, from_grid = 0, True
    else:
        kp_arr, vp_arr = kpast, vpast
        kp_spec = pl.BlockSpec((None, None, past_len, GW), lambda b, i: (layer, b, 0, 0))
        vp_spec = kp_spec
        npast, from_grid = past_len // tk, False
    qrow = lambda b, i: (blk0 + b * nq + i, 0)
    return pl.pallas_call(
        functools.partial(_sb_kernel, tq=tq, tk=tk, npast=npast, past_from_grid=from_grid),
        grid=(nseq, nq),
        in_specs=[pl.BlockSpec((tq, GW), qrow), pl.BlockSpec((tq, GW), qrow),
                  pl.BlockSpec((tq, GW), lambda b, i: (blk0 + b * nq + i, v_col_blk)),
                  kp_spec, vp_spec, pl.BlockSpec((1, GW), lambda b, i: (0, 0))],
        out_specs=pl.BlockSpec((tq, GW), lambda b, i: (b * nq + i, 0)),
        out_shape=jax.ShapeDtypeStruct((nseq * t, GW), F32),
        scratch_shapes=[pltpu.VMEM((npairs, 2 * tq, 2 * HD), BF16), pltpu.VMEM((npairs, 2 * tq, 2 * HD), F32),
                        pltpu.VMEM((npairs, 2 * tq, 2 * HD), F32)],
        compiler_params=_cparams("parallel", "arbitrary"),
        name="sb_attn",
    )(qn, kn, z, kp_arr, vp_arr, out_norm)


def _sbs_kernel(q_ref, kd_ref, vd_ref, kt_ref, vt_ref, on_ref, o_ref, qh_ref, acc_ref, carry_ref,
                *, tq, tk, nblk, nchunks):
    c = pl.program_id(1)
    nh = GW // HD
    npairs = nh // 2
    hs = [slice(h * HD, (h + 1) * HD) for h in range(nh)]
    stack = functools.partial(jnp.concatenate, axis=0)
    mm = functools.partial(lax.dot_general, preferred_element_type=F32)

    @pl.when(c == 0)
    def _():
        row = _iota2(2 * tq, tq, 0)
        causal = _iota2(2 * tq, tq, 1) < jnp.where(row >= tq, row - tq, row)
        ud = jnp.where(_iota2(tq, tq, 0) > _iota2(tq, tq, 1), 1.0, 0.0).astype(BF16)
        qs = [_bf(q_ref[:, s]) for s in hs]
        kd = [_bf(kd_ref[:, s]) for s in hs]
        vd = [_bf(vd_ref[:, s]) for s in hs]
        for h in range(nh):
            qh_ref[h] = qs[h]
        qk = [lambda p=p: stack([mm(qs[2 * p], kd[2 * p], _NT), mm(qs[2 * p + 1], kd[2 * p + 1], _NT)])
              for p in range(npairs)]
        pv = [lambda w, p=p: stack([mm(w[:tq], vd[2 * p], _NN), mm(w[tq:], vd[2 * p + 1], _NN)])
              for p in range(npairs)]
        pvs, tots = _sb_stages(qk, pv, [None] * npairs, ud, causal, skew=0)
        for p in range(npairs):
            acc_ref[p] = pvs[p]
            carry_ref[p] = tots[p]

    up = jnp.where(_iota2(tk, tk, 0) > _iota2(tk, tk, 1), 1.0, 0.0).astype(BF16)
    for jb in reversed(range(nblk)):
        ks = slice(jb * tk, (jb + 1) * tk)
        qk = [lambda p=p: stack([mm(qh_ref[2 * p], _bf(kt_ref[2 * p, :, ks]), _NN),
                                 mm(qh_ref[2 * p + 1], _bf(kt_ref[2 * p + 1, :, ks]), _NN)])
              for p in range(npairs)]
        pv = [lambda w, p=p: stack([mm(w[:tq], _bf(vt_ref[2 * p, :, ks]), _NT),
                                    mm(w[tq:], _bf(vt_ref[2 * p + 1, :, ks]), _NT)])
              for p in range(npairs)]
        pvs, tots = _sb_stages(qk, pv, [carry_ref[p] for p in range(npairs)], up, None, skew=0)
        for p in range(npairs):
            acc_ref[p] += pvs[p]
            carry_ref[p] += tots[p]

    @pl.when(c == nchunks - 1)
    def _():
        heads = []
        for p in range(npairs):
            a = acc_ref[p]
            heads += [a[:tq], a[tq:]]
        o_ref[...] = _rms_rows(jnp.concatenate(heads, axis=1), on_ref[...])


def _sbs_call(qb, kb, vb, cache_kt, cache_vt, row0, nseq, tq, tk, tkc, out_norm, layer):
    past_len = cache_kt.shape[-1]
    nchunks = past_len // tkc
    blk0 = row0 // tq
    nh = GW // HD
    qrow = pl.BlockSpec((tq, GW), lambda b, c: (blk0 + b, 0))
    cspec = pl.BlockSpec((None, None, nh, HD, tkc), lambda b, c: (layer, b, 0, 0, nchunks - 1 - c))
    return pl.pallas_call(
        functools.partial(_sbs_kernel, tq=tq, tk=tk, nblk=tkc // tk, nchunks=nchunks),
        grid=(nseq, nchunks),
        in_specs=[qrow, qrow, qrow, cspec, cspec, pl.BlockSpec((1, GW), lambda b, c: (0, 0))],
        out_specs=pl.BlockSpec((tq, GW), lambda b, c: (b, 0)),
        out_shape=jax.ShapeDtypeStruct((nseq * tq, GW), F32),
        scratch_shapes=[pltpu.VMEM((nh, tq, HD), BF16), pltpu.VMEM((nh // 2, 2 * tq, HD), F32),
                        pltpu.VMEM((nh // 2, 2 * tq, 2 * HD), F32)],
        compiler_params=_cparams("parallel", "arbitrary"),
        name="sb_attn_cache",
    )(qb, kb, vb, cache_kt, cache_vt, out_norm)


def _rwkv_prep_kernel(zr_ref, zk_ref, zv_ref, zwa_ref, zg_ref, first_ref, mu_ref, w0_ref, w2_ref, a0_ref, a2_ref,
                      g2_ref, kk_ref, ka_ref, rk_ref, gmat_ref,
                      r_out, lw_out, k_out, v_out, kk_out, b_out, g_out, bonus_out, prev_ref, *, tm, wr):
    c = pl.program_id(1)
    cols = prev_ref.shape[1]

    @pl.when(c == 0)
    def _():
        prev_ref[...] = first_ref[0]

    first_row = _iota2(tm, 1, 0) == 0

    def shifted(ref, lo, width):
        x = ref[...]
        prev = jnp.where(first_row, prev_ref[:, lo:lo + width], pltpu.roll(x, 1, 0))
        return x + (prev - x) * mu_ref[:, lo:lo + width]

    r = shifted(zr_ref, 0, GW)
    k = shifted(zk_ref, GW, GW)
    v = shifted(zv_ref, 2 * GW, GW)
    wa = shifted(zwa_ref, 3 * GW, 2 * wr)
    gl = shifted(zg_ref, 3 * GW + 2 * wr, cols - 3 * GW - 2 * wr)
    for ref, lo in ((zr_ref, 0), (zk_ref, GW), (zv_ref, 2 * GW), (zwa_ref, 3 * GW), (zg_ref, 3 * GW + 2 * wr)):
        prev_ref[:, lo:lo + ref.shape[1]] = ref[tm - 1:tm, :]

    xw = w0_ref[...] + _dot(jnp.tanh(wa), w2_ref[...])
    w_log = -(jnp.maximum(-xw, 0.0) + jnp.log(1.0 + jnp.exp(-jnp.abs(xw)))) - 0.5
    a = jax.nn.sigmoid(a0_ref[...] + _dot(wa, a2_ref[...]))
    g_out[...] = _dot(jax.nn.sigmoid(gl), g2_ref[...])
    gmat = gmat_ref[...]
    kk = k * kk_ref[...]
    kk = kk / jnp.maximum(jnp.sqrt(_dot2_exact_rhs(kk * kk, gmat)), 1e-12)
    k = k * (1.0 + (a - 1.0) * ka_ref[...])
    r_out[...] = r
    lw_out[...] = -jnp.exp(w_log)
    k_out[...] = k
    v_out[...] = v
    kk_out[...] = kk
    b_out[...] = kk * a
    bonus_out[...] = _dot2_exact_rhs(r * k * rk_ref[...], gmat) * v


def _rwkv_prep_call(z, row0, nseq, t, tm, first, mu, w0, w2p, a0, a2p, g2, k_k, k_a, r_k, gmat, wr):
    nct = t // tm
    blk0 = row0 // tm
    cols = mu.shape[-1]
    zrow = lambda blkw, off: pl.BlockSpec((tm, blkw), lambda b, c: (blk0 + b * nct + c, off))
    c0 = (z.shape[1] - cols)
    const = lambda shp: pl.BlockSpec(shp, lambda b, c: (0,) * len(shp))
    orow = pl.BlockSpec((tm, GW), lambda b, c: (b * nct + c, 0))
    return pl.pallas_call(
        functools.partial(_rwkv_prep_kernel, tm=tm, wr=wr),
        grid=(nseq, nct),
        in_specs=[zrow(GW, c0 // GW), zrow(GW, c0 // GW + 1), zrow(GW, c0 // GW + 2),
                  zrow(2 * wr, (c0 + 3 * GW) // (2 * wr)), zrow(cols - 3 * GW - 2 * wr, (c0 + 3 * GW) // (2 * wr) + 1),
                  pl.BlockSpec((1, 1, cols), lambda b, c: (b, 0, 0)),
                  const((1, cols)), const((1, GW)), const((2 * wr, GW)), const((1, GW)), const((2 * wr, GW)),
                  const((cols - 3 * GW - 2 * wr, GW)), const((1, GW)), const((1, GW)), const((1, GW)),
                  const((GW, GW))],
        out_specs=[orow] * 8,
        out_shape=[jax.ShapeDtypeStruct((nseq * t, GW), F32)] * 8,
        scratch_shapes=[pltpu.VMEM((1, cols), F32)],
        compiler_params=_cparams("parallel", "arbitrary"),
        name="rwkv_prep",
    )(z, z, z, z, z, first.reshape(nseq, 1, cols), mu, w0, w2p, a0, a2p, g2, k_k, k_a, r_k, gmat)


def _hat(x, m0):
    return jnp.concatenate([jnp.where(m0, x, 0.0), jnp.where(m0, 0.0, x)], axis=0)


def _rwkv_chunk_kernel(r_ref, lw_ref, k_ref, v_ref, kk_ref, b_ref, rt_out, p3_out, m_out, n_out, *, group):
    C = RWKV_CHUNK
    C2 = 2 * C
    lane = _iota2(1, 2 * HD, 1)
    m0 = lane < HD
    ri = _iota2(C2, C2, 0)
    ci = _iota2(C2, C2, 1)
    same = (ri < C) == (ci < C)
    strict = jnp.logical_and(same, ri > ci)
    incl = jnp.logical_and(same, ri >= ci)
    ltri = jnp.where(_iota2(C, C, 0) >= _iota2(C, C, 1), 1.0, 0.0).astype(BF16)
    eye = _iota2(2 * HD, 2 * HD, 0) == _iota2(2 * HD, 2 * HD, 1)
    eye2 = ri == ci

    def pair_stages(p):
        sl = slice(p * 2 * HD, (p + 1) * 2 * HD)
        lw = lw_ref[:, sl]
        cl = _dot2_exact_lhs(ltri, lw)
        yield
        clast = cl[C - 1:C, :]
        kkt = _hat(kk_ref[:, sl] * jnp.exp(cl - lw), m0)
        rt = _hat(r_ref[:, sl] * jnp.exp(cl), m0)
        einv = jnp.exp(-cl)
        kb = _split(_hat(k_ref[:, sl] * einv, m0))
        bb = _split(_hat(b_ref[:, sl] * einv, m0))
        efin = jnp.exp(clast - cl)
        kh = _split(_hat(k_ref[:, sl] * efin, m0))
        bh = _split(_hat(b_ref[:, sl] * efin, m0))
        vh = _split(_hat(v_ref[:, sl], m0))
        lhs = _split(jnp.concatenate([kkt, rt], axis=0))
        gk = _dot3s(lhs, kb, _NT)
        gb = _dot3s(lhs, bb, _NT)
        yield
        a_kk = jnp.where(strict, gk[:C2], 0.0)
        a_rk = _split(jnp.where(incl, gk[C2:], 0.0))
        a_kb = jnp.where(strict, gb[:C2], 0.0)
        a_rb = _split(jnp.where(incl, gb[C2:], 0.0))
        p1 = _dot3s(_split(a_kk), vh)
        tinv = jnp.where(eye2, 1.0, 0.0) - a_kb
        lp = _split(a_kb)
        n = 2
        while n < C:
            yield
            lpf = _dot3s(lp, lp)
            lp = _split(lpf)
            tinv = tinv + _dot3s(_split(tinv), lp)
            n *= 2
        yield
        x = _dot3s(_split(tinv), _split(jnp.concatenate([kkt, p1], axis=1)))
        yield
        xs = _split(x)
        kt = (xs[0][:, :2 * HD], xs[1][:, :2 * HD])
        p2 = (xs[0][:, 2 * HD:], xs[1][:, 2 * HD:])
        y2 = _dot3s(a_rb, xs)
        p3 = _dot3s(a_rk, vh)
        mm = _dot3s(kt, bh, _TN)
        nn = _dot3s(vh, kh, _TN) - _dot3s(p2, bh, _TN)
        yield
        rt_out[p] = rt - y2[:, :2 * HD]
        p3_out[p] = p3 - y2[:, 2 * HD:]
        m_out[p] = jnp.where(eye, jnp.exp(clast), 0.0) - mm
        n_out[p] = nn

    npairs = GW // (2 * HD)
    for p0 in range(0, npairs, group):
        gens = [pair_stages(p) for p in range(p0, p0 + group)]
        while gens:
            for g in list(gens):
                try:
                    next(g)
                except StopIteration:
                    gens.remove(g)


def _rwkv_chunk_call(r, lw, k, v, kk, b):
    n = r.shape[0]
    C = RWKV_CHUNK
    nc = n // C
    npairs = GW // (2 * HD)
    row = pl.BlockSpec((C, GW), lambda i: (i, 0))
    blk = lambda rows: pl.BlockSpec((None, npairs, rows, 2 * HD), lambda i: (i, 0, 0, 0))
    shp = lambda rows: jax.ShapeDtypeStruct((nc, npairs, rows, 2 * HD), F32)
    return pl.pallas_call(
        functools.partial(_rwkv_chunk_kernel, group=4),
        grid=(nc,),
        in_specs=[row] * 6,
        out_specs=[blk(2 * C), blk(2 * C), blk(2 * HD), blk(2 * HD)],
        out_shape=[shp(2 * C), shp(2 * C), shp(2 * HD), shp(2 * HD)],
        compiler_params=_cparams("parallel"),
        name="rwkv_chunk",
    )(r, lw, k, v, kk, b)


def _rwkv_seq_kernel(rt_ref, p3_ref, m_ref, n_ref, s0_ref, g_ref, bonus_ref, lnw_ref, lnb_ref, gmat_ref,
                     o_ref, s_out_ref, s_ref, *, nct, gsz):
    c = pl.program_id(1)
    C = RWKV_CHUNK

    @pl.when(c == 0)
    def _():
        s_ref[...] = s0_ref[...]

    gm = gmat_ref[0:2 * HD, 0:2 * HD]
    npairs = GW // (2 * HD)
    pairs = [slice(p * 2 * HD, (p + 1) * 2 * HD) for p in range(npairs)]
    sp = [(s, p) for s in range(gsz) for p in range(npairs)]
    ss = [_split(s_ref[s, p]) for s, p in sp]
    yhs = [_dot3s(_split(rt_ref[s, p]), ss[i], _NT) + p3_ref[s, p] for i, (s, p) in enumerate(sp)]
    snew = [_dot3s(ss[i], _split(m_ref[s, p])) + n_ref[s, p] for i, (s, p) in enumerate(sp)]
    ys = [yh[:C] + yh[C:] for yh in yhs]
    means = [_dot2_exact_rhs(y, gm) * (1.0 / HD) for y in ys]
    ds = [y - m for y, m in zip(ys, means)]
    vrs = [_dot2_exact_rhs(d * d, gm) * (1.0 / HD) for d in ds]
    for i, (s, p) in enumerate(sp):
        sl = pairs[p]
        yn = ds[i] * lax.rsqrt(vrs[i] + GN_EPS) * lnw_ref[:, sl] + lnb_ref[:, sl]
        o_ref[s, :, sl] = (yn + bonus_ref[s, :, sl]) * g_ref[s, :, sl]
        s_ref[s, p] = snew[i]

    @pl.when(c == nct - 1)
    def _():
        s_out_ref[...] = s_ref[...]


def _rwkv_seq_call(rt, p3, m, n, nseq, nct, gsz, s0, g, bonus, lnx_w, lnx_b, gmat):
    C = RWKV_CHUNK
    npairs = GW // (2 * HD)
    c5 = lambda a: a.reshape(nseq, nct, *a.shape[1:])
    cblk = lambda rows: pl.BlockSpec((gsz, None, npairs, rows, 2 * HD), lambda b, c: (b, c, 0, 0, 0))
    sblk = pl.BlockSpec((gsz, npairs, 2 * HD, 2 * HD), lambda b, c: (b, 0, 0, 0))
    rowblk = pl.BlockSpec((gsz, C, GW), lambda b, c: (b, c, 0))
    const = lambda shp: pl.BlockSpec(shp, lambda b, c: (0,) * len(shp))
    od, s_out = pl.pallas_call(
        functools.partial(_rwkv_seq_kernel, nct=nct, gsz=gsz),
        grid=(nseq // gsz, nct),
        in_specs=[cblk(2 * C), cblk(2 * C), cblk(2 * HD), cblk(2 * HD), sblk, rowblk, rowblk,
                  const((1, GW)), const((1, GW)), const((GW, GW))],
        out_specs=[rowblk, sblk],
        out_shape=[jax.ShapeDtypeStruct((nseq, nct * C, GW), F32),
                   jax.ShapeDtypeStruct((nseq, npairs, 2 * HD, 2 * HD), F32)],
        scratch_shapes=[pltpu.VMEM((gsz, npairs, 2 * HD, 2 * HD), F32)],
        compiler_params=_cparams("parallel", "arbitrary"),
        name="rwkv_seq",
    )(c5(rt), c5(p3), c5(m), c5(n), s0, g.reshape(nseq, nct * C, GW), bonus.reshape(nseq, nct * C, GW),
      lnx_w, lnx_b, gmat)
    return od.reshape(nseq * nct * C, GW), s_out


def _pair_states(s):
    b, h = s.shape[:2]
    s = s.reshape(b, h // 2, 2, HD, HD)
    z = jnp.zeros_like(s[:, :, 0])
    top = jnp.concatenate([s[:, :, 0], z], axis=-1)
    bot = jnp.concatenate([z, s[:, :, 1]], axis=-1)
    return jnp.concatenate([top, bot], axis=-2)


def _unpair_states(s2):
    b, hp = s2.shape[:2]
    return jnp.stack([s2[:, :, :HD, :HD], s2[:, :, HD:, HD:]], axis=2).reshape(b, 2 * hp, HD, HD)


def _forward(xp, xs, cache_k, cache_v, s5_re0, s5_im0, wkv0, shift0, W):
    Bp, Tp, D = xp.shape
    Bs, Ts, _ = xs.shape
    depth = W['ln_ffn1'].shape[0]
    past_len = cache_k.shape[2]
    Np, Ns = Bp * Tp, Bs * Ts
    n_heads_d = GW // HD
    wr = W['rwkv_w2'].shape[1]
    dcols = W['rwkv_mu'].shape[-1]

    x = jnp.concatenate([xp.reshape(Np, D), xs.reshape(Ns, D)], axis=0)
    cache_kt = jnp.transpose(cache_k, (0, 1, 3, 4, 2))
    cache_vt = jnp.transpose(cache_v, (0, 1, 3, 4, 2))
    wbf = {k: _bf(W[k]) for k in ('w_ffn1_gate', 'w_ffn1_up', 'w_ffn1_down', 'w_in', 'w_out',
                                  'w_ffn2_gate', 'w_ffn2_up', 'w_ffn2_down')}
    hid = lax.broadcasted_iota(jnp.int32, (GW, GW), 0) // HD
    gmat = (hid == jnp.transpose(hid)).astype(BF16)
    grp = lax.broadcasted_iota(jnp.int32, (GW, S5_STATE), 0) // (GW // 32) == \
        lax.broadcasted_iota(jnp.int32, (GW, S5_STATE), 1) // (S5_STATE // 32)
    zeros_first = jnp.zeros((Bp, dcols), F32)
    zeros_s5 = jnp.zeros((Bp, S5_STATE), F32)
    zeros_wkv = jnp.zeros((Bp, n_heads_d // 2, 2 * HD, 2 * HD), F32)

    h = _rms_call(x, W['ln_ffn1'][0][None])
    outs = {k: [] for k in ('ks', 'vs', 's5rp', 's5ip', 's5rs', 's5is', 'wkvp', 'wkvs', 'shp', 'shs', 'gv')}
    kt_buf = vt_buf = None
    for l in range(depth):
        g1 = lambda name: W[name][l][None]
        x, h = _ffn_call(x, h, wbf['w_ffn1_gate'], wbf['w_ffn1_up'], wbf['w_ffn1_down'], g1('ln_mix'), l)
        z = _inproj_call(h, wbf['w_in'], l)

        oa_p, _ = _gmlp_call(z, 0, Np, min(Tp, GMLP_CHUNK), g1('gmlp_v_norm'), W['gmlp_ws'][l], W['gmlp_b'][l],
                             g1('out_norm_a'))
        oa_s, gv_s = _gmlp_call(z, Np, Ns, min(Ts, GMLP_CHUNK), g1('gmlp_v_norm'), W['gmlp_ws'][l], W['gmlp_b'][l],
                                g1('out_norm_a'))

        pw_re, pw_im, bb_re, bb_im = _s5_params(W['s5_lam_re'][l], W['s5_lam_im'][l], W['s5_log_dt'][l],
                                                W['s5_b_re'][l], W['s5_b_im'][l])
        bblk = jnp.concatenate([jnp.where(grp, jnp.tile(bb_re, (32, 1)), 0.0),
                                jnp.where(grp, jnp.tile(bb_im, (32, 1)), 0.0)], axis=1).astype(BF16)
        grp_t = jnp.transpose(grp)
        cre = jnp.where(grp_t, jnp.tile(jnp.transpose(W['s5_c_re'][l], (0, 2, 1)).reshape(S5_STATE, -1), (1, 32)),
                        0.0).astype(BF16)
        cim = jnp.where(grp_t, jnp.tile(jnp.transpose(W['s5_c_im'][l], (0, 2, 1)).reshape(S5_STATE, -1), (1, 32)),
                        0.0).astype(BF16)
        s5_args = (pw_re, pw_im, bblk, cre, cim, g1('s5_d'), W['s5_w_glu'][l], g1('s5_b_glu'), g1('out_norm_b'))
        ob_p, s5rp, s5ip = _s5_call(z, 0, Bp, Tp, zeros_s5, zeros_s5, *s5_args, col_blk=2, tc=min(Tp, 256))
        ob_s, s5rs, s5is = _s5_call(z, Np, Bs, Ts, s5_re0[l].reshape(Bs, S5_STATE), s5_im0[l].reshape(Bs, S5_STATE),
                                    *s5_args, col_blk=2, tc=min(Ts, 256))

        qg = jnp.tile(W['sb_q_norm'][l], GW // HD)[None]
        kg = jnp.tile(W['sb_k_norm'][l], GW // HD)[None]
        qb_p, kb_p, vb_p, kt_buf, vt_buf = _sbprep_call(z, 0, Bp, Tp, gmat, qg, kg, l, depth, kt_buf, vt_buf)
        qb_s, kb_s, vb_s, kn_s = _sbprep_call(z, Np, Bs, Ts, gmat, qg, kg)
        oc_p = _sb_call(qb_p, kb_p, vb_p, None, None, 0, Bp, Tp, min(Tp, SB_TQ), min(Tp, SB_TK), 0, 0,
                        g1('out_norm_c'))
        oc_s = _sbs_call(qb_s, kb_s, vb_s, cache_kt, cache_vt, 0, Bs, Ts, min(past_len, SB_TK_SAMPLE),
                         min(past_len, SB_CACHE_CHUNK), g1('out_norm_c'), l)

        w2p = jnp.concatenate([W['rwkv_w2'][l], jnp.zeros_like(W['rwkv_a2'][l])], axis=0)
        a2p = jnp.concatenate([jnp.zeros_like(W['rwkv_w2'][l]), W['rwkv_a2'][l]], axis=0)
        rk = W['rwkv_r_k'][l].reshape(1, GW)
        prep_args = (g1('rwkv_mu'), g1('rwkv_w0'), w2p, g1('rwkv_a0'), a2p, W['rwkv_g2'][l], g1('rwkv_k_k'),
                     g1('rwkv_k_a'), rk, gmat, wr)
        pp = _rwkv_prep_call(z, 0, Bp, Tp, min(Tp, 512), zeros_first, *prep_args)
        ps = _rwkv_prep_call(z, Np, Bs, Ts, Ts, shift0[l], *prep_args)
        seq_args = (g1('rwkv_lnx_w'), g1('rwkv_lnx_b'), gmat)
        od_p, wkv_p = _rwkv_seq_call(*_rwkv_chunk_call(*pp[:6]), Bp, Tp // RWKV_CHUNK, min(Bp, 2), zeros_wkv,
                                     pp[6], pp[7], *seq_args)
        od_s, wkv_s = _rwkv_seq_call(*_rwkv_chunk_call(*ps[:6]), Bs, Ts // RWKV_CHUNK, min(Bs, 4),
                                     _pair_states(wkv0[l]), ps[6], ps[7], *seq_args)

        gain_next = g1('ln_ffn2')
        x, h = _outproj_call(x, (oa_p, ob_p, oc_p, od_p), (oa_s, ob_s, oc_s, od_s), wbf['w_out'], gain_next, l)
        gain_next = W['ln_ffn1'][l + 1][None] if l + 1 < depth else g1('ln_ffn2')
        x, h = _ffn_call(x, h, wbf['w_ffn2_gate'], wbf['w_ffn2_up'], wbf['w_ffn2_down'], gain_next, l)

        nh = GW // HD
        outs['ks'].append(kn_s.reshape(Bs, Ts, nh, HD))
        outs['vs'].append(z[Np:, 5 * GW:6 * GW].reshape(Bs, Ts, nh, HD))
        outs['s5rp'].append(s5rp.reshape(Bp, 32, -1))
        outs['s5ip'].append(s5ip.reshape(Bp, 32, -1))
        outs['s5rs'].append(s5rs.reshape(Bs, 32, -1))
        outs['s5is'].append(s5is.reshape(Bs, 32, -1))
        outs['wkvp'].append(_unpair_states(wkv_p))
        outs['wkvs'].append(_unpair_states(wkv_s))
        zd = z[:, z.shape[1] - dcols:]
        outs['shp'].append(zd[:Np].reshape(Bp, Tp, dcols)[:, -1])
        outs['shs'].append(zd[Np:].reshape(Bs, Ts, dcols)[:, -1])
        outs['gv'].append(gv_s.reshape(Bs, Ts, GW))

    st = lambda k: jnp.stack(outs[k], axis=0)
    untr = lambda a: jnp.transpose(a.reshape(depth, Bp, GW // HD, HD, Tp), (0, 1, 4, 2, 3))
    return (x[:Np].reshape(Bp, Tp, D), x[Np:].reshape(Bs, Ts, D),
            untr(kt_buf), untr(vt_buf), st('ks'), st('vs'),
            st('s5rp'), st('s5ip'), st('s5rs'), st('s5is'),
            st('wkvp'), st('wkvs'), st('shp'), st('shs'), st('gv'))


_WEIGHT_NAMES = ('ln_ffn1', 'w_ffn1_gate', 'w_ffn1_up', 'w_ffn1_down', 'ln_mix', 'w_in',
                 'gmlp_v_norm', 'gmlp_ws', 'gmlp_b', 'out_norm_a',
                 's5_lam_re', 's5_lam_im', 's5_log_dt', 's5_b_re', 's5_b_im', 's5_c_re', 's5_c_im', 's5_d',
                 's5_w_glu', 's5_b_glu', 'out_norm_b',
                 'sb_q_norm', 'sb_k_norm', 'out_norm_c',
                 'rwkv_mu', 'rwkv_w0', 'rwkv_w2', 'rwkv_a0', 'rwkv_a2', 'rwkv_g2', 'rwkv_k_k', 'rwkv_k_a', 'rwkv_r_k',
                 'rwkv_lnx_w', 'rwkv_lnx_b',
                 'w_out', 'ln_ffn2', 'w_ffn2_gate', 'w_ffn2_up', 'w_ffn2_down')


def kernel(x_prompt, x_sample, cache_sb_k, cache_sb_v, state_s5_re, state_s5_im, state_rwkv_wkv, state_rwkv_shift,
           ln_ffn1, w_ffn1_gate, w_ffn1_up, w_ffn1_down, ln_mix, w_in,
           gmlp_v_norm, gmlp_ws, gmlp_b, out_norm_a,
           s5_lam_re, s5_lam_im, s5_log_dt, s5_b_re, s5_b_im, s5_c_re, s5_c_im, s5_d, s5_w_glu, s5_b_glu, out_norm_b,
           sb_q_norm, sb_k_norm, out_norm_c,
           rwkv_mu, rwkv_w0, rwkv_w2, rwkv_a0, rwkv_a2, rwkv_g2, rwkv_k_k, rwkv_k_a, rwkv_r_k, rwkv_lnx_w, rwkv_lnx_b,
           w_out, ln_ffn2, w_ffn2_gate, w_ffn2_up, w_ffn2_down):
    weights = (ln_ffn1, w_ffn1_gate, w_ffn1_up, w_ffn1_down, ln_mix, w_in,
               gmlp_v_norm, gmlp_ws, gmlp_b, out_norm_a,
               s5_lam_re, s5_lam_im, s5_log_dt, s5_b_re, s5_b_im, s5_c_re, s5_c_im, s5_d, s5_w_glu, s5_b_glu,
               out_norm_b, sb_q_norm, sb_k_norm, out_norm_c,
               rwkv_mu, rwkv_w0, rwkv_w2, rwkv_a0, rwkv_a2, rwkv_g2, rwkv_k_k, rwkv_k_a, rwkv_r_k, rwkv_lnx_w,
               rwkv_lnx_b, w_out, ln_ffn2, w_ffn2_gate, w_ffn2_up, w_ffn2_down)
    W = dict(zip(_WEIGHT_NAMES, weights))
    return _forward(x_prompt, x_sample, cache_sb_k, cache_sb_v, state_s5_re, state_s5_im, state_rwkv_wkv,
                    state_rwkv_shift, W)
```

```python
import functools

import jax
import jax.numpy as jnp
from jax import lax
from jax.experimental import pallas as pl
from jax.experimental.pallas import tpu as pltpu

F32 = jnp.float32
BF16 = jnp.bfloat16

RMS_EPS = 1e-6
GN_EPS = 64e-5
A_HEADS = 4
GMLP_CHUNK = 128
HD = 64
GW = 512
S5_STATE = 2048
RWKV_CHUNK = 64
SB_TQ, SB_TK = 256, 256
SB_TK_SAMPLE = 256
SB_CACHE_CHUNK = 1024
VMEM_LIMIT = 56 * 1024 * 1024


def _cparams(*sem):
    return pltpu.CompilerParams(dimension_semantics=sem, vmem_limit_bytes=VMEM_LIMIT)


def _bf(x):
    return x.astype(BF16)


_NN = (((1,), (0,)), ((), ()))
_NT = (((1,), (1,)), ((), ()))
_TN = (((0,), (0,)), ((), ()))


def _dot(a, b, dims=_NN):
    return lax.dot_general(_bf(a), _bf(b), dims, preferred_element_type=F32)


def _split(x):
    hi = _bf(x)
    lo = _bf(x - hi.astype(F32))
    return hi, lo


def _dot3(a, b, dims=_NN):
    ah, al = _split(a)
    bh, bl = _split(b)
    d = functools.partial(lax.dot_general, dimension_numbers=dims, preferred_element_type=F32)
    return d(ah, bh) + (d(ah, bl) + d(al, bh))


def _dot3s(a, b, dims=_NN):
    d = functools.partial(lax.dot_general, dimension_numbers=dims, preferred_element_type=F32)
    return d(a[0], b[0]) + d(a[1], b[0])


def _dot2_exact_rhs(a, b_bf, dims=_NN):
    ah, al = _split(a)
    d = functools.partial(lax.dot_general, dimension_numbers=dims, preferred_element_type=F32)
    return d(ah, b_bf) + d(al, b_bf)


def _dot2_exact_lhs(a_bf, b, dims=_NN):
    bh, bl = _split(b)
    d = functools.partial(lax.dot_general, dimension_numbers=dims, preferred_element_type=F32)
    return d(a_bf, bh) + d(a_bf, bl)


def _round_robin(gens, skew=0):
    done = [False] * len(gens)
    tick = 0
    while not all(done):
        for i, g in enumerate(gens):
            if not done[i] and tick >= i * skew:
                try:
                    next(g)
                except StopIteration:
                    done[i] = True
        tick += 1


def _rms_rows(x, gain):
    ms = jnp.mean(x * x, axis=-1, keepdims=True)
    return x * lax.rsqrt(ms + RMS_EPS) * gain


def _iota2(n, m, axis):
    return lax.broadcasted_iota(jnp.int32, (n, m), axis)


def _rms_kernel(x_ref, g_ref, h_ref):
    h_ref[...] = _bf(_rms_rows(x_ref[...], g_ref[...]))


def _rms_call(x, gain, tm=512):
    n, d = x.shape
    return pl.pallas_call(
        _rms_kernel,
        grid=(n // tm,),
        in_specs=[pl.BlockSpec((tm, d), lambda i: (i, 0)), pl.BlockSpec((1, d), lambda i: (0, 0))],
        out_specs=pl.BlockSpec((tm, d), lambda i: (i, 0)),
        out_shape=jax.ShapeDtypeStruct((n, d), BF16),
        compiler_params=_cparams("parallel"),
        name="rms",
    )(x, gain)


def _ffn_kernel(x_ref, h_ref, wg_ref, wu_ref, wd_ref, gn_ref, o_ref, hn_ref, *, nj):
    j = pl.program_id(1)

    @pl.when(j == 0)
    def _():
        o_ref[...] = jnp.zeros_like(o_ref)

    h = h_ref[...]
    g = jnp.dot(h, wg_ref[...], preferred_element_type=F32)
    u = jnp.dot(h, wu_ref[...], preferred_element_type=F32)
    a = _bf(g * jax.nn.sigmoid(g) * u)
    o_ref[...] += jnp.dot(a, wd_ref[...], preferred_element_type=F32)

    @pl.when(j == nj - 1)
    def _():
        y = x_ref[...] + 0.5 * o_ref[...]
        o_ref[...] = y
        hn_ref[...] = _bf(_rms_rows(y, gn_ref[...]))


def _ffn_call(x, h, wg, wu, wd, gain_next, layer, tm=512, tf=512):
    n, d = x.shape
    ff = wg.shape[-1]
    nj = ff // tf
    return pl.pallas_call(
        functools.partial(_ffn_kernel, nj=nj),
        grid=(n // tm, nj),
        in_specs=[
            pl.BlockSpec((tm, d), lambda i, j: (i, 0)),
            pl.BlockSpec((tm, d), lambda i, j: (i, 0)),
            pl.BlockSpec((None, d, tf), lambda i, j: (layer, 0, j)),
            pl.BlockSpec((None, d, tf), lambda i, j: (layer, 0, j)),
            pl.BlockSpec((None, tf, d), lambda i, j: (layer, j, 0)),
            pl.BlockSpec((1, d), lambda i, j: (0, 0)),
        ],
        out_specs=[pl.BlockSpec((tm, d), lambda i, j: (i, 0)), pl.BlockSpec((tm, d), lambda i, j: (i, 0))],
        out_shape=[jax.ShapeDtypeStruct((n, d), F32), jax.ShapeDtypeStruct((n, d), BF16)],
        compiler_params=_cparams("parallel", "arbitrary"),
        name="ffn",
    )(x, h, wg, wu, wd, gain_next)


def _inproj_kernel(h_ref, w_ref, z_ref):
    z_ref[...] = jnp.dot(h_ref[...], w_ref[...], preferred_element_type=F32)


def _inproj_call(h, w_in, layer, tm=512):
    n, d = h.shape
    cols = w_in.shape[-1]
    tn = cols // 2
    return pl.pallas_call(
        _inproj_kernel,
        grid=(2, n // tm),
        in_specs=[pl.BlockSpec((tm, d), lambda j, i: (i, 0)),
                  pl.BlockSpec((None, d, tn), lambda j, i: (layer, 0, j))],
        out_specs=pl.BlockSpec((tm, tn), lambda j, i: (i, j)),
        out_shape=jax.ShapeDtypeStruct((n, cols), F32),
        compiler_params=_cparams("parallel", "parallel"),
        name="inproj",
    )(h, w_in)


def _outproj_kernel(x_ref, *refs, n_first):
    first, second = refs[0:4], refs[4:8]
    w_ref, gn_ref, o_ref, hn_ref = refs[8:]

    def run(mix_refs):
        acc = x_ref[...]
        for i, r in enumerate(mix_refs):
            acc = acc + jnp.dot(_bf(r[...]), w_ref[i * GW:(i + 1) * GW, :], preferred_element_type=F32)
        o_ref[...] = acc
        hn_ref[...] = _bf(_rms_rows(acc, gn_ref[...]))

    i = pl.program_id(0)
    pl.when(i < n_first)(lambda: run(first))
    pl.when(i >= n_first)(lambda: run(second))


def _outproj_call(x, mix_first, mix_second, w_out, gain_next, layer, tm=256):
    n, d = x.shape
    n_first = mix_first[0].shape[0] // tm
    n_second = mix_second[0].shape[0] // tm
    row = lambda i: (i, 0)
    first_row = lambda i: (jnp.minimum(i, n_first - 1), 0)
    second_row = lambda i: (jnp.maximum(i - n_first, 0), 0)
    return pl.pallas_call(
        functools.partial(_outproj_kernel, n_first=n_first),
        grid=(n_first + n_second,),
        in_specs=[pl.BlockSpec((tm, d), row)] + [pl.BlockSpec((tm, GW), first_row)] * 4
        + [pl.BlockSpec((tm, GW), second_row)] * 4
        + [pl.BlockSpec((None, d, d), lambda i: (layer, 0, 0)), pl.BlockSpec((1, d), lambda i: (0, 0))],
        out_specs=[pl.BlockSpec((tm, d), row), pl.BlockSpec((tm, d), row)],
        out_shape=[jax.ShapeDtypeStruct((n, d), F32), jax.ShapeDtypeStruct((n, d), BF16)],
        compiler_params=_cparams("parallel"),
        name="outproj",
    )(x, *mix_first, *mix_second, w_out, gain_next)


def _gmlp_kernel(z_ref, vn_ref, ws_ref, bt_ref, on_ref, o_ref, v_ref, *, L, nch):
    causal = _iota2(L, L, 1) <= _iota2(L, L, 0)
    hw = GW // A_HEADS
    ws = [_bf(jnp.where(causal, ws_ref[h], 0.0)) for h in range(A_HEADS)]

    def chunk_stages(c):
        rows = slice(c * L, (c + 1) * L)
        z = jax.nn.gelu(z_ref[rows, :])
        vhs = [_rms_rows(z[:, GW + h * hw:GW + (h + 1) * hw], vn_ref[...]) for h in range(A_HEADS)]
        for h in range(A_HEADS):
            v_ref[rows, h * hw:(h + 1) * hw] = vhs[h]
        yield
        ss = [_dot(ws[h], vhs[h]) for h in range(A_HEADS)]
        yield
        outs = [z[:, h * hw:(h + 1) * hw] * (ss[h] + bt_ref[:, h:h + 1]) for h in range(A_HEADS)]
        sq = sum(jnp.sum(o * o, axis=-1, keepdims=True) for o in outs)
        scale = lax.rsqrt(sq * (1.0 / GW) + RMS_EPS)
        for h in range(A_HEADS):
            o_ref[rows, h * hw:(h + 1) * hw] = outs[h] * scale * on_ref[:, h * hw:(h + 1) * hw]

    _round_robin([chunk_stages(c) for c in range(nch)])


def _gmlp_call(z, row0, nrows, L, v_norm, ws, b, out_norm, nch=4):
    tm = nch * L
    nb = nrows // tm
    b0 = row0 // tm
    ws_l = ws[:, :L, :L]
    bt = jnp.transpose(b[:, :L])
    row = lambda i: (i, 0)
    return pl.pallas_call(
        functools.partial(_gmlp_kernel, L=L, nch=nch),
        grid=(nb,),
        in_specs=[pl.BlockSpec((tm, 2 * GW), lambda i: (b0 + i, 0)),
                  pl.BlockSpec((1, GW // A_HEADS), lambda i: (0, 0)),
                  pl.BlockSpec((A_HEADS, L, L), lambda i: (0, 0, 0)),
                  pl.BlockSpec((L, A_HEADS), lambda i: (0, 0)),
                  pl.BlockSpec((1, GW), lambda i: (0, 0))],
        out_specs=[pl.BlockSpec((tm, GW), row), pl.BlockSpec((tm, GW), row)],
        out_shape=[jax.ShapeDtypeStruct((nrows, GW), F32), jax.ShapeDtypeStruct((nrows, GW), F32)],
        compiler_params=_cparams("parallel"),
        name="gmlp",
    )(z, v_norm, ws_l, bt, out_norm)


def _s5_param_kernel(lr_ref, li_ref, ldt_ref, brt_ref, bit_ref, pw_re_ref, pw_im_ref, bb_re_ref, bb_im_ref):
    lr = lr_ref[...]
    li = li_ref[...]
    dt = jnp.exp(ldt_ref[...])
    n = (_iota2(8, S5_STATE, 0) + 1).astype(F32)
    mag = jnp.exp(n * (lr * dt))
    ang = n * (li * dt)
    pw_re = mag * jnp.cos(ang)
    pw_im = mag * jnp.sin(ang)
    pw_re_ref[...] = pw_re
    pw_im_ref[...] = pw_im
    ab_re = pw_re[0:1, :]
    ab_im = pw_im[0:1, :]
    den = lr * lr + li * li
    nr, ni = ab_re - 1.0, ab_im
    cf_re = (nr * lr + ni * li) / den
    cf_im = (ni * lr - nr * li) / den
    br = brt_ref[...]
    bi = bit_ref[...]
    bb_re_ref[...] = cf_re * br - cf_im * bi
    bb_im_ref[...] = cf_re * bi + cf_im * br


def _s5_params(lam_re, lam_im, log_dt, b_re, b_im):
    g, p = lam_re.shape
    s = g * p
    ch = b_re.shape[-1]
    ldt = jnp.broadcast_to(log_dt[:, None], (g, p)).reshape(1, s)
    brt = jnp.transpose(b_re.reshape(s, ch))
    bit = jnp.transpose(b_im.reshape(s, ch))
    return pl.pallas_call(
        _s5_param_kernel,
        out_shape=[jax.ShapeDtypeStruct((8, s), F32), jax.ShapeDtypeStruct((8, s), F32),
                   jax.ShapeDtypeStruct((ch, s), F32), jax.ShapeDtypeStruct((ch, s), F32)],
        name="s5_params",
    )(lam_re.reshape(1, s), lam_im.reshape(1, s), ldt, brt, bit)


def _cmul(ar, ai, xr, xi):
    return ar * xr - ai * xi, ar * xi + ai * xr


def _s5_kernel(u_ref, h0r_ref, h0i_ref, pwr_ref, pwi_ref, bblk_ref, cre_ref, cim_ref, d_ref, wglu_ref, bglu_ref,
               on_ref, o_ref, hr_out_ref, hi_out_ref, bu_ref, hre_ref, him_ref, cr_ref, ci_ref, *, tc, nct):
    c = pl.program_id(1)
    S = S5_STATE
    LW = 512

    @pl.when(c == 0)
    def _():
        cr_ref[...] = h0r_ref[0]
        ci_ref[...] = h0i_ref[0]

    u = u_ref[...]
    NK = GW // 128
    SW = S // NK
    for kc in range(NK):
        ukc = _bf(u[:, kc * 128:(kc + 1) * 128])
        for half in range(2):
            cs = slice(half * S + kc * SW, half * S + (kc + 1) * SW)
            bu_ref[:, cs] = jnp.dot(ukc, bblk_ref[kc * 128:(kc + 1) * 128, cs], preferred_element_type=F32)

    rows = _iota2(8, LW, 0)
    for lc in range(S // LW):
        sl = slice(lc * LW, (lc + 1) * LW)
        pr = pwr_ref[:, sl]
        pi = pwi_ref[:, sl]
        a1r = jnp.where(rows >= 1, pr[0:1, :], 0.0)
        a1i = jnp.where(rows >= 1, pi[0:1, :], 0.0)
        a2r = jnp.where(rows >= 2, pr[1:2, :], 0.0)
        a2i = jnp.where(rows >= 2, pi[1:2, :], 0.0)
        a4r = jnp.where(rows >= 4, pr[3:4, :], 0.0)
        a4i = jnp.where(rows >= 4, pi[3:4, :], 0.0)

        def tile(i, carry):
            kr, ki = carry
            r0 = pl.multiple_of(i * 8, 8)
            xr = bu_ref[pl.ds(r0, 8), lc * LW:(lc + 1) * LW]
            xi = bu_ref[pl.ds(r0, 8), S + lc * LW:S + (lc + 1) * LW]
            for (ar, ai, sh) in ((a1r, a1i, 1), (a2r, a2i, 2), (a4r, a4i, 4)):
                sr, si = _cmul(ar, ai, pltpu.roll(xr, sh, 0), pltpu.roll(xi, sh, 0))
                xr, xi = xr + sr, xi + si
            sr, si = _cmul(pr, pi, kr, ki)
            xr, xi = xr + sr, xi + si
            hre_ref[pl.ds(r0, 8), sl] = xr
            him_ref[pl.ds(r0, 8), sl] = xi
            return xr[7:8, :], xi[7:8, :]

        kr, ki = lax.fori_loop(0, tc // 8, tile, (cr_ref[:, sl], ci_ref[:, sl]))
        cr_ref[:, sl] = kr
        ci_ref[:, sl] = ki

    ys = []
    for kc in range(NK):
        ss, os_ = slice(kc * SW, (kc + 1) * SW), slice(kc * 128, (kc + 1) * 128)
        ys.append(jnp.dot(_bf(hre_ref[:, ss]), cre_ref[ss, os_], preferred_element_type=F32)
                  - jnp.dot(_bf(him_ref[:, ss]), cim_ref[ss, os_], preferred_element_type=F32))
    y = jnp.concatenate(ys, axis=1) + d_ref[...] * u
    g = jax.nn.gelu(y)
    out = g * jax.nn.sigmoid(_dot(g, wglu_ref[...]) + bglu_ref[...])
    o_ref[...] = _rms_rows(out, on_ref[...])

    @pl.when(c == nct - 1)
    def _():
        hr_out_ref[0] = cr_ref[...]
        hi_out_ref[0] = ci_ref[...]


def _s5_call(z, row0, nseq, t, h0_re, h0_im, pw_re, pw_im, bblk, c_re, c_im, d_skip, w_glu, b_glu, out_norm,
             col_blk, tc):
    nct = t // tc
    S = S5_STATE
    blk0 = row0 // tc
    const2 = lambda b, c: (0, 0)
    outs = pl.pallas_call(
        functools.partial(_s5_kernel, tc=tc, nct=nct),
        grid=(nseq, nct),
        in_specs=[pl.BlockSpec((tc, GW), lambda b, c: (blk0 + b * nct + c, col_blk)),
                  pl.BlockSpec((1, 1, S), lambda b, c: (b, 0, 0)),
                  pl.BlockSpec((1, 1, S), lambda b, c: (b, 0, 0)),
                  pl.BlockSpec((8, S), const2), pl.BlockSpec((8, S), const2),
                  pl.BlockSpec((GW, 2 * S), const2),
                  pl.BlockSpec((S, GW), const2), pl.BlockSpec((S, GW), const2),
                  pl.BlockSpec((1, GW), const2),
                  pl.BlockSpec((GW, GW), const2), pl.BlockSpec((1, GW), const2), pl.BlockSpec((1, GW), const2)],
        out_specs=[pl.BlockSpec((tc, GW), lambda b, c: (b * nct + c, 0)),
                   pl.BlockSpec((1, 1, S), lambda b, c: (b, 0, 0)),
                   pl.BlockSpec((1, 1, S), lambda b, c: (b, 0, 0))],
        out_shape=[jax.ShapeDtypeStruct((nseq * t, GW), F32),
                   jax.ShapeDtypeStruct((nseq, 1, S), F32), jax.ShapeDtypeStruct((nseq, 1, S), F32)],
        scratch_shapes=[pltpu.VMEM((tc, 2 * S), F32), pltpu.VMEM((tc, S), F32), pltpu.VMEM((tc, S), F32),
                        pltpu.VMEM((1, S), F32), pltpu.VMEM((1, S), F32)],
        compiler_params=_cparams("parallel", "arbitrary"),
        name="s5",
    )(z, h0_re.reshape(nseq, 1, S), h0_im.reshape(nseq, 1, S), pw_re, pw_im, bblk, c_re, c_im,
      d_skip, w_glu, b_glu, out_norm)
    return outs[0], outs[1].reshape(nseq, S), outs[2].reshape(nseq, S)


def _sbprep_kernel(q_ref, k_ref, v_ref, gmat_ref, qg_ref, kg_ref, *refs, transposed):
    gmat = gmat_ref[...]

    def head_rms(x, gain):
        ms = _dot2_exact_rhs(x * x, gmat) * (1.0 / HD)
        return x * lax.rsqrt(ms + RMS_EPS) * gain

    if transposed:
        qb_ref, kb_ref, vb_ref, kt_ref, vt_ref = refs[-5:]
    else:
        qb_ref, kb_ref, vb_ref, kn_ref = refs
    qb_ref[...] = _bf(head_rms(q_ref[...], qg_ref[...]) * (HD ** -0.5))
    kn = head_rms(k_ref[...], kg_ref[...])
    v = v_ref[...]
    kb_ref[...] = _bf(kn)
    vb_ref[...] = _bf(v)
    if transposed:
        kt_ref[...] = kn.T
        vt_ref[...] = v.T
    else:
        kn_ref[...] = kn


def _sbprep_call(z, row0, nseq, t, gmat, q_gain, k_gain, layer=0, depth=0, kt_prev=None, vt_prev=None):
    tm = min(t, 512)
    nct = t // tm
    blk0 = row0 // tm
    zcol = lambda j: pl.BlockSpec((tm, GW), lambda b, c: (blk0 + b * nct + c, j))
    const = lambda shp: pl.BlockSpec(shp, lambda b, c: (0,) * len(shp))
    row = pl.BlockSpec((tm, GW), lambda b, c: (b * nct + c, 0))
    bf_shape = jax.ShapeDtypeStruct((nseq * t, GW), BF16)
    in_specs = [zcol(3), zcol(4), zcol(5), const((GW, GW)), const((1, GW)), const((1, GW))]
    args = [z, z, z, gmat, q_gain, k_gain]
    aliases = {}
    if depth:
        tspec = pl.BlockSpec((None, None, GW, tm), lambda b, c: (layer, b, 0, c))
        tshape = jax.ShapeDtypeStruct((depth, nseq, GW, t), F32)
        out_specs, out_shape = [row, row, row, tspec, tspec], [bf_shape, bf_shape, bf_shape, tshape, tshape]
        if kt_prev is not None:
            in_specs += [pl.BlockSpec(memory_space=pl.ANY)] * 2
            args += [kt_prev, vt_prev]
            aliases = {6: 3, 7: 4}
    else:
        out_specs = [row, row, row, row]
        out_shape = [bf_shape, bf_shape, bf_shape, jax.ShapeDtypeStruct((nseq * t, GW), F32)]
    return pl.pallas_call(
        functools.partial(_sbprep_kernel, transposed=bool(depth)),
        grid=(nseq, nct),
        in_specs=in_specs,
        out_specs=out_specs,
        out_shape=out_shape,
        input_output_aliases=aliases,
        compiler_params=_cparams("parallel", "parallel"),
        name="sbprep",
    )(*args)


def _sb_tiles(qhs, k2s, v2s, carries, umat, mask):
    qk = [functools.partial(lax.dot_general, qhs[p], k2s[p], _NT, preferred_element_type=F32)
          for p in range(len(qhs))]
    pv = [functools.partial(lambda w, v2: jnp.dot(w, v2, preferred_element_type=F32), v2=v2s[p])
          for p in range(len(qhs))]
    return _sb_stages(qk, pv, carries, umat, mask)


def _sb_stages(qk, pv, carries, umat, mask, skew=1):
    d = functools.partial(jnp.dot, preferred_element_type=F32)
    n = len(qk)
    pvs, tots = [None] * n, [None] * n

    def pair_stages(p):
        z = qk[p]()
        yield
        lk = -(jnp.maximum(z, 0.0) + jnp.log(1.0 + jnp.exp(-jnp.abs(z))))
        if mask is not None:
            lk = jnp.where(mask, lk, 0.0)
        aft = d(_bf(lk), umat)
        tots[p] = jnp.broadcast_to(aft[:, 0:1] + lk[:, 0:1], (lk.shape[0], 2 * HD))
        yield
        e = z + lk + aft
        if carries[p] is not None:
            reps = e.shape[1] // (2 * HD)
            e = e + (carries[p] if reps == 1 else jnp.concatenate([carries[p]] * reps, axis=1))
        w = jnp.exp(e)
        if mask is not None:
            w = jnp.where(mask, w, 0.0)
        pvs[p] = pv[p](_bf(w))

    _round_robin([pair_stages(p) for p in range(n)], skew)
    return pvs, tots


def _sb_kernel(q_ref, kd_ref, vd_ref, kp_ref, vp_ref, on_ref, o_ref, qh_ref, acc_ref, carry_ref,
               *, tq, tk, npast, past_from_grid):
    PW = 2 * HD
    npairs = GW // PW
    m0 = _iota2(1, PW, 1) < HD
    row = _iota2(2 * tq, tq, 0)
    causal = _iota2(2 * tq, tq, 1) < jnp.where(row >= tq, row - tq, row)

    ud = jnp.where(_iota2(tq, tq, 0) > _iota2(tq, tq, 1), 1.0, 0.0).astype(BF16)
    up = jnp.where(_iota2(tk, tk, 0) > _iota2(tk, tk, 1), 1.0, 0.0).astype(BF16)
    pairs = [slice(p * PW, (p + 1) * PW) for p in range(npairs)]

    qhs = [_bf(_hat(q_ref[:, sl], m0)) for sl in pairs]
    pvs, tots = _sb_tiles(qhs, [_bf(kd_ref[:, sl]) for sl in pairs], [_bf(vd_ref[:, sl]) for sl in pairs],
                          [None] * npairs, ud, causal)
    for p in range(npairs):
        qh_ref[p] = qhs[p]
        acc_ref[p] = pvs[p]
        carry_ref[p] = tots[p]

    nblk = pl.program_id(1) * (tq // tk) if past_from_grid else npast

    def body(i, _):
        j = nblk - 1 - i
        r0 = pl.multiple_of(j * tk, tk)
        pvs, tots = _sb_tiles([qh_ref[p] for p in range(npairs)],
                              [_bf(kp_ref[pl.ds(r0, tk), sl]) for sl in pairs],
                              [_bf(vp_ref[pl.ds(r0, tk), sl]) for sl in pairs],
                              [carry_ref[p] for p in range(npairs)], up, None)
        for p in range(npairs):
            acc_ref[p] += pvs[p]
            carry_ref[p] += tots[p]
        return 0

    lax.fori_loop(0, nblk, body, 0)

    outs = []
    ss = jnp.zeros((tq, 1), F32)
    for p in range(npairs):
        a = acc_ref[p]
        o = jnp.where(m0, a[:tq], a[tq:])
        ss = ss + jnp.sum(o * o, axis=-1, keepdims=True)
        outs.append(o)
    scale = lax.rsqrt(ss * (1.0 / GW) + RMS_EPS)
    for p in range(npairs):
        sl = slice(p * PW, (p + 1) * PW)
        o_ref[:, sl] = outs[p] * scale * on_ref[:, sl]


def _sb_call(qn, kn, z, kpast, vpast, row0, nseq, t, tq, tk, past_len, v_col_blk, out_norm, layer=0):
    nq = t // tq
    blk0 = row0 // tq
    npairs = GW // (2 * HD)
    if kpast is None:
        kp_arr, vp_arr = kn, z
        sblk = row0 // t
        kp_spec = pl.BlockSpec((t, GW), lambda b, i: (sblk + b, 0))
        vp_spec = pl.BlockSpec((t, GW), lambda b, i: (sblk + b, v_col_blk))
        npast, from_grid = 0, True
    else:
        kp_arr, vp_arr = kpast, vpast
        kp_spec = pl.BlockSpec((None, None, past_len, GW), lambda b, i: (layer, b, 0, 0))
        vp_spec = kp_spec
        npast, from_grid = past_len // tk, False
    qrow = lambda b, i: (blk0 + b * nq + i, 0)
    return pl.pallas_call(
        functools.partial(_sb_kernel, tq=tq, tk=tk, npast=npast, past_from_grid=from_grid),
        grid=(nseq, nq),
        in_specs=[pl.BlockSpec((tq, GW), qrow), pl.BlockSpec((tq, GW), qrow),
                  pl.BlockSpec((tq, GW), lambda b, i: (blk0 + b * nq + i, v_col_blk)),
                  kp_spec, vp_spec, pl.BlockSpec((1, GW), lambda b, i: (0, 0))],
        out_specs=pl.BlockSpec((tq, GW), lambda b, i: (b * nq + i, 0)),
        out_shape=jax.ShapeDtypeStruct((nseq * t, GW), F32),
        scratch_shapes=[pltpu.VMEM((npairs, 2 * tq, 2 * HD), BF16), pltpu.VMEM((npairs, 2 * tq, 2 * HD), F32),
                        pltpu.VMEM((npairs, 2 * tq, 2 * HD), F32)],
        compiler_params=_cparams("parallel", "arbitrary"),
        name="sb_attn",
    )(qn, kn, z, kp_arr, vp_arr, out_norm)


def _sbs_kernel(q_ref, kd_ref, vd_ref, kt_ref, vt_ref, on_ref, o_ref, qh_ref, acc_ref, carry_ref,
                *, tq, tk, nblk, nchunks):
    c = pl.program_id(1)
    nh = GW // HD
    npairs = nh // 2
    hs = [slice(h * HD, (h + 1) * HD) for h in range(nh)]
    stack = functools.partial(jnp.concatenate, axis=0)
    mm = functools.partial(lax.dot_general, preferred_element_type=F32)

    @pl.when(c == 0)
    def _():
        row = _iota2(2 * tq, tq, 0)
        causal = _iota2(2 * tq, tq, 1) < jnp.where(row >= tq, row - tq, row)
        ud = jnp.where(_iota2(tq, tq, 0) > _iota2(tq, tq, 1), 1.0, 0.0).astype(BF16)
        qs = [_bf(q_ref[:, s]) for s in hs]
        kd = [_bf(kd_ref[:, s]) for s in hs]
        vd = [_bf(vd_ref[:, s]) for s in hs]
        for h in range(nh):
            qh_ref[h] = qs[h]
        qk = [lambda p=p: stack([mm(qs[2 * p], kd[2 * p], _NT), mm(qs[2 * p + 1], kd[2 * p + 1], _NT)])
              for p in range(npairs)]
        pv = [lambda w, p=p: stack([mm(w[:tq], vd[2 * p], _NN), mm(w[tq:], vd[2 * p + 1], _NN)])
              for p in range(npairs)]
        pvs, tots = _sb_stages(qk, pv, [None] * npairs, ud, causal, skew=0)
        for p in range(npairs):
            acc_ref[p] = pvs[p]
            carry_ref[p] = tots[p]

    up = jnp.where(_iota2(tk, tk, 0) > _iota2(tk, tk, 1), 1.0, 0.0).astype(BF16)
    for jb in reversed(range(nblk)):
        ks = slice(jb * tk, (jb + 1) * tk)
        qk = [lambda p=p: stack([mm(qh_ref[2 * p], _bf(kt_ref[2 * p, :, ks]), _NN),
                                 mm(qh_ref[2 * p + 1], _bf(kt_ref[2 * p + 1, :, ks]), _NN)])
              for p in range(npairs)]
        pv = [lambda w, p=p: stack([mm(w[:tq], _bf(vt_ref[2 * p, :, ks]), _NT),
                                    mm(w[tq:], _bf(vt_ref[2 * p + 1, :, ks]), _NT)])
              for p in range(npairs)]
        pvs, tots = _sb_stages(qk, pv, [carry_ref[p] for p in range(npairs)], up, None, skew=0)
        for p in range(npairs):
            acc_ref[p] += pvs[p]
            carry_ref[p] += tots[p]

    @pl.when(c == nchunks - 1)
    def _():
        heads = []
        for p in range(npairs):
            a = acc_ref[p]
            heads += [a[:tq], a[tq:]]
        o_ref[...] = _rms_rows(jnp.concatenate(heads, axis=1), on_ref[...])


def _sbs_call(qb, kb, vb, cache_kt, cache_vt, row0, nseq, tq, tk, tkc, out_norm, layer):
    past_len = cache_kt.shape[-1]
    nchunks = past_len // tkc
    blk0 = row0 // tq
    nh = GW // HD
    qrow = pl.BlockSpec((tq, GW), lambda b, c: (blk0 + b, 0))
    cspec = pl.BlockSpec((None, None, nh, HD, tkc), lambda b, c: (layer, b, 0, 0, nchunks - 1 - c))
    return pl.pallas_call(
        functools.partial(_sbs_kernel, tq=tq, tk=tk, nblk=tkc // tk, nchunks=nchunks),
        grid=(nseq, nchunks),
        in_specs=[qrow, qrow, qrow, cspec, cspec, pl.BlockSpec((1, GW), lambda b, c: (0, 0))],
        out_specs=pl.BlockSpec((tq, GW), lambda b, c: (b, 0)),
        out_shape=jax.ShapeDtypeStruct((nseq * tq, GW), F32),
        scratch_shapes=[pltpu.VMEM((nh, tq, HD), BF16), pltpu.VMEM((nh // 2, 2 * tq, HD), F32),
                        pltpu.VMEM((nh // 2, 2 * tq, 2 * HD), F32)],
        compiler_params=_cparams("parallel", "arbitrary"),
        name="sb_attn_cache",
    )(qb, kb, vb, cache_kt, cache_vt, out_norm)


def _rwkv_prep_kernel(zr_ref, zk_ref, zv_ref, zwa_ref, zg_ref, first_ref, mu_ref, w0_ref, w2_ref, a0_ref, a2_ref,
                      g2_ref, kk_ref, ka_ref, rk_ref, gmat_ref,
                      r_out, lw_out, k_out, v_out, kk_out, b_out, g_out, bonus_out, prev_ref, *, tm, wr):
    c = pl.program_id(1)
    cols = prev_ref.shape[1]

    @pl.when(c == 0)
    def _():
        prev_ref[...] = first_ref[0]

    first_row = _iota2(tm, 1, 0) == 0

    def shifted(ref, lo, width):
        x = ref[...]
        prev = jnp.where(first_row, prev_ref[:, lo:lo + width], pltpu.roll(x, 1, 0))
        return x + (prev - x) * mu_ref[:, lo:lo + width]

    r = shifted(zr_ref, 0, GW)
    k = shifted(zk_ref, GW, GW)
    v = shifted(zv_ref, 2 * GW, GW)
    wa = shifted(zwa_ref, 3 * GW, 2 * wr)
    gl = shifted(zg_ref, 3 * GW + 2 * wr, cols - 3 * GW - 2 * wr)
    for ref, lo in ((zr_ref, 0), (zk_ref, GW), (zv_ref, 2 * GW), (zwa_ref, 3 * GW), (zg_ref, 3 * GW + 2 * wr)):
        prev_ref[:, lo:lo + ref.shape[1]] = ref[tm - 1:tm, :]

    xw = w0_ref[...] + _dot(jnp.tanh(wa), w2_ref[...])
    w_log = -(jnp.maximum(-xw, 0.0) + jnp.log(1.0 + jnp.exp(-jnp.abs(xw)))) - 0.5
    a = jax.nn.sigmoid(a0_ref[...] + _dot(wa, a2_ref[...]))
    g_out[...] = _dot(jax.nn.sigmoid(gl), g2_ref[...])
    gmat = gmat_ref[...]
    kk = k * kk_ref[...]
    kk = kk / jnp.maximum(jnp.sqrt(_dot2_exact_rhs(kk * kk, gmat)), 1e-12)
    k = k * (1.0 + (a - 1.0) * ka_ref[...])
    r_out[...] = r
    lw_out[...] = -jnp.exp(w_log)
    k_out[...] = k
    v_out[...] = v
    kk_out[...] = kk
    b_out[...] = kk * a
    bonus_out[...] = _dot2_exact_rhs(r * k * rk_ref[...], gmat) * v


def _rwkv_prep_call(z, row0, nseq, t, tm, first, mu, w0, w2p, a0, a2p, g2, k_k, k_a, r_k, gmat, wr):
    nct = t // tm
    blk0 = row0 // tm
    cols = mu.shape[-1]
    zrow = lambda blkw, off: pl.BlockSpec((tm, blkw), lambda b, c: (blk0 + b * nct + c, off))
    c0 = (z.shape[1] - cols)
    const = lambda shp: pl.BlockSpec(shp, lambda b, c: (0,) * len(shp))
    orow = pl.BlockSpec((tm, GW), lambda b, c: (b * nct + c, 0))
    return pl.pallas_call(
        functools.partial(_rwkv_prep_kernel, tm=tm, wr=wr),
        grid=(nseq, nct),
        in_specs=[zrow(GW, c0 // GW), zrow(GW, c0 // GW + 1), zrow(GW, c0 // GW + 2),
                  zrow(2 * wr, (c0 + 3 * GW) // (2 * wr)), zrow(cols - 3 * GW - 2 * wr, (c0 + 3 * GW) // (2 * wr) + 1),
                  pl.BlockSpec((1, 1, cols), lambda b, c: (b, 0, 0)),
                  const((1, cols)), const((1, GW)), const((2 * wr, GW)), const((1, GW)), const((2 * wr, GW)),
                  const((cols - 3 * GW - 2 * wr, GW)), const((1, GW)), const((1, GW)), const((1, GW)),
                  const((GW, GW))],
        out_specs=[orow] * 8,
        out_shape=[jax.ShapeDtypeStruct((nseq * t, GW), F32)] * 8,
        scratch_shapes=[pltpu.VMEM((1, cols), F32)],
        compiler_params=_cparams("parallel", "arbitrary"),
        name="rwkv_prep",
    )(z, z, z, z, z, first.reshape(nseq, 1, cols), mu, w0, w2p, a0, a2p, g2, k_k, k_a, r_k, gmat)


def _hat(x, m0):
    return jnp.concatenate([jnp.where(m0, x, 0.0), jnp.where(m0, 0.0, x)], axis=0)


def _rwkv_chunk_kernel(r_ref, lw_ref, k_ref, v_ref, kk_ref, b_ref, rt_out, p3_out, m_out, n_out, *, group):
    C = RWKV_CHUNK
    C2 = 2 * C
    lane = _iota2(1, 2 * HD, 1)
    m0 = lane < HD
    ri = _iota2(C2, C2, 0)
    ci = _iota2(C2, C2, 1)
    same = (ri < C) == (ci < C)
    strict = jnp.logical_and(same, ri > ci)
    incl = jnp.logical_and(same, ri >= ci)
    ltri = jnp.where(_iota2(C, C, 0) >= _iota2(C, C, 1), 1.0, 0.0).astype(BF16)
    eye = _iota2(2 * HD, 2 * HD, 0) == _iota2(2 * HD, 2 * HD, 1)
    eye2 = ri == ci

    def pair_stages(p):
        sl = slice(p * 2 * HD, (p + 1) * 2 * HD)
        lw = lw_ref[:, sl]
        cl = _dot2_exact_lhs(ltri, lw)
        yield
        clast = cl[C - 1:C, :]
        kkt = _hat(kk_ref[:, sl] * jnp.exp(cl - lw), m0)
        rt = _hat(r_ref[:, sl] * jnp.exp(cl), m0)
        einv = jnp.exp(-cl)
        kb = _split(_hat(k_ref[:, sl] * einv, m0))
        bb = _split(_hat(b_ref[:, sl] * einv, m0))
        efin = jnp.exp(clast - cl)
        kh = _split(_hat(k_ref[:, sl] * efin, m0))
        bh = _split(_hat(b_ref[:, sl] * efin, m0))
        vh = _split(_hat(v_ref[:, sl], m0))
        lhs = _split(jnp.concatenate([kkt, rt], axis=0))
        gk = _dot3s(lhs, kb, _NT)
        gb = _dot3s(lhs, bb, _NT)
        yield
        a_kk = jnp.where(strict, gk[:C2], 0.0)
        a_rk = _split(jnp.where(incl, gk[C2:], 0.0))
        a_kb = jnp.where(strict, gb[:C2], 0.0)
        a_rb = _split(jnp.where(incl, gb[C2:], 0.0))
        p1 = _dot3s(_split(a_kk), vh)
        tinv = jnp.where(eye2, 1.0, 0.0) - a_kb
        lp = _split(a_kb)
        n = 2
        while n < C:
            yield
            lpf = _dot3s(lp, lp)
            lp = _split(lpf)
            tinv = tinv + _dot3s(_split(tinv), lp)
            n *= 2
        yield
        x = _dot3s(_split(tinv), _split(jnp.concatenate([kkt, p1], axis=1)))
        yield
        xs = _split(x)
        kt = (xs[0][:, :2 * HD], xs[1][:, :2 * HD])
        p2 = (xs[0][:, 2 * HD:], xs[1][:, 2 * HD:])
        y2 = _dot3s(a_rb, xs)
        p3 = _dot3s(a_rk, vh)
        mm = _dot3s(kt, bh, _TN)
        nn = _dot3s(vh, kh, _TN) - _dot3s(p2, bh, _TN)
        yield
        rt_out[p] = rt - y2[:, :2 * HD]
        p3_out[p] = p3 - y2[:, 2 * HD:]
        m_out[p] = jnp.where(eye, jnp.exp(clast), 0.0) - mm
        n_out[p] = nn

    npairs = GW // (2 * HD)
    for p0 in range(0, npairs, group):
        _round_robin([pair_stages(p) for p in range(p0, p0 + group)])


def _rwkv_chunk_call(r, lw, k, v, kk, b):
    n = r.shape[0]
    C = RWKV_CHUNK
    nc = n // C
    npairs = GW // (2 * HD)
    row = pl.BlockSpec((C, GW), lambda i: (i, 0))
    blk = lambda rows: pl.BlockSpec((None, npairs, rows, 2 * HD), lambda i: (i, 0, 0, 0))
    shp = lambda rows: jax.ShapeDtypeStruct((nc, npairs, rows, 2 * HD), F32)
    return pl.pallas_call(
        functools.partial(_rwkv_chunk_kernel, group=4),
        grid=(nc,),
        in_specs=[row] * 6,
        out_specs=[blk(2 * C), blk(2 * C), blk(2 * HD), blk(2 * HD)],
        out_shape=[shp(2 * C), shp(2 * C), shp(2 * HD), shp(2 * HD)],
        compiler_params=_cparams("parallel"),
        name="rwkv_chunk",
    )(r, lw, k, v, kk, b)


def _rwkv_seq_kernel(rt_ref, p3_ref, m_ref, n_ref, s0_ref, g_ref, bonus_ref, lnw_ref, lnb_ref, gmat_ref,
                     o_ref, s_out_ref, s_ref, *, nct, gsz):
    c = pl.program_id(1)
    C = RWKV_CHUNK

    @pl.when(c == 0)
    def _():
        s_ref[...] = s0_ref[...]

    gm = gmat_ref[0:2 * HD, 0:2 * HD]
    npairs = GW // (2 * HD)
    pairs = [slice(p * 2 * HD, (p + 1) * 2 * HD) for p in range(npairs)]
    sp = [(s, p) for s in range(gsz) for p in range(npairs)]
    ss = [_split(s_ref[s, p]) for s, p in sp]
    yhs = [_dot3s(_split(rt_ref[s, p]), ss[i], _NT) + p3_ref[s, p] for i, (s, p) in enumerate(sp)]
    snew = [_dot3s(ss[i], _split(m_ref[s, p])) + n_ref[s, p] for i, (s, p) in enumerate(sp)]
    ys = [yh[:C] + yh[C:] for yh in yhs]
    means = [_dot2_exact_rhs(y, gm) * (1.0 / HD) for y in ys]
    ds = [y - m for y, m in zip(ys, means)]
    vrs = [_dot2_exact_rhs(d * d, gm) * (1.0 / HD) for d in ds]
    for i, (s, p) in enumerate(sp):
        sl = pairs[p]
        yn = ds[i] * lax.rsqrt(vrs[i] + GN_EPS) * lnw_ref[:, sl] + lnb_ref[:, sl]
        o_ref[s, :, sl] = (yn + bonus_ref[s, :, sl]) * g_ref[s, :, sl]
        s_ref[s, p] = snew[i]

    @pl.when(c == nct - 1)
    def _():
        s_out_ref[...] = s_ref[...]


def _rwkv_seq_call(rt, p3, m, n, nseq, nct, gsz, s0, g, bonus, lnx_w, lnx_b, gmat):
    C = RWKV_CHUNK
    npairs = GW // (2 * HD)
    c5 = lambda a: a.reshape(nseq, nct, *a.shape[1:])
    cblk = lambda rows: pl.BlockSpec((gsz, None, npairs, rows, 2 * HD), lambda b, c: (b, c, 0, 0, 0))
    sblk = pl.BlockSpec((gsz, npairs, 2 * HD, 2 * HD), lambda b, c: (b, 0, 0, 0))
    rowblk = pl.BlockSpec((gsz, C, GW), lambda b, c: (b, c, 0))
    const = lambda shp: pl.BlockSpec(shp, lambda b, c: (0,) * len(shp))
    od, s_out = pl.pallas_call(
        functools.partial(_rwkv_seq_kernel, nct=nct, gsz=gsz),
        grid=(nseq // gsz, nct),
        in_specs=[cblk(2 * C), cblk(2 * C), cblk(2 * HD), cblk(2 * HD), sblk, rowblk, rowblk,
                  const((1, GW)), const((1, GW)), const((GW, GW))],
        out_specs=[rowblk, sblk],
        out_shape=[jax.ShapeDtypeStruct((nseq, nct * C, GW), F32),
                   jax.ShapeDtypeStruct((nseq, npairs, 2 * HD, 2 * HD), F32)],
        scratch_shapes=[pltpu.VMEM((gsz, npairs, 2 * HD, 2 * HD), F32)],
        compiler_params=_cparams("parallel", "arbitrary"),
        name="rwkv_seq",
    )(c5(rt), c5(p3), c5(m), c5(n), s0, g.reshape(nseq, nct * C, GW), bonus.reshape(nseq, nct * C, GW),
      lnx_w, lnx_b, gmat)
    return od.reshape(nseq * nct * C, GW), s_out


def _pair_states(s):
    b, h = s.shape[:2]
    s = s.reshape(b, h // 2, 2, HD, HD)
    z = jnp.zeros_like(s[:, :, 0])
    top = jnp.concatenate([s[:, :, 0], z], axis=-1)
    bot = jnp.concatenate([z, s[:, :, 1]], axis=-1)
    return jnp.concatenate([top, bot], axis=-2)


def _unpair_states(s2):
    b, hp = s2.shape[:2]
    return jnp.stack([s2[:, :, :HD, :HD], s2[:, :, HD:, HD:]], axis=2).reshape(b, 2 * hp, HD, HD)


def _forward(xp, xs, cache_k, cache_v, s5_re0, s5_im0, wkv0, shift0, W):
    Bp, Tp, D = xp.shape
    Bs, Ts, _ = xs.shape
    depth = W['ln_ffn1'].shape[0]
    past_len = cache_k.shape[2]
    Np, Ns = Bp * Tp, Bs * Ts
    n_heads_d = GW // HD
    wr = W['rwkv_w2'].shape[1]
    dcols = W['rwkv_mu'].shape[-1]

    x = jnp.concatenate([xp.reshape(Np, D), xs.reshape(Ns, D)], axis=0)
    cache_kt = jnp.transpose(cache_k, (0, 1, 3, 4, 2))
    cache_vt = jnp.transpose(cache_v, (0, 1, 3, 4, 2))
    wbf = {k: _bf(W[k]) for k in ('w_ffn1_gate', 'w_ffn1_up', 'w_ffn1_down', 'w_in', 'w_out',
                                  'w_ffn2_gate', 'w_ffn2_up', 'w_ffn2_down')}
    hid = lax.broadcasted_iota(jnp.int32, (GW, GW), 0) // HD
    gmat = (hid == jnp.transpose(hid)).astype(BF16)
    grp = lax.broadcasted_iota(jnp.int32, (GW, S5_STATE), 0) // (GW // 32) == \
        lax.broadcasted_iota(jnp.int32, (GW, S5_STATE), 1) // (S5_STATE // 32)
    zeros_first = jnp.zeros((Bp, dcols), F32)
    zeros_s5 = jnp.zeros((Bp, S5_STATE), F32)
    zeros_wkv = jnp.zeros((Bp, n_heads_d // 2, 2 * HD, 2 * HD), F32)

    h = _rms_call(x, W['ln_ffn1'][0][None])
    outs = {k: [] for k in ('ks', 'vs', 's5rp', 's5ip', 's5rs', 's5is', 'wkvp', 'wkvs', 'shp', 'shs', 'gv')}
    kt_buf = vt_buf = None
    for l in range(depth):
        g1 = lambda name: W[name][l][None]
        x, h = _ffn_call(x, h, wbf['w_ffn1_gate'], wbf['w_ffn1_up'], wbf['w_ffn1_down'], g1('ln_mix'), l)
        z = _inproj_call(h, wbf['w_in'], l)

        oa_p, _ = _gmlp_call(z, 0, Np, min(Tp, GMLP_CHUNK), g1('gmlp_v_norm'), W['gmlp_ws'][l], W['gmlp_b'][l],
                             g1('out_norm_a'))
        oa_s, gv_s = _gmlp_call(z, Np, Ns, min(Ts, GMLP_CHUNK), g1('gmlp_v_norm'), W['gmlp_ws'][l], W['gmlp_b'][l],
                                g1('out_norm_a'))

        pw_re, pw_im, bb_re, bb_im = _s5_params(W['s5_lam_re'][l], W['s5_lam_im'][l], W['s5_log_dt'][l],
                                                W['s5_b_re'][l], W['s5_b_im'][l])
        bblk = jnp.concatenate([jnp.where(grp, jnp.tile(bb_re, (32, 1)), 0.0),
                                jnp.where(grp, jnp.tile(bb_im, (32, 1)), 0.0)], axis=1).astype(BF16)
        grp_t = jnp.transpose(grp)
        cre = jnp.where(grp_t, jnp.tile(jnp.transpose(W['s5_c_re'][l], (0, 2, 1)).reshape(S5_STATE, -1), (1, 32)),
                        0.0).astype(BF16)
        cim = jnp.where(grp_t, jnp.tile(jnp.transpose(W['s5_c_im'][l], (0, 2, 1)).reshape(S5_STATE, -1), (1, 32)),
                        0.0).astype(BF16)
        s5_args = (pw_re, pw_im, bblk, cre, cim, g1('s5_d'), W['s5_w_glu'][l], g1('s5_b_glu'), g1('out_norm_b'))
        ob_p, s5rp, s5ip = _s5_call(z, 0, Bp, Tp, zeros_s5, zeros_s5, *s5_args, col_blk=2, tc=min(Tp, 256))
        ob_s, s5rs, s5is = _s5_call(z, Np, Bs, Ts, s5_re0[l].reshape(Bs, S5_STATE), s5_im0[l].reshape(Bs, S5_STATE),
                                    *s5_args, col_blk=2, tc=min(Ts, 256))

        qg = jnp.tile(W['sb_q_norm'][l], GW // HD)[None]
        kg = jnp.tile(W['sb_k_norm'][l], GW // HD)[None]
        qb_p, kb_p, vb_p, kt_buf, vt_buf = _sbprep_call(z, 0, Bp, Tp, gmat, qg, kg, l, depth, kt_buf, vt_buf)
        qb_s, kb_s, vb_s, kn_s = _sbprep_call(z, Np, Bs, Ts, gmat, qg, kg)
        oc_p = _sb_call(qb_p, kb_p, vb_p, None, None, 0, Bp, Tp, min(Tp, SB_TQ), min(Tp, SB_TK), 0, 0,
                        g1('out_norm_c'))
        oc_s = _sbs_call(qb_s, kb_s, vb_s, cache_kt, cache_vt, 0, Bs, Ts, min(past_len, SB_TK_SAMPLE),
                         min(past_len, SB_CACHE_CHUNK), g1('out_norm_c'), l)

        w2p = jnp.concatenate([W['rwkv_w2'][l], jnp.zeros_like(W['rwkv_a2'][l])], axis=0)
        a2p = jnp.concatenate([jnp.zeros_like(W['rwkv_w2'][l]), W['rwkv_a2'][l]], axis=0)
        rk = W['rwkv_r_k'][l].reshape(1, GW)
        prep_args = (g1('rwkv_mu'), g1('rwkv_w0'), w2p, g1('rwkv_a0'), a2p, W['rwkv_g2'][l], g1('rwkv_k_k'),
                     g1('rwkv_k_a'), rk, gmat, wr)
        pp = _rwkv_prep_call(z, 0, Bp, Tp, min(Tp, 512), zeros_first, *prep_args)
        ps = _rwkv_prep_call(z, Np, Bs, Ts, Ts, shift0[l], *prep_args)
        seq_args = (g1('rwkv_lnx_w'), g1('rwkv_lnx_b'), gmat)
        od_p, wkv_p = _rwkv_seq_call(*_rwkv_chunk_call(*pp[:6]), Bp, Tp // RWKV_CHUNK, min(Bp, 2), zeros_wkv,
                                     pp[6], pp[7], *seq_args)
        od_s, wkv_s = _rwkv_seq_call(*_rwkv_chunk_call(*ps[:6]), Bs, Ts // RWKV_CHUNK, min(Bs, 4),
                                     _pair_states(wkv0[l]), ps[6], ps[7], *seq_args)

        gain_next = g1('ln_ffn2')
        x, h = _outproj_call(x, (oa_p, ob_p, oc_p, od_p), (oa_s, ob_s, oc_s, od_s), wbf['w_out'], gain_next, l)
        gain_next = W['ln_ffn1'][l + 1][None] if l + 1 < depth else g1('ln_ffn2')
        x, h = _ffn_call(x, h, wbf['w_ffn2_gate'], wbf['w_ffn2_up'], wbf['w_ffn2_down'], gain_next, l)

        nh = GW // HD
        outs['ks'].append(kn_s.reshape(Bs, Ts, nh, HD))
        outs['vs'].append(z[Np:, 5 * GW:6 * GW].reshape(Bs, Ts, nh, HD))
        outs['s5rp'].append(s5rp.reshape(Bp, 32, -1))
        outs['s5ip'].append(s5ip.reshape(Bp, 32, -1))
        outs['s5rs'].append(s5rs.reshape(Bs, 32, -1))
        outs['s5is'].append(s5is.reshape(Bs, 32, -1))
        outs['wkvp'].append(_unpair_states(wkv_p))
        outs['wkvs'].append(_unpair_states(wkv_s))
        zd = z[:, z.shape[1] - dcols:]
        outs['shp'].append(zd[:Np].reshape(Bp, Tp, dcols)[:, -1])
        outs['shs'].append(zd[Np:].reshape(Bs, Ts, dcols)[:, -1])
        outs['gv'].append(gv_s.reshape(Bs, Ts, GW))

    st = lambda k: jnp.stack(outs[k], axis=0)
    untr = lambda a: jnp.transpose(a.reshape(depth, Bp, GW // HD, HD, Tp), (0, 1, 4, 2, 3))
    return (x[:Np].reshape(Bp, Tp, D), x[Np:].reshape(Bs, Ts, D),
            untr(kt_buf), untr(vt_buf), st('ks'), st('vs'),
            st('s5rp'), st('s5ip'), st('s5rs'), st('s5is'),
            st('wkvp'), st('wkvs'), st('shp'), st('shs'), st('gv'))


_WEIGHT_NAMES = ('ln_ffn1', 'w_ffn1_gate', 'w_ffn1_up', 'w_ffn1_down', 'ln_mix', 'w_in',
                 'gmlp_v_norm', 'gmlp_ws', 'gmlp_b', 'out_norm_a',
                 's5_lam_re', 's5_lam_im', 's5_log_dt', 's5_b_re', 's5_b_im', 's5_c_re', 's5_c_im', 's5_d',
                 's5_w_glu', 's5_b_glu', 'out_norm_b',
                 'sb_q_norm', 'sb_k_norm', 'out_norm_c',
                 'rwkv_mu', 'rwkv_w0', 'rwkv_w2', 'rwkv_a0', 'rwkv_a2', 'rwkv_g2', 'rwkv_k_k', 'rwkv_k_a', 'rwkv_r_k',
                 'rwkv_lnx_w', 'rwkv_lnx_b',
                 'w_out', 'ln_ffn2', 'w_ffn2_gate', 'w_ffn2_up', 'w_ffn2_down')


def kernel(x_prompt, x_sample, cache_sb_k, cache_sb_v, state_s5_re, state_s5_im, state_rwkv_wkv, state_rwkv_shift,
           ln_ffn1, w_ffn1_gate, w_ffn1_up, w_ffn1_down, ln_mix, w_in,
           gmlp_v_norm, gmlp_ws, gmlp_b, out_norm_a,
           s5_lam_re, s5_lam_im, s5_log_dt, s5_b_re, s5_b_im, s5_c_re, s5_c_im, s5_d, s5_w_glu, s5_b_glu, out_norm_b,
           sb_q_norm, sb_k_norm, out_norm_c,
           rwkv_mu, rwkv_w0, rwkv_w2, rwkv_a0, rwkv_a2, rwkv_g2, rwkv_k_k, rwkv_k_a, rwkv_r_k, rwkv_lnx_w, rwkv_lnx_b,
           w_out, ln_ffn2, w_ffn2_gate, w_ffn2_up, w_ffn2_down):
    weights = (ln_ffn1, w_ffn1_gate, w_ffn1_up, w_ffn1_down, ln_mix, w_in,
               gmlp_v_norm, gmlp_ws, gmlp_b, out_norm_a,
               s5_lam_re, s5_lam_im, s5_log_dt, s5_b_re, s5_b_im, s5_c_re, s5_c_im, s5_d, s5_w_glu, s5_b_glu,
               out_norm_b, sb_q_norm, sb_k_norm, out_norm_c,
               rwkv_mu, rwkv_w0, rwkv_w2, rwkv_a0, rwkv_a2, rwkv_g2, rwkv_k_k, rwkv_k_a, rwkv_r_k, rwkv_lnx_w,
               rwkv_lnx_b, w_out, ln_ffn2, w_ffn2_gate, w_ffn2_up, w_ffn2_down)
    W = dict(zip(_WEIGHT_NAMES, weights))
    return _forward(x_prompt, x_sample, cache_sb_k, cache_sb_v, state_s5_re, state_s5_im, state_rwkv_wkv,
                    state_rwkv_shift, W)
```

```python
import functools

import jax
import jax.numpy as jnp
from jax import lax
from jax.experimental import pallas as pl
from jax.experimental.pallas import tpu as pltpu

F32 = jnp.float32
BF16 = jnp.bfloat16

RMS_EPS = 1e-6
GN_EPS = 64e-5
A_HEADS = 4
GMLP_CHUNK = 128
HD = 64
GW = 512
S5_STATE = 2048
S5_TC = 256
S5_POWERS = 8 + S5_TC // 8
RWKV_CHUNK = 64
SB_TQ, SB_TK = 256, 256
SB_TK_SAMPLE = 256
SB_CACHE_CHUNK = 1024
FFN_TM = 768
FFN_VMEM_LIMIT = 60 * 1024 * 1024
VMEM_LIMIT = 56 * 1024 * 1024


def _cparams(*sem, vmem=VMEM_LIMIT):
    return pltpu.CompilerParams(dimension_semantics=sem, vmem_limit_bytes=vmem)


def _bf(x):
    return x.astype(BF16)


_NN = (((1,), (0,)), ((), ()))
_NT = (((1,), (1,)), ((), ()))
_TN = (((0,), (0,)), ((), ()))


def _dot(a, b, dims=_NN):
    return lax.dot_general(_bf(a), _bf(b), dims, preferred_element_type=F32)


def _split(x):
    hi = _bf(x)
    lo = _bf(x - hi.astype(F32))
    return hi, lo


def _dot3(a, b, dims=_NN):
    ah, al = _split(a)
    bh, bl = _split(b)
    d = functools.partial(lax.dot_general, dimension_numbers=dims, preferred_element_type=F32)
    return d(ah, bh) + (d(ah, bl) + d(al, bh))


def _dot3s(a, b, dims=_NN):
    d = functools.partial(lax.dot_general, dimension_numbers=dims, preferred_element_type=F32)
    return d(a[0], b[0]) + d(a[1], b[0])


def _dot2_exact_rhs(a, b_bf, dims=_NN):
    ah, al = _split(a)
    d = functools.partial(lax.dot_general, dimension_numbers=dims, preferred_element_type=F32)
    return d(ah, b_bf) + d(al, b_bf)


def _dot2_exact_lhs(a_bf, b, dims=_NN):
    bh, bl = _split(b)
    d = functools.partial(lax.dot_general, dimension_numbers=dims, preferred_element_type=F32)
    return d(a_bf, bh) + d(a_bf, bl)


def _round_robin(gens, skew=0):
    done = [False] * len(gens)
    tick = 0
    while not all(done):
        for i, g in enumerate(gens):
            if not done[i] and tick >= i * skew:
                try:
                    next(g)
                except StopIteration:
                    done[i] = True
        tick += 1


def _rms_rows(x, gain):
    ms = jnp.mean(x * x, axis=-1, keepdims=True)
    return x * lax.rsqrt(ms + RMS_EPS) * gain


def _iota2(n, m, axis):
    return lax.broadcasted_iota(jnp.int32, (n, m), axis)


def _rms_kernel(x_ref, g_ref, h_ref):
    h_ref[...] = _bf(_rms_rows(x_ref[...], g_ref[...]))


def _rms_call(x, gain, tm=512):
    n, d = x.shape
    return pl.pallas_call(
        _rms_kernel,
        grid=(n // tm,),
        in_specs=[pl.BlockSpec((tm, d), lambda i: (i, 0)), pl.BlockSpec((1, d), lambda i: (0, 0))],
        out_specs=pl.BlockSpec((tm, d), lambda i: (i, 0)),
        out_shape=jax.ShapeDtypeStruct((n, d), BF16),
        compiler_params=_cparams("parallel"),
        name="rms",
    )(x, gain)


def _ffn_kernel(x_ref, h_ref, wg_ref, wu_ref, wd_ref, gn_ref, o_ref, hn_ref, *, nj):
    j = pl.program_id(1)

    @pl.when(j == 0)
    def _():
        o_ref[...] = jnp.zeros_like(o_ref)

    h = h_ref[...]
    g = jnp.dot(h, wg_ref[...], preferred_element_type=F32)
    u = jnp.dot(h, wu_ref[...], preferred_element_type=F32)
    a = _bf(g * jax.nn.sigmoid(g) * u)
    o_ref[...] += jnp.dot(a, wd_ref[...], preferred_element_type=F32)

    @pl.when(j == nj - 1)
    def _():
        y = x_ref[...] + 0.5 * o_ref[...]
        o_ref[...] = y
        hn_ref[...] = _bf(_rms_rows(y, gn_ref[...]))


def _ffn_call(x, h, wg, wu, wd, gain_next, layer, tf=512):
    n, d = x.shape
    ff = wg.shape[-1]
    nj = ff // tf
    tm = FFN_TM if n % FFN_TM == 0 else 512
    return pl.pallas_call(
        functools.partial(_ffn_kernel, nj=nj),
        grid=(n // tm, nj),
        in_specs=[
            pl.BlockSpec((tm, d), lambda i, j: (i, 0)),
            pl.BlockSpec((tm, d), lambda i, j: (i, 0)),
            pl.BlockSpec((None, d, tf), lambda i, j: (layer, 0, j)),
            pl.BlockSpec((None, d, tf), lambda i, j: (layer, 0, j)),
            pl.BlockSpec((None, tf, d), lambda i, j: (layer, j, 0)),
            pl.BlockSpec((1, d), lambda i, j: (0, 0)),
        ],
        out_specs=[pl.BlockSpec((tm, d), lambda i, j: (i, 0)), pl.BlockSpec((tm, d), lambda i, j: (i, 0))],
        out_shape=[jax.ShapeDtypeStruct((n, d), F32), jax.ShapeDtypeStruct((n, d), BF16)],
        compiler_params=_cparams("parallel", "arbitrary", vmem=FFN_VMEM_LIMIT),
        name="ffn",
    )(x, h, wg, wu, wd, gain_next)


def _inproj_kernel(h_ref, w_ref, z_ref):
    z_ref[...] = jnp.dot(h_ref[...], w_ref[...], preferred_element_type=F32)


def _inproj_call(h, w_in, layer, tm=512):
    n, d = h.shape
    cols = w_in.shape[-1]
    tn = cols // 2
    return pl.pallas_call(
        _inproj_kernel,
        grid=(2, n // tm),
        in_specs=[pl.BlockSpec((tm, d), lambda j, i: (i, 0)),
                  pl.BlockSpec((None, d, tn), lambda j, i: (layer, 0, j))],
        out_specs=pl.BlockSpec((tm, tn), lambda j, i: (i, j)),
        out_shape=jax.ShapeDtypeStruct((n, cols), F32),
        compiler_params=_cparams("parallel", "parallel"),
        name="inproj",
    )(h, w_in)


def _outproj_kernel(x_ref, *refs, n_first):
    first, second = refs[0:4], refs[4:8]
    w_ref, gn_ref, o_ref, hn_ref = refs[8:]

    def run(mix_refs):
        acc = x_ref[...]
        for i, r in enumerate(mix_refs):
            acc = acc + jnp.dot(_bf(r[...]), w_ref[i * GW:(i + 1) * GW, :], preferred_element_type=F32)
        o_ref[...] = acc
        hn_ref[...] = _bf(_rms_rows(acc, gn_ref[...]))

    i = pl.program_id(0)
    pl.when(i < n_first)(lambda: run(first))
    pl.when(i >= n_first)(lambda: run(second))


def _outproj_call(x, mix_first, mix_second, w_out, gain_next, layer, tm=256):
    n, d = x.shape
    n_first = mix_first[0].shape[0] // tm
    n_second = mix_second[0].shape[0] // tm
    row = lambda i: (i, 0)
    first_row = lambda i: (jnp.minimum(i, n_first - 1), 0)
    second_row = lambda i: (jnp.maximum(i - n_first, 0), 0)
    return pl.pallas_call(
        functools.partial(_outproj_kernel, n_first=n_first),
        grid=(n_first + n_second,),
        in_specs=[pl.BlockSpec((tm, d), row)] + [pl.BlockSpec((tm, GW), first_row)] * 4
        + [pl.BlockSpec((tm, GW), second_row)] * 4
        + [pl.BlockSpec((None, d, d), lambda i: (layer, 0, 0)), pl.BlockSpec((1, d), lambda i: (0, 0))],
        out_specs=[pl.BlockSpec((tm, d), row), pl.BlockSpec((tm, d), row)],
        out_shape=[jax.ShapeDtypeStruct((n, d), F32), jax.ShapeDtypeStruct((n, d), BF16)],
        compiler_params=_cparams("parallel"),
        name="outproj",
    )(x, *mix_first, *mix_second, w_out, gain_next)


def _gmlp_kernel(z_ref, vn_ref, ws_ref, bt_ref, on_ref, o_ref, v_ref, *, L, nch):
    causal = _iota2(L, L, 1) <= _iota2(L, L, 0)
    hw = GW // A_HEADS
    ws = [_bf(jnp.where(causal, ws_ref[h], 0.0)) for h in range(A_HEADS)]

    def chunk_stages(c):
        rows = slice(c * L, (c + 1) * L)
        z = jax.nn.gelu(z_ref[rows, :])
        vhs = [_rms_rows(z[:, GW + h * hw:GW + (h + 1) * hw], vn_ref[...]) for h in range(A_HEADS)]
        for h in range(A_HEADS):
            v_ref[rows, h * hw:(h + 1) * hw] = vhs[h]
        yield
        ss = [_dot(ws[h], vhs[h]) for h in range(A_HEADS)]
        yield
        outs = [z[:, h * hw:(h + 1) * hw] * (ss[h] + bt_ref[:, h:h + 1]) for h in range(A_HEADS)]
        sq = sum(jnp.sum(o * o, axis=-1, keepdims=True) for o in outs)
        scale = lax.rsqrt(sq * (1.0 / GW) + RMS_EPS)
        for h in range(A_HEADS):
            o_ref[rows, h * hw:(h + 1) * hw] = outs[h] * scale * on_ref[:, h * hw:(h + 1) * hw]

    _round_robin([chunk_stages(c) for c in range(nch)])


def _gmlp_call(z, row0, nrows, L, v_norm, ws, b, out_norm, nch=4):
    tm = nch * L
    nb = nrows // tm
    b0 = row0 // tm
    ws_l = ws[:, :L, :L]
    bt = jnp.transpose(b[:, :L])
    row = lambda i: (i, 0)
    return pl.pallas_call(
        functools.partial(_gmlp_kernel, L=L, nch=nch),
        grid=(nb,),
        in_specs=[pl.BlockSpec((tm, 2 * GW), lambda i: (b0 + i, 0)),
                  pl.BlockSpec((1, GW // A_HEADS), lambda i: (0, 0)),
                  pl.BlockSpec((A_HEADS, L, L), lambda i: (0, 0, 0)),
                  pl.BlockSpec((L, A_HEADS), lambda i: (0, 0)),
                  pl.BlockSpec((1, GW), lambda i: (0, 0))],
        out_specs=[pl.BlockSpec((tm, GW), row), pl.BlockSpec((tm, GW), row)],
        out_shape=[jax.ShapeDtypeStruct((nrows, GW), F32), jax.ShapeDtypeStruct((nrows, GW), F32)],
        compiler_params=_cparams("parallel"),
        name="gmlp",
    )(z, v_norm, ws_l, bt, out_norm)


def _s5_param_kernel(lr_ref, li_ref, ldt_ref, brt_ref, bit_ref, pw_re_ref, pw_im_ref, bb_re_ref, bb_im_ref):
    lr = lr_ref[...]
    li = li_ref[...]
    dt = jnp.exp(ldt_ref[...])
    row = _iota2(S5_POWERS, S5_STATE, 0)
    n = jnp.where(row < 8, row + 1, 8 * (row - 7)).astype(F32)
    mag = jnp.exp(n * (lr * dt))
    ang = n * (li * dt)
    pw_re = mag * jnp.cos(ang)
    pw_im = mag * jnp.sin(ang)
    pw_re_ref[...] = pw_re
    pw_im_ref[...] = pw_im
    ab_re = pw_re[0:1, :]
    ab_im = pw_im[0:1, :]
    den = lr * lr + li * li
    nr, ni = ab_re - 1.0, ab_im
    cf_re = (nr * lr + ni * li) / den
    cf_im = (ni * lr - nr * li) / den
    br = brt_ref[...]
    bi = bit_ref[...]
    bb_re_ref[...] = cf_re * br - cf_im * bi
    bb_im_ref[...] = cf_re * bi + cf_im * br


def _s5_params(lam_re, lam_im, log_dt, b_re, b_im):
    g, p = lam_re.shape
    s = g * p
    ch = b_re.shape[-1]
    ldt = jnp.broadcast_to(log_dt[:, None], (g, p)).reshape(1, s)
    brt = jnp.transpose(b_re.reshape(s, ch))
    bit = jnp.transpose(b_im.reshape(s, ch))
    return pl.pallas_call(
        _s5_param_kernel,
        out_shape=[jax.ShapeDtypeStruct((S5_POWERS, s), F32), jax.ShapeDtypeStruct((S5_POWERS, s), F32),
                   jax.ShapeDtypeStruct((ch, s), F32), jax.ShapeDtypeStruct((ch, s), F32)],
        name="s5_params",
    )(lam_re.reshape(1, s), lam_im.reshape(1, s), ldt, brt, bit)


def _cmul(ar, ai, xr, xi):
    return ar * xr - ai * xi, ar * xi + ai * xr


def _s5_kernel(u_ref, h0r_ref, h0i_ref, pwr_ref, pwi_ref, bblk_ref, cre_ref, cim_ref, d_ref, wglu_ref, bglu_ref,
               on_ref, o_ref, hr_out_ref, hi_out_ref, bu_ref, hre_ref, him_ref, cr_ref, ci_ref, *, tc, nct):
    c = pl.program_id(1)
    S = S5_STATE
    LW = 128

    @pl.when(c == 0)
    def _():
        cr_ref[...] = h0r_ref[0]
        ci_ref[...] = h0i_ref[0]

    u = u_ref[...]
    NK = GW // 128
    SW = S // NK
    for kc in range(NK):
        ukc = _bf(u[:, kc * 128:(kc + 1) * 128])
        for half in range(2):
            cs = slice(half * S + kc * SW, half * S + (kc + 1) * SW)
            res = jnp.dot(ukc, bblk_ref[kc * 128:(kc + 1) * 128, cs], preferred_element_type=F32)
            for j in range(SW // LW):
                bu_ref[(half * S + kc * SW) // LW + j] = res[:, j * LW:(j + 1) * LW]

    nt = tc // 8
    trow = _iota2(nt, LW, 0)

    def lane_block(lc, _):
        pw = lambda n: (pwr_ref[lc, n:n + 1, :], pwi_ref[lc, n:n + 1, :])
        ar, ai = pw(0)
        xr = xi = None
        for r in range(8):
            tr = pl.ds(r, nt, stride=8)
            nr, ni = bu_ref[lc, tr, :], bu_ref[S // LW + lc, tr, :]
            if r:
                sr, si = _cmul(ar, ai, xr, xi)
                nr, ni = nr + sr, ni + si
            xr, xi = nr, ni
            hre_ref[lc, tr, :] = xr
            him_ref[lc, tr, :] = xi
        k = 1
        while k < nt:
            pr, pi = pw(8 + k - 1)
            sr, si = _cmul(jnp.where(trow >= k, pr, 0.0), jnp.where(trow >= k, pi, 0.0),
                           pltpu.roll(xr, k, 0), pltpu.roll(xi, k, 0))
            xr, xi = xr + sr, xi + si
            k *= 2
        kr, ki = cr_ref[lc], ci_ref[lc]
        sr, si = _cmul(pwr_ref[lc, 8:8 + nt, :], pwi_ref[lc, 8:8 + nt, :], kr, ki)
        xr, xi = xr + sr, xi + si
        cr_ref[lc] = xr[nt - 1:nt, :]
        ci_ref[lc] = xi[nt - 1:nt, :]
        pr_ = jnp.where(trow == 0, kr, pltpu.roll(xr, 1, 0))
        pi_ = jnp.where(trow == 0, ki, pltpu.roll(xi, 1, 0))
        for r in range(8):
            tr = pl.ds(r, nt, stride=8)
            sr, si = _cmul(*pw(r), pr_, pi_)
            hre_ref[lc, tr, :] = hre_ref[lc, tr, :] + sr
            him_ref[lc, tr, :] = him_ref[lc, tr, :] + si
        return 0

    def two_lane_blocks(i, _):
        lane_block(2 * i, 0)
        return lane_block(2 * i + 1, 0)

    lax.fori_loop(0, S // LW // 2, two_lane_blocks, 0)

    ys = []
    for kc in range(NK):
        ss, os_ = slice(kc * SW, (kc + 1) * SW), slice(kc * 128, (kc + 1) * 128)
        blocks = range(kc * SW // LW, (kc + 1) * SW // LW)
        hr = jnp.concatenate([_bf(hre_ref[j]) for j in blocks], axis=1)
        hi = jnp.concatenate([_bf(him_ref[j]) for j in blocks], axis=1)
        ys.append(jnp.dot(hr, cre_ref[ss, os_], preferred_element_type=F32)
                  - jnp.dot(hi, cim_ref[ss, os_], preferred_element_type=F32))
    y = jnp.concatenate(ys, axis=1) + d_ref[...] * u
    g = jax.nn.gelu(y)
    out = g * jax.nn.sigmoid(_dot(g, wglu_ref[...]) + bglu_ref[...])
    o_ref[...] = _rms_rows(out, on_ref[...])

    @pl.when(c == nct - 1)
    def _():
        hr_out_ref[0] = cr_ref[...]
        hi_out_ref[0] = ci_ref[...]


def _s5_call(z, row0, nseq, t, h0_re, h0_im, pw_re, pw_im, bblk, c_re, c_im, d_skip, w_glu, b_glu, out_norm,
             col_blk, tc):
    nct = t // tc
    S = S5_STATE
    NB = S // 128
    blk0 = row0 // tc
    const2 = lambda b, c: (0, 0)
    state = pl.BlockSpec((1, NB, 1, 128), lambda b, c: (b, 0, 0, 0))
    powers = pl.BlockSpec((NB, S5_POWERS, 128), lambda b, c: (0, 0, 0))
    lane_blocks = lambda p: jnp.transpose(p.reshape(S5_POWERS, NB, 128), (1, 0, 2))
    outs = pl.pallas_call(
        functools.partial(_s5_kernel, tc=tc, nct=nct),
        grid=(nseq, nct),
        in_specs=[pl.BlockSpec((tc, GW), lambda b, c: (blk0 + b * nct + c, col_blk)),
                  state, state, powers, powers,
                  pl.BlockSpec((GW, 2 * S), const2),
                  pl.BlockSpec((S, GW), const2), pl.BlockSpec((S, GW), const2),
                  pl.BlockSpec((1, GW), const2),
                  pl.BlockSpec((GW, GW), const2), pl.BlockSpec((1, GW), const2), pl.BlockSpec((1, GW), const2)],
        out_specs=[pl.BlockSpec((tc, GW), lambda b, c: (b * nct + c, 0)), state, state],
        out_shape=[jax.ShapeDtypeStruct((nseq * t, GW), F32),
                   jax.ShapeDtypeStruct((nseq, NB, 1, 128), F32), jax.ShapeDtypeStruct((nseq, NB, 1, 128), F32)],
        scratch_shapes=[pltpu.VMEM((2 * NB, tc, 128), F32), pltpu.VMEM((NB, tc, 128), F32),
                        pltpu.VMEM((NB, tc, 128), F32),
                        pltpu.VMEM((NB, 1, 128), F32), pltpu.VMEM((NB, 1, 128), F32)],
        compiler_params=_cparams("parallel", "arbitrary"),
        name="s5",
    )(z, h0_re.reshape(nseq, NB, 1, 128), h0_im.reshape(nseq, NB, 1, 128), lane_blocks(pw_re), lane_blocks(pw_im),
      bblk, c_re, c_im, d_skip, w_glu, b_glu, out_norm)
    return outs[0], outs[1].reshape(nseq, S), outs[2].reshape(nseq, S)


def _sbprep_kernel(q_ref, k_ref, v_ref, gmat_ref, qg_ref, kg_ref, *refs, transposed):
    gmat = gmat_ref[...]

    def head_rms(x, gain):
        ms = _dot2_exact_rhs(x * x, gmat) * (1.0 / HD)
        return x * lax.rsqrt(ms + RMS_EPS) * gain

    if transposed:
        qb_ref, kb_ref, vb_ref, kt_ref, vt_ref = refs[-5:]
    else:
        qb_ref, kb_ref, vb_ref, kn_ref = refs
    qb_ref[...] = _bf(head_rms(q_ref[...], qg_ref[...]) * (HD ** -0.5))
    kn = head_rms(k_ref[...], kg_ref[...])
    v = v_ref[...]
    kb_ref[...] = _bf(kn)
    vb_ref[...] = _bf(v)
    if transposed:
        kt_ref[...] = kn.T
        vt_ref[...] = v.T
    else:
        kn_ref[...] = kn


def _sbprep_call(z, row0, nseq, t, gmat, q_gain, k_gain, layer=0, depth=0, kt_prev=None, vt_prev=None):
    tm = min(t, 512)
    nct = t // tm
    blk0 = row0 // tm
    zcol = lambda j: pl.BlockSpec((tm, GW), lambda b, c: (blk0 + b * nct + c, j))
    const = lambda shp: pl.BlockSpec(shp, lambda b, c: (0,) * len(shp))
    row = pl.BlockSpec((tm, GW), lambda b, c: (b * nct + c, 0))
    bf_shape = jax.ShapeDtypeStruct((nseq * t, GW), BF16)
    in_specs = [zcol(3), zcol(4), zcol(5), const((GW, GW)), const((1, GW)), const((1, GW))]
    args = [z, z, z, gmat, q_gain, k_gain]
    aliases = {}
    if depth:
        tspec = pl.BlockSpec((None, None, GW, tm), lambda b, c: (layer, b, 0, c))
        tshape = jax.ShapeDtypeStruct((depth, nseq, GW, t), F32)
        out_specs, out_shape = [row, row, row, tspec, tspec], [bf_shape, bf_shape, bf_shape, tshape, tshape]
        if kt_prev is not None:
            in_specs += [pl.BlockSpec(memory_space=pl.ANY)] * 2
            args += [kt_prev, vt_prev]
            aliases = {6: 3, 7: 4}
    else:
        out_specs = [row, row, row, row]
        out_shape = [bf_shape, bf_shape, bf_shape, jax.ShapeDtypeStruct((nseq * t, GW), F32)]
    return pl.pallas_call(
        functools.partial(_sbprep_kernel, transposed=bool(depth)),
        grid=(nseq, nct),
        in_specs=in_specs,
        out_specs=out_specs,
        out_shape=out_shape,
        input_output_aliases=aliases,
        compiler_params=_cparams("parallel", "parallel"),
        name="sbprep",
    )(*args)


def _sb_tiles(qhs, k2s, v2s, carries, umat, mask):
    qk = [functools.partial(lax.dot_general, qhs[p], k2s[p], _NT, preferred_element_type=F32)
          for p in range(len(qhs))]
    pv = [functools.partial(lambda w, v2: jnp.dot(w, v2, preferred_element_type=F32), v2=v2s[p])
          for p in range(len(qhs))]
    return _sb_stages(qk, pv, carries, umat, mask)


def _sb_stages(qk, pv, carries, umat, mask, skew=1):
    d = functools.partial(jnp.dot, preferred_element_type=F32)
    n = len(qk)
    pvs, tots = [None] * n, [None] * n

    def pair_stages(p):
        z = qk[p]()
        yield
        lk = -(jnp.maximum(z, 0.0) + jnp.log(1.0 + jnp.exp(-jnp.abs(z))))
        if mask is not None:
            lk = jnp.where(mask, lk, 0.0)
        aft = d(_bf(lk), umat)
        tots[p] = jnp.broadcast_to(aft[:, 0:1] + lk[:, 0:1], (lk.shape[0], 2 * HD))
        yield
        e = z + lk + aft
        if carries[p] is not None:
            reps = e.shape[1] // (2 * HD)
            e = e + (carries[p] if reps == 1 else jnp.concatenate([carries[p]] * reps, axis=1))
        w = jnp.exp(e)
        if mask is not None:
            w = jnp.where(mask, w, 0.0)
        pvs[p] = pv[p](_bf(w))

    _round_robin([pair_stages(p) for p in range(n)], skew)
    return pvs, tots


def _sb_kernel(q_ref, kd_ref, vd_ref, kp_ref, vp_ref, on_ref, o_ref, qh_ref, acc_ref, carry_ref,
               *, tq, tk, npast, past_from_grid):
    PW = 2 * HD
    npairs = GW // PW
    m0 = _iota2(1, PW, 1) < HD
    row = _iota2(2 * tq, tq, 0)
    causal = _iota2(2 * tq, tq, 1) < jnp.where(row >= tq, row - tq, row)

    ud = jnp.where(_iota2(tq, tq, 0) > _iota2(tq, tq, 1), 1.0, 0.0).astype(BF16)
    up = jnp.where(_iota2(tk, tk, 0) > _iota2(tk, tk, 1), 1.0, 0.0).astype(BF16)
    pairs = [slice(p * PW, (p + 1) * PW) for p in range(npairs)]

    qhs = [_bf(_hat(q_ref[:, sl], m0)) for sl in pairs]
    pvs, tots = _sb_tiles(qhs, [_bf(kd_ref[:, sl]) for sl in pairs], [_bf(vd_ref[:, sl]) for sl in pairs],
                          [None] * npairs, ud, causal)
    for p in range(npairs):
        qh_ref[p] = qhs[p]
        acc_ref[p] = pvs[p]
        carry_ref[p] = tots[p]

    nblk = pl.program_id(1) * (tq // tk) if past_from_grid else npast

    def body(i, _):
        j = nblk - 1 - i
        r0 = pl.multiple_of(j * tk, tk)
        pvs, tots = _sb_tiles([qh_ref[p] for p in range(npairs)],
                              [_bf(kp_ref[pl.ds(r0, tk), sl]) for sl in pairs],
                              [_bf(vp_ref[pl.ds(r0, tk), sl]) for sl in pairs],
                              [carry_ref[p] for p in range(npairs)], up, None)
        for p in range(npairs):
            acc_ref[p] += pvs[p]
            carry_ref[p] += tots[p]
        return 0

    lax.fori_loop(0, nblk, body, 0)

    outs = []
    ss = jnp.zeros((tq, 1), F32)
    for p in range(npairs):
        a = acc_ref[p]
        o = jnp.where(m0, a[:tq], a[tq:])
        ss = ss + jnp.sum(o * o, axis=-1, keepdims=True)
        outs.append(o)
    scale = lax.rsqrt(ss * (1.0 / GW) + RMS_EPS)
    for p in range(npairs):
        sl = slice(p * PW, (p + 1) * PW)
        o_ref[:, sl] = outs[p] * scale * on_ref[:, sl]


def _sb_call(qn, kn, z, kpast, vpast, row0, nseq, t, tq, tk, past_len, v_col_blk, out_norm, layer=0):
    nq = t // tq
    blk0 = row0 // tq
    npairs = GW // (2 * HD)
    if kpast is None:
        kp_arr, vp_arr = kn, z
        sblk = row0 // t
        kp_spec = pl.BlockSpec((t, GW), lambda b, i: (sblk + b, 0))
        vp_spec = pl.BlockSpec((t, GW), lambda b, i: (sblk + b, v_col_blk))
        npast, from_grid = 0, True
    else:
        kp_arr, vp_arr = kpast, vpast
        kp_spec = pl.BlockSpec((None, None, past_len, GW), lambda b, i: (layer, b, 0, 0))
        vp_spec = kp_spec
        npast, from_grid = past_len // tk, False
    qrow = lambda b, i: (blk0 + b * nq + i, 0)
    return pl.pallas_call(
        functools.partial(_sb_kernel, tq=tq, tk=tk, npast=npast, past_from_grid=from_grid),
        grid=(nseq, nq),
        in_specs=[pl.BlockSpec((tq, GW), qrow), pl.BlockSpec((tq, GW), qrow),
                  pl.BlockSpec((tq, GW), lambda b, i: (blk0 + b * nq + i, v_col_blk)),
                  kp_spec, vp_spec, pl.BlockSpec((1, GW), lambda b, i: (0, 0))],
        out_specs=pl.BlockSpec((tq, GW), lambda b, i: (b * nq + i, 0)),
        out_shape=jax.ShapeDtypeStruct((nseq * t, GW), F32),
        scratch_shapes=[pltpu.VMEM((npairs, 2 * tq, 2 * HD), BF16), pltpu.VMEM((npairs, 2 * tq, 2 * HD), F32),
                        pltpu.VMEM((npairs, 2 * tq, 2 * HD), F32)],
        compiler_params=_cparams("parallel", "arbitrary"),
        name="sb_attn",
    )(qn, kn, z, kp_arr, vp_arr, out_norm)


def _sbs_kernel(q_ref, kd_ref, vd_ref, kt_ref, vt_ref, on_ref, o_ref, qh_ref, acc_ref, carry_ref,
                *, tq, tk, nblk, nchunks):
    c = pl.program_id(1)
    nh = GW // HD
    npairs = nh // 2
    hs = [slice(h * HD, (h + 1) * HD) for h in range(nh)]
    stack = functools.partial(jnp.concatenate, axis=0)
    mm = functools.partial(lax.dot_general, preferred_element_type=F32)

    @pl.when(c == 0)
    def _():
        row = _iota2(2 * tq, tq, 0)
        causal = _iota2(2 * tq, tq, 1) < jnp.where(row >= tq, row - tq, row)
        ud = jnp.where(_iota2(tq, tq, 0) > _iota2(tq, tq, 1), 1.0, 0.0).astype(BF16)
        qs = [_bf(q_ref[:, s]) for s in hs]
        kd = [_bf(kd_ref[:, s]) for s in hs]
        vd = [_bf(vd_ref[:, s]) for s in hs]
        for h in range(nh):
            qh_ref[h] = qs[h]
        qk = [lambda p=p: stack([mm(qs[2 * p], kd[2 * p], _NT), mm(qs[2 * p + 1], kd[2 * p + 1], _NT)])
              for p in range(npairs)]
        pv = [lambda w, p=p: stack([mm(w[:tq], vd[2 * p], _NN), mm(w[tq:], vd[2 * p + 1], _NN)])
              for p in range(npairs)]
        pvs, tots = _sb_stages(qk, pv, [None] * npairs, ud, causal, skew=0)
        for p in range(npairs):
            acc_ref[p] = pvs[p]
            carry_ref[p] = tots[p]

    up = jnp.where(_iota2(tk, tk, 0) > _iota2(tk, tk, 1), 1.0, 0.0).astype(BF16)
    for jb in reversed(range(nblk)):
        ks = slice(jb * tk, (jb + 1) * tk)
        qk = [lambda p=p: stack([mm(qh_ref[2 * p], _bf(kt_ref[2 * p, :, ks]), _NN),
                                 mm(qh_ref[2 * p + 1], _bf(kt_ref[2 * p + 1, :, ks]), _NN)])
              for p in range(npairs)]
        pv = [lambda w, p=p: stack([mm(w[:tq], _bf(vt_ref[2 * p, :, ks]), _NT),
                                    mm(w[tq:], _bf(vt_ref[2 * p + 1, :, ks]), _NT)])
              for p in range(npairs)]
        pvs, tots = _sb_stages(qk, pv, [carry_ref[p] for p in range(npairs)], up, None, skew=0)
        for p in range(npairs):
            acc_ref[p] += pvs[p]
            carry_ref[p] += tots[p]

    @pl.when(c == nchunks - 1)
    def _():
        heads = []
        for p in range(npairs):
            a = acc_ref[p]
            heads += [a[:tq], a[tq:]]
        o_ref[...] = _rms_rows(jnp.concatenate(heads, axis=1), on_ref[...])


def _sbs_call(qb, kb, vb, cache_kt, cache_vt, row0, nseq, tq, tk, tkc, out_norm, layer):
    past_len = cache_kt.shape[-1]
    nchunks = past_len // tkc
    blk0 = row0 // tq
    nh = GW // HD
    qrow = pl.BlockSpec((tq, GW), lambda b, c: (blk0 + b, 0))
    cspec = pl.BlockSpec((None, None, nh, HD, tkc), lambda b, c: (layer, b, 0, 0, nchunks - 1 - c))
    return pl.pallas_call(
        functools.partial(_sbs_kernel, tq=tq, tk=tk, nblk=tkc // tk, nchunks=nchunks),
        grid=(nseq, nchunks),
        in_specs=[qrow, qrow, qrow, cspec, cspec, pl.BlockSpec((1, GW), lambda b, c: (0, 0))],
        out_specs=pl.BlockSpec((tq, GW), lambda b, c: (b, 0)),
        out_shape=jax.ShapeDtypeStruct((nseq * tq, GW), F32),
        scratch_shapes=[pltpu.VMEM((nh, tq, HD), BF16), pltpu.VMEM((nh // 2, 2 * tq, HD), F32),
                        pltpu.VMEM((nh // 2, 2 * tq, 2 * HD), F32)],
        compiler_params=_cparams("parallel", "arbitrary"),
        name="sb_attn_cache",
    )(qb, kb, vb, cache_kt, cache_vt, out_norm)


def _rwkv_prep_kernel(zr_ref, zk_ref, zv_ref, zwa_ref, zg_ref, first_ref, mu_ref, w0_ref, w2_ref, a0_ref, a2_ref,
                      g2_ref, kk_ref, ka_ref, rk_ref, gmat_ref,
                      r_out, lw_out, k_out, v_out, kk_out, b_out, g_out, bonus_out, prev_ref, *, tm, wr):
    c = pl.program_id(1)
    cols = prev_ref.shape[1]

    @pl.when(c == 0)
    def _():
        prev_ref[...] = first_ref[0]

    first_row = _iota2(tm, 1, 0) == 0

    def shifted(ref, lo, width):
        x = ref[...]
        prev = jnp.where(first_row, prev_ref[:, lo:lo + width], pltpu.roll(x, 1, 0))
        return x + (prev - x) * mu_ref[:, lo:lo + width]

    r = shifted(zr_ref, 0, GW)
    k = shifted(zk_ref, GW, GW)
    v = shifted(zv_ref, 2 * GW, GW)
    wa = shifted(zwa_ref, 3 * GW, 2 * wr)
    gl = shifted(zg_ref, 3 * GW + 2 * wr, cols - 3 * GW - 2 * wr)
    for ref, lo in ((zr_ref, 0), (zk_ref, GW), (zv_ref, 2 * GW), (zwa_ref, 3 * GW), (zg_ref, 3 * GW + 2 * wr)):
        prev_ref[:, lo:lo + ref.shape[1]] = ref[tm - 1:tm, :]

    xw = w0_ref[...] + _dot(jnp.tanh(wa), w2_ref[...])
    w_log = -(jnp.maximum(-xw, 0.0) + jnp.log(1.0 + jnp.exp(-jnp.abs(xw)))) - 0.5
    a = jax.nn.sigmoid(a0_ref[...] + _dot(wa, a2_ref[...]))
    g_out[...] = _dot(jax.nn.sigmoid(gl), g2_ref[...])
    gmat = gmat_ref[...]
    kk = k * kk_ref[...]
    kk = kk / jnp.maximum(jnp.sqrt(_dot2_exact_rhs(kk * kk, gmat)), 1e-12)
    k = k * (1.0 + (a - 1.0) * ka_ref[...])
    r_out[...] = r
    lw_out[...] = -jnp.exp(w_log)
    k_out[...] = k
    v_out[...] = v
    kk_out[...] = kk
    b_out[...] = kk * a
    bonus_out[...] = _dot2_exact_rhs(r * k * rk_ref[...], gmat) * v


def _rwkv_prep_call(z, row0, nseq, t, tm, first, mu, w0, w2p, a0, a2p, g2, k_k, k_a, r_k, gmat, wr):
    nct = t // tm
    blk0 = row0 // tm
    cols = mu.shape[-1]
    zrow = lambda blkw, off: pl.BlockSpec((tm, blkw), lambda b, c: (blk0 + b * nct + c, off))
    c0 = (z.shape[1] - cols)
    const = lambda shp: pl.BlockSpec(shp, lambda b, c: (0,) * len(shp))
    orow = pl.BlockSpec((tm, GW), lambda b, c: (b * nct + c, 0))
    return pl.pallas_call(
        functools.partial(_rwkv_prep_kernel, tm=tm, wr=wr),
        grid=(nseq, nct),
        in_specs=[zrow(GW, c0 // GW), zrow(GW, c0 // GW + 1), zrow(GW, c0 // GW + 2),
                  zrow(2 * wr, (c0 + 3 * GW) // (2 * wr)), zrow(cols - 3 * GW - 2 * wr, (c0 + 3 * GW) // (2 * wr) + 1),
                  pl.BlockSpec((1, 1, cols), lambda b, c: (b, 0, 0)),
                  const((1, cols)), const((1, GW)), const((2 * wr, GW)), const((1, GW)), const((2 * wr, GW)),
                  const((cols - 3 * GW - 2 * wr, GW)), const((1, GW)), const((1, GW)), const((1, GW)),
                  const((GW, GW))],
        out_specs=[orow] * 8,
        out_shape=[jax.ShapeDtypeStruct((nseq * t, GW), F32)] * 8,
        scratch_shapes=[pltpu.VMEM((1, cols), F32)],
        compiler_params=_cparams("parallel", "arbitrary"),
        name="rwkv_prep",
    )(z, z, z, z, z, first.reshape(nseq, 1, cols), mu, w0, w2p, a0, a2p, g2, k_k, k_a, r_k, gmat)


def _hat(x, m0):
    return jnp.concatenate([jnp.where(m0, x, 0.0), jnp.where(m0, 0.0, x)], axis=0)


def _rwkv_chunk_kernel(r_ref, lw_ref, k_ref, v_ref, kk_ref, b_ref, rt_out, p3_out, m_out, n_out, *, group):
    C = RWKV_CHUNK
    C2 = 2 * C
    lane = _iota2(1, 2 * HD, 1)
    m0 = lane < HD
    ri = _iota2(C2, C2, 0)
    ci = _iota2(C2, C2, 1)
    same = (ri < C) == (ci < C)
    strict = jnp.logical_and(same, ri > ci)
    incl = jnp.logical_and(same, ri >= ci)
    ltri = jnp.where(_iota2(C, C, 0) >= _iota2(C, C, 1), 1.0, 0.0).astype(BF16)
    eye = _iota2(2 * HD, 2 * HD, 0) == _iota2(2 * HD, 2 * HD, 1)
    eye2 = ri == ci

    def pair_stages(p):
        sl = slice(p * 2 * HD, (p + 1) * 2 * HD)
        lw = lw_ref[:, sl]
        cl = _dot2_exact_lhs(ltri, lw)
        yield
        clast = cl[C - 1:C, :]
        kkt = _hat(kk_ref[:, sl] * jnp.exp(cl - lw), m0)
        rt = _hat(r_ref[:, sl] * jnp.exp(cl), m0)
        einv = jnp.exp(-cl)
        kb = _split(_hat(k_ref[:, sl] * einv, m0))
        bb = _split(_hat(b_ref[:, sl] * einv, m0))
        efin = jnp.exp(clast - cl)
        kh = _split(_hat(k_ref[:, sl] * efin, m0))
        bh = _split(_hat(b_ref[:, sl] * efin, m0))
        vh = _split(_hat(v_ref[:, sl], m0))
        lhs = _split(jnp.concatenate([kkt, rt], axis=0))
        gk = _dot3s(lhs, kb, _NT)
        gb = _dot3s(lhs, bb, _NT)
        yield
        a_kk = jnp.where(strict, gk[:C2], 0.0)
        a_rk = _split(jnp.where(incl, gk[C2:], 0.0))
        a_kb = jnp.where(strict, gb[:C2], 0.0)
        a_rb = _split(jnp.where(incl, gb[C2:], 0.0))
        p1 = _dot3s(_split(a_kk), vh)
        tinv = jnp.where(eye2, 1.0, 0.0) - a_kb
        lp = _split(a_kb)
        n = 2
        while n < C:
            yield
            lpf = _dot3s(lp, lp)
            lp = _split(lpf)
            tinv = tinv + _dot3s(_split(tinv), lp)
            n *= 2
        yield
        x = _dot3s(_split(tinv), _split(jnp.concatenate([kkt, p1], axis=1)))
        yield
        xs = _split(x)
        kt = (xs[0][:, :2 * HD], xs[1][:, :2 * HD])
        p2 = (xs[0][:, 2 * HD:], xs[1][:, 2 * HD:])
        y2 = _dot3s(a_rb, xs)
        p3 = _dot3s(a_rk, vh)
        mm = _dot3s(kt, bh, _TN)
        nn = _dot3s(vh, kh, _TN) - _dot3s(p2, bh, _TN)
        yield
        rt_out[p] = rt - y2[:, :2 * HD]
        p3_out[p] = p3 - y2[:, 2 * HD:]
        m_out[p] = jnp.where(eye, jnp.exp(clast), 0.0) - mm
        n_out[p] = nn

    npairs = GW // (2 * HD)
    for p0 in range(0, npairs, group):
        _round_robin([pair_stages(p) for p in range(p0, p0 + group)])


def _rwkv_chunk_call(r, lw, k, v, kk, b):
    n = r.shape[0]
    C = RWKV_CHUNK
    nc = n // C
    npairs = GW // (2 * HD)
    row = pl.BlockSpec((C, GW), lambda i: (i, 0))
    blk = lambda rows: pl.BlockSpec((None, npairs, rows, 2 * HD), lambda i: (i, 0, 0, 0))
    shp = lambda rows: jax.ShapeDtypeStruct((nc, npairs, rows, 2 * HD), F32)
    return pl.pallas_call(
        functools.partial(_rwkv_chunk_kernel, group=4),
        grid=(nc,),
        in_specs=[row] * 6,
        out_specs=[blk(2 * C), blk(2 * C), blk(2 * HD), blk(2 * HD)],
        out_shape=[shp(2 * C), shp(2 * C), shp(2 * HD), shp(2 * HD)],
        compiler_params=_cparams("parallel"),
        name="rwkv_chunk",
    )(r, lw, k, v, kk, b)


def _rwkv_seq_kernel(rt_ref, p3_ref, m_ref, n_ref, s0_ref, g_ref, bonus_ref, lnw_ref, lnb_ref, gmat_ref,
                     o_ref, s_out_ref, s_ref, *, nct, gsz):
    c = pl.program_id(1)
    C = RWKV_CHUNK

    @pl.when(c == 0)
    def _():
        s_ref[...] = s0_ref[...]

    gm = gmat_ref[0:2 * HD, 0:2 * HD]
    npairs = GW // (2 * HD)
    pairs = [slice(p * 2 * HD, (p + 1) * 2 * HD) for p in range(npairs)]
    sp = [(s, p) for s in range(gsz) for p in range(npairs)]
    ss = [_split(s_ref[s, p]) for s, p in sp]
    yhs = [_dot3s(_split(rt_ref[s, p]), ss[i], _NT) + p3_ref[s, p] for i, (s, p) in enumerate(sp)]
    snew = [_dot3s(ss[i], _split(m_ref[s, p])) + n_ref[s, p] for i, (s, p) in enumerate(sp)]
    ys = [yh[:C] + yh[C:] for yh in yhs]
    means = [_dot2_exact_rhs(y, gm) * (1.0 / HD) for y in ys]
    ds = [y - m for y, m in zip(ys, means)]
    vrs = [_dot2_exact_rhs(d * d, gm) * (1.0 / HD) for d in ds]
    for i, (s, p) in enumerate(sp):
        sl = pairs[p]
        yn = ds[i] * lax.rsqrt(vrs[i] + GN_EPS) * lnw_ref[:, sl] + lnb_ref[:, sl]
        o_ref[s, :, sl] = (yn + bonus_ref[s, :, sl]) * g_ref[s, :, sl]
        s_ref[s, p] = snew[i]

    @pl.when(c == nct - 1)
    def _():
        s_out_ref[...] = s_ref[...]


def _rwkv_seq_call(rt, p3, m, n, nseq, nct, gsz, s0, g, bonus, lnx_w, lnx_b, gmat):
    C = RWKV_CHUNK
    npairs = GW // (2 * HD)
    c5 = lambda a: a.reshape(nseq, nct, *a.shape[1:])
    cblk = lambda rows: pl.BlockSpec((gsz, None, npairs, rows, 2 * HD), lambda b, c: (b, c, 0, 0, 0))
    sblk = pl.BlockSpec((gsz, npairs, 2 * HD, 2 * HD), lambda b, c: (b, 0, 0, 0))
    rowblk = pl.BlockSpec((gsz, C, GW), lambda b, c: (b, c, 0))
    const = lambda shp: pl.BlockSpec(shp, lambda b, c: (0,) * len(shp))
    od, s_out = pl.pallas_call(
        functools.partial(_rwkv_seq_kernel, nct=nct, gsz=gsz),
        grid=(nseq // gsz, nct),
        in_specs=[cblk(2 * C), cblk(2 * C), cblk(2 * HD), cblk(2 * HD), sblk, rowblk, rowblk,
                  const((1, GW)), const((1, GW)), const((GW, GW))],
        out_specs=[rowblk, sblk],
        out_shape=[jax.ShapeDtypeStruct((nseq, nct * C, GW), F32),
                   jax.ShapeDtypeStruct((nseq, npairs, 2 * HD, 2 * HD), F32)],
        scratch_shapes=[pltpu.VMEM((gsz, npairs, 2 * HD, 2 * HD), F32)],
        compiler_params=_cparams("parallel", "arbitrary"),
        name="rwkv_seq",
    )(c5(rt), c5(p3), c5(m), c5(n), s0, g.reshape(nseq, nct * C, GW), bonus.reshape(nseq, nct * C, GW),
      lnx_w, lnx_b, gmat)
    return od.reshape(nseq * nct * C, GW), s_out


def _pair_states(s):
    b, h = s.shape[:2]
    s = s.reshape(b, h // 2, 2, HD, HD)
    z = jnp.zeros_like(s[:, :, 0])
    top = jnp.concatenate([s[:, :, 0], z], axis=-1)
    bot = jnp.concatenate([z, s[:, :, 1]], axis=-1)
    return jnp.concatenate([top, bot], axis=-2)


def _unpair_states(s2):
    b, hp = s2.shape[:2]
    return jnp.stack([s2[:, :, :HD, :HD], s2[:, :, HD:, HD:]], axis=2).reshape(b, 2 * hp, HD, HD)


def _forward(xp, xs, cache_k, cache_v, s5_re0, s5_im0, wkv0, shift0, W):
    Bp, Tp, D = xp.shape
    Bs, Ts, _ = xs.shape
    depth = W['ln_ffn1'].shape[0]
    past_len = cache_k.shape[2]
    Np, Ns = Bp * Tp, Bs * Ts
    n_heads_d = GW // HD
    wr = W['rwkv_w2'].shape[1]
    dcols = W['rwkv_mu'].shape[-1]

    x = jnp.concatenate([xp.reshape(Np, D), xs.reshape(Ns, D)], axis=0)
    cache_kt = jnp.transpose(cache_k, (0, 1, 3, 4, 2))
    cache_vt = jnp.transpose(cache_v, (0, 1, 3, 4, 2))
    wbf = {k: _bf(W[k]) for k in ('w_ffn1_gate', 'w_ffn1_up', 'w_ffn1_down', 'w_in', 'w_out',
                                  'w_ffn2_gate', 'w_ffn2_up', 'w_ffn2_down')}
    hid = lax.broadcasted_iota(jnp.int32, (GW, GW), 0) // HD
    gmat = (hid == jnp.transpose(hid)).astype(BF16)
    grp = lax.broadcasted_iota(jnp.int32, (GW, S5_STATE), 0) // (GW // 32) == \
        lax.broadcasted_iota(jnp.int32, (GW, S5_STATE), 1) // (S5_STATE // 32)
    zeros_first = jnp.zeros((Bp, dcols), F32)
    zeros_s5 = jnp.zeros((Bp, S5_STATE), F32)
    zeros_wkv = jnp.zeros((Bp, n_heads_d // 2, 2 * HD, 2 * HD), F32)

    h = _rms_call(x, W['ln_ffn1'][0][None])
    outs = {k: [] for k in ('ks', 'vs', 's5rp', 's5ip', 's5rs', 's5is', 'wkvp', 'wkvs', 'shp', 'shs', 'gv')}
    kt_buf = vt_buf = None
    for l in range(depth):
        g1 = lambda name: W[name][l][None]
        x, h = _ffn_call(x, h, wbf['w_ffn1_gate'], wbf['w_ffn1_up'], wbf['w_ffn1_down'], g1('ln_mix'), l)
        z = _inproj_call(h, wbf['w_in'], l)

        oa_p, _ = _gmlp_call(z, 0, Np, min(Tp, GMLP_CHUNK), g1('gmlp_v_norm'), W['gmlp_ws'][l], W['gmlp_b'][l],
                             g1('out_norm_a'))
        oa_s, gv_s = _gmlp_call(z, Np, Ns, min(Ts, GMLP_CHUNK), g1('gmlp_v_norm'), W['gmlp_ws'][l], W['gmlp_b'][l],
                                g1('out_norm_a'))

        pw_re, pw_im, bb_re, bb_im = _s5_params(W['s5_lam_re'][l], W['s5_lam_im'][l], W['s5_log_dt'][l],
                                                W['s5_b_re'][l], W['s5_b_im'][l])
        bblk = jnp.concatenate([jnp.where(grp, jnp.tile(bb_re, (32, 1)), 0.0),
                                jnp.where(grp, jnp.tile(bb_im, (32, 1)), 0.0)], axis=1).astype(BF16)
        grp_t = jnp.transpose(grp)
        cre = jnp.where(grp_t, jnp.tile(jnp.transpose(W['s5_c_re'][l], (0, 2, 1)).reshape(S5_STATE, -1), (1, 32)),
                        0.0).astype(BF16)
        cim = jnp.where(grp_t, jnp.tile(jnp.transpose(W['s5_c_im'][l], (0, 2, 1)).reshape(S5_STATE, -1), (1, 32)),
                        0.0).astype(BF16)
        s5_args = (pw_re, pw_im, bblk, cre, cim, g1('s5_d'), W['s5_w_glu'][l], g1('s5_b_glu'), g1('out_norm_b'))
        ob_p, s5rp, s5ip = _s5_call(z, 0, Bp, Tp, zeros_s5, zeros_s5, *s5_args, col_blk=2, tc=min(Tp, S5_TC))
        ob_s, s5rs, s5is = _s5_call(z, Np, Bs, Ts, s5_re0[l].reshape(Bs, S5_STATE), s5_im0[l].reshape(Bs, S5_STATE),
                                    *s5_args, col_blk=2, tc=min(Ts, S5_TC))

        qg = jnp.tile(W['sb_q_norm'][l], GW // HD)[None]
        kg = jnp.tile(W['sb_k_norm'][l], GW // HD)[None]
        qb_p, kb_p, vb_p, kt_buf, vt_buf = _sbprep_call(z, 0, Bp, Tp, gmat, qg, kg, l, depth, kt_buf, vt_buf)
        qb_s, kb_s, vb_s, kn_s = _sbprep_call(z, Np, Bs, Ts, gmat, qg, kg)
        oc_p = _sb_call(qb_p, kb_p, vb_p, None, None, 0, Bp, Tp, min(Tp, SB_TQ), min(Tp, SB_TK), 0, 0,
                        g1('out_norm_c'))
        oc_s = _sbs_call(qb_s, kb_s, vb_s, cache_kt, cache_vt, 0, Bs, Ts, min(past_len, SB_TK_SAMPLE),
                         min(past_len, SB_CACHE_CHUNK), g1('out_norm_c'), l)

        w2p = jnp.concatenate([W['rwkv_w2'][l], jnp.zeros_like(W['rwkv_a2'][l])], axis=0)
        a2p = jnp.concatenate([jnp.zeros_like(W['rwkv_w2'][l]), W['rwkv_a2'][l]], axis=0)
        rk = W['rwkv_r_k'][l].reshape(1, GW)
        prep_args = (g1('rwkv_mu'), g1('rwkv_w0'), w2p, g1('rwkv_a0'), a2p, W['rwkv_g2'][l], g1('rwkv_k_k'),
                     g1('rwkv_k_a'), rk, gmat, wr)
        pp = _rwkv_prep_call(z, 0, Bp, Tp, min(Tp, 512), zeros_first, *prep_args)
        ps = _rwkv_prep_call(z, Np, Bs, Ts, Ts, shift0[l], *prep_args)
        seq_args = (g1('rwkv_lnx_w'), g1('rwkv_lnx_b'), gmat)
        od_p, wkv_p = _rwkv_seq_call(*_rwkv_chunk_call(*pp[:6]), Bp, Tp // RWKV_CHUNK, min(Bp, 2), zeros_wkv,
                                     pp[6], pp[7], *seq_args)
        od_s, wkv_s = _rwkv_seq_call(*_rwkv_chunk_call(*ps[:6]), Bs, Ts // RWKV_CHUNK, min(Bs, 4),
                                     _pair_states(wkv0[l]), ps[6], ps[7], *seq_args)

        gain_next = g1('ln_ffn2')
        x, h = _outproj_call(x, (oa_p, ob_p, oc_p, od_p), (oa_s, ob_s, oc_s, od_s), wbf['w_out'], gain_next, l)
        gain_next = W['ln_ffn1'][l + 1][None] if l + 1 < depth else g1('ln_ffn2')
        x, h = _ffn_call(x, h, wbf['w_ffn2_gate'], wbf['w_ffn2_up'], wbf['w_ffn2_down'], gain_next, l)

        nh = GW // HD
        outs['ks'].append(kn_s.reshape(Bs, Ts, nh, HD))
        outs['vs'].append(z[Np:, 5 * GW:6 * GW].reshape(Bs, Ts, nh, HD))
        outs['s5rp'].append(s5rp.reshape(Bp, 32, -1))
        outs['s5ip'].append(s5ip.reshape(Bp, 32, -1))
        outs['s5rs'].append(s5rs.reshape(Bs, 32, -1))
        outs['s5is'].append(s5is.reshape(Bs, 32, -1))
        outs['wkvp'].append(_unpair_states(wkv_p))
        outs['wkvs'].append(_unpair_states(wkv_s))
        zd = z[:, z.shape[1] - dcols:]
        outs['shp'].append(zd[:Np].reshape(Bp, Tp, dcols)[:, -1])
        outs['shs'].append(zd[Np:].reshape(Bs, Ts, dcols)[:, -1])
        outs['gv'].append(gv_s.reshape(Bs, Ts, GW))

    st = lambda k: jnp.stack(outs[k], axis=0)
    untr = lambda a: jnp.transpose(a.reshape(depth, Bp, GW // HD, HD, Tp), (0, 1, 4, 2, 3))
    return (x[:Np].reshape(Bp, Tp, D), x[Np:].reshape(Bs, Ts, D),
            untr(kt_buf), untr(vt_buf), st('ks'), st('vs'),
            st('s5rp'), st('s5ip'), st('s5rs'), st('s5is'),
            st('wkvp'), st('wkvs'), st('shp'), st('shs'), st('gv'))


_WEIGHT_NAMES = ('ln_ffn1', 'w_ffn1_gate', 'w_ffn1_up', 'w_ffn1_down', 'ln_mix', 'w_in',
                 'gmlp_v_norm', 'gmlp_ws', 'gmlp_b', 'out_norm_a',
                 's5_lam_re', 's5_lam_im', 's5_log_dt', 's5_b_re', 's5_b_im', 's5_c_re', 's5_c_im', 's5_d',
                 's5_w_glu', 's5_b_glu', 'out_norm_b',
                 'sb_q_norm', 'sb_k_norm', 'out_norm_c',
                 'rwkv_mu', 'rwkv_w0', 'rwkv_w2', 'rwkv_a0', 'rwkv_a2', 'rwkv_g2', 'rwkv_k_k', 'rwkv_k_a', 'rwkv_r_k',
                 'rwkv_lnx_w', 'rwkv_lnx_b',
                 'w_out', 'ln_ffn2', 'w_ffn2_gate', 'w_ffn2_up', 'w_ffn2_down')


def kernel(x_prompt, x_sample, cache_sb_k, cache_sb_v, state_s5_re, state_s5_im, state_rwkv_wkv, state_rwkv_shift,
           ln_ffn1, w_ffn1_gate, w_ffn1_up, w_ffn1_down, ln_mix, w_in,
           gmlp_v_norm, gmlp_ws, gmlp_b, out_norm_a,
           s5_lam_re, s5_lam_im, s5_log_dt, s5_b_re, s5_b_im, s5_c_re, s5_c_im, s5_d, s5_w_glu, s5_b_glu, out_norm_b,
           sb_q_norm, sb_k_norm, out_norm_c,
           rwkv_mu, rwkv_w0, rwkv_w2, rwkv_a0, rwkv_a2, rwkv_g2, rwkv_k_k, rwkv_k_a, rwkv_r_k, rwkv_lnx_w, rwkv_lnx_b,
           w_out, ln_ffn2, w_ffn2_gate, w_ffn2_up, w_ffn2_down):
    weights = (ln_ffn1, w_ffn1_gate, w_ffn1_up, w_ffn1_down, ln_mix, w_in,
               gmlp_v_norm, gmlp_ws, gmlp_b, out_norm_a,
               s5_lam_re, s5_lam_im, s5_log_dt, s5_b_re, s5_b_im, s5_c_re, s5_c_im, s5_d, s5_w_glu, s5_b_glu,
               out_norm_b, sb_q_norm, sb_k_norm, out_norm_c,
               rwkv_mu, rwkv_w0, rwkv_w2, rwkv_a0, rwkv_a2, rwkv_g2, rwkv_k_k, rwkv_k_a, rwkv_r_k, rwkv_lnx_w,
               rwkv_lnx_b, w_out, ln_ffn2, w_ffn2_gate, w_ffn2_up, w_ffn2_down)
    W = dict(zip(_WEIGHT_NAMES, weights))
    return _forward(x_prompt, x_sample, cache_sb_k, cache_sb_v, state_s5_re, state_s5_im, state_rwkv_wkv,
                    state_rwkv_shift, W)
```

```python
import functools

import jax
import jax.numpy as jnp
from jax import lax
from jax.experimental import pallas as pl
from jax.experimental.pallas import tpu as pltpu

F32 = jnp.float32
BF16 = jnp.bfloat16

RMS_EPS = 1e-6
GN_EPS = 64e-5
A_HEADS = 4
GMLP_CHUNK = 128
HD = 64
GW = 512
S5_STATE = 2048
S5_TC = 256
S5_POWERS = 8 + S5_TC // 8
RWKV_CHUNK = 64
SB_TQ, SB_TK = 256, 256
SB_TK_SAMPLE = 256
SB_CACHE_CHUNK = 1024
FFN_TM = 768
FFN_VMEM_LIMIT = 60 * 1024 * 1024
VMEM_LIMIT = 56 * 1024 * 1024


def _cparams(*sem, vmem=VMEM_LIMIT):
    return pltpu.CompilerParams(dimension_semantics=sem, vmem_limit_bytes=vmem)


def _bf(x):
    return x.astype(BF16)


_NN = (((1,), (0,)), ((), ()))
_NT = (((1,), (1,)), ((), ()))
_TN = (((0,), (0,)), ((), ()))


def _dot(a, b, dims=_NN):
    return lax.dot_general(_bf(a), _bf(b), dims, preferred_element_type=F32)


def _split(x):
    hi = _bf(x)
    lo = _bf(x - hi.astype(F32))
    return hi, lo


def _dot3(a, b, dims=_NN):
    ah, al = _split(a)
    bh, bl = _split(b)
    d = functools.partial(lax.dot_general, dimension_numbers=dims, preferred_element_type=F32)
    return d(ah, bh) + (d(ah, bl) + d(al, bh))


def _dot3s(a, b, dims=_NN):
    d = functools.partial(lax.dot_general, dimension_numbers=dims, preferred_element_type=F32)
    return d(a[0], b[0]) + d(a[1], b[0])


def _dot2_exact_rhs(a, b_bf, dims=_NN):
    ah, al = _split(a)
    d = functools.partial(lax.dot_general, dimension_numbers=dims, preferred_element_type=F32)
    return d(ah, b_bf) + d(al, b_bf)


def _dot2_exact_lhs(a_bf, b, dims=_NN):
    bh, bl = _split(b)
    d = functools.partial(lax.dot_general, dimension_numbers=dims, preferred_element_type=F32)
    return d(a_bf, bh) + d(a_bf, bl)


def _round_robin(gens, skew=0):
    done = [False] * len(gens)
    tick = 0
    while not all(done):
        for i, g in enumerate(gens):
            if not done[i] and tick >= i * skew:
                try:
                    next(g)
                except StopIteration:
                    done[i] = True
        tick += 1


def _rms_rows(x, gain):
    ms = jnp.mean(x * x, axis=-1, keepdims=True)
    return x * lax.rsqrt(ms + RMS_EPS) * gain


def _iota2(n, m, axis):
    return lax.broadcasted_iota(jnp.int32, (n, m), axis)


def _rms_kernel(x_ref, g_ref, h_ref):
    h_ref[...] = _bf(_rms_rows(x_ref[...], g_ref[...]))


def _rms_call(x, gain, tm=512):
    n, d = x.shape
    return pl.pallas_call(
        _rms_kernel,
        grid=(n // tm,),
        in_specs=[pl.BlockSpec((tm, d), lambda i: (i, 0)), pl.BlockSpec((1, d), lambda i: (0, 0))],
        out_specs=pl.BlockSpec((tm, d), lambda i: (i, 0)),
        out_shape=jax.ShapeDtypeStruct((n, d), BF16),
        compiler_params=_cparams("parallel"),
        name="rms",
    )(x, gain)


def _ffn_kernel(x_ref, h_ref, wg_ref, wu_ref, wd_ref, gn_ref, o_ref, hn_ref, *, nj):
    j = pl.program_id(1)

    @pl.when(j == 0)
    def _():
        o_ref[...] = jnp.zeros_like(o_ref)

    h = h_ref[...]
    g = jnp.dot(h, wg_ref[...], preferred_element_type=F32)
    u = jnp.dot(h, wu_ref[...], preferred_element_type=F32)
    a = _bf(g * jax.nn.sigmoid(g) * u)
    o_ref[...] += jnp.dot(a, wd_ref[...], preferred_element_type=F32)

    @pl.when(j == nj - 1)
    def _():
        y = x_ref[...] + 0.5 * o_ref[...]
        o_ref[...] = y
        hn_ref[...] = _bf(_rms_rows(y, gn_ref[...]))


def _ffn_call(x, h, wg, wu, wd, gain_next, layer, tf=512):
    n, d = x.shape
    ff = wg.shape[-1]
    nj = ff // tf
    tm = FFN_TM if n % FFN_TM == 0 else 512
    return pl.pallas_call(
        functools.partial(_ffn_kernel, nj=nj),
        grid=(n // tm, nj),
        in_specs=[
            pl.BlockSpec((tm, d), lambda i, j: (i, 0)),
            pl.BlockSpec((tm, d), lambda i, j: (i, 0)),
            pl.BlockSpec((None, d, tf), lambda i, j: (layer, 0, j)),
            pl.BlockSpec((None, d, tf), lambda i, j: (layer, 0, j)),
            pl.BlockSpec((None, tf, d), lambda i, j: (layer, j, 0)),
            pl.BlockSpec((1, d), lambda i, j: (0, 0)),
        ],
        out_specs=[pl.BlockSpec((tm, d), lambda i, j: (i, 0)), pl.BlockSpec((tm, d), lambda i, j: (i, 0))],
        out_shape=[jax.ShapeDtypeStruct((n, d), F32), jax.ShapeDtypeStruct((n, d), BF16)],
        compiler_params=_cparams("parallel", "arbitrary", vmem=FFN_VMEM_LIMIT),
        name="ffn",
    )(x, h, wg, wu, wd, gain_next)


def _inproj_kernel(h_ref, w_ref, z_ref):
    z_ref[...] = jnp.dot(h_ref[...], w_ref[...], preferred_element_type=F32)


def _inproj_call(h, w_in, layer):
    n, d = h.shape
    tm = FFN_TM if n % FFN_TM == 0 else 512
    cols = w_in.shape[-1]
    tn = cols // 2
    return pl.pallas_call(
        _inproj_kernel,
        grid=(2, n // tm),
        in_specs=[pl.BlockSpec((tm, d), lambda j, i: (i, 0)),
                  pl.BlockSpec((None, d, tn), lambda j, i: (layer, 0, j))],
        out_specs=pl.BlockSpec((tm, tn), lambda j, i: (i, j)),
        out_shape=jax.ShapeDtypeStruct((n, cols), F32),
        compiler_params=_cparams("parallel", "parallel"),
        name="inproj",
    )(h, w_in)


def _outproj_kernel(x_ref, *refs, n_first):
    first, second = refs[0:4], refs[4:8]
    w_ref, gn_ref, o_ref, hn_ref = refs[8:]

    def run(mix_refs):
        acc = x_ref[...]
        for i, r in enumerate(mix_refs):
            acc = acc + jnp.dot(_bf(r[...]), w_ref[i * GW:(i + 1) * GW, :], preferred_element_type=F32)
        o_ref[...] = acc
        hn_ref[...] = _bf(_rms_rows(acc, gn_ref[...]))

    i = pl.program_id(0)
    pl.when(i < n_first)(lambda: run(first))
    pl.when(i >= n_first)(lambda: run(second))


def _outproj_call(x, mix_first, mix_second, w_out, gain_next, layer, tm=256):
    n, d = x.shape
    n_first = mix_first[0].shape[0] // tm
    n_second = mix_second[0].shape[0] // tm
    row = lambda i: (i, 0)
    first_row = lambda i: (jnp.minimum(i, n_first - 1), 0)
    second_row = lambda i: (jnp.maximum(i - n_first, 0), 0)
    return pl.pallas_call(
        functools.partial(_outproj_kernel, n_first=n_first),
        grid=(n_first + n_second,),
        in_specs=[pl.BlockSpec((tm, d), row)] + [pl.BlockSpec((tm, GW), first_row)] * 4
        + [pl.BlockSpec((tm, GW), second_row)] * 4
        + [pl.BlockSpec((None, d, d), lambda i: (layer, 0, 0)), pl.BlockSpec((1, d), lambda i: (0, 0))],
        out_specs=[pl.BlockSpec((tm, d), row), pl.BlockSpec((tm, d), row)],
        out_shape=[jax.ShapeDtypeStruct((n, d), F32), jax.ShapeDtypeStruct((n, d), BF16)],
        compiler_params=_cparams("parallel"),
        name="outproj",
    )(x, *mix_first, *mix_second, w_out, gain_next)


def _gmlp_kernel(z_ref, vn_ref, ws_ref, bt_ref, on_ref, o_ref, v_ref, *, L, nch):
    causal = _iota2(L, L, 1) <= _iota2(L, L, 0)
    hw = GW // A_HEADS
    ws = [_bf(jnp.where(causal, ws_ref[h], 0.0)) for h in range(A_HEADS)]

    def chunk_stages(c):
        rows = slice(c * L, (c + 1) * L)
        z = jax.nn.gelu(z_ref[rows, :])
        vhs = [_rms_rows(z[:, GW + h * hw:GW + (h + 1) * hw], vn_ref[...]) for h in range(A_HEADS)]
        for h in range(A_HEADS):
            v_ref[rows, h * hw:(h + 1) * hw] = vhs[h]
        yield
        ss = [_dot(ws[h], vhs[h]) for h in range(A_HEADS)]
        yield
        outs = [z[:, h * hw:(h + 1) * hw] * (ss[h] + bt_ref[:, h:h + 1]) for h in range(A_HEADS)]
        sq = sum(jnp.sum(o * o, axis=-1, keepdims=True) for o in outs)
        scale = lax.rsqrt(sq * (1.0 / GW) + RMS_EPS)
        for h in range(A_HEADS):
            o_ref[rows, h * hw:(h + 1) * hw] = outs[h] * scale * on_ref[:, h * hw:(h + 1) * hw]

    _round_robin([chunk_stages(c) for c in range(nch)])


def _gmlp_call(z, row0, nrows, L, v_norm, ws, b, out_norm, nch=4):
    tm = nch * L
    nb = nrows // tm
    b0 = row0 // tm
    ws_l = ws[:, :L, :L]
    bt = jnp.transpose(b[:, :L])
    row = lambda i: (i, 0)
    return pl.pallas_call(
        functools.partial(_gmlp_kernel, L=L, nch=nch),
        grid=(nb,),
        in_specs=[pl.BlockSpec((tm, 2 * GW), lambda i: (b0 + i, 0)),
                  pl.BlockSpec((1, GW // A_HEADS), lambda i: (0, 0)),
                  pl.BlockSpec((A_HEADS, L, L), lambda i: (0, 0, 0)),
                  pl.BlockSpec((L, A_HEADS), lambda i: (0, 0)),
                  pl.BlockSpec((1, GW), lambda i: (0, 0))],
        out_specs=[pl.BlockSpec((tm, GW), row), pl.BlockSpec((tm, GW), row)],
        out_shape=[jax.ShapeDtypeStruct((nrows, GW), F32), jax.ShapeDtypeStruct((nrows, GW), F32)],
        compiler_params=_cparams("parallel"),
        name="gmlp",
    )(z, v_norm, ws_l, bt, out_norm)


def _s5_param_kernel(lr_ref, li_ref, ldt_ref, brt_ref, bit_ref, pw_re_ref, pw_im_ref, bb_re_ref, bb_im_ref):
    lr = lr_ref[...]
    li = li_ref[...]
    dt = jnp.exp(ldt_ref[...])
    row = _iota2(S5_POWERS, S5_STATE, 0)
    n = jnp.where(row < 8, row + 1, 8 * (row - 7)).astype(F32)
    mag = jnp.exp(n * (lr * dt))
    ang = n * (li * dt)
    pw_re = mag * jnp.cos(ang)
    pw_im = mag * jnp.sin(ang)
    pw_re_ref[...] = pw_re
    pw_im_ref[...] = pw_im
    ab_re = pw_re[0:1, :]
    ab_im = pw_im[0:1, :]
    den = lr * lr + li * li
    nr, ni = ab_re - 1.0, ab_im
    cf_re = (nr * lr + ni * li) / den
    cf_im = (ni * lr - nr * li) / den
    br = brt_ref[...]
    bi = bit_ref[...]
    bb_re_ref[...] = cf_re * br - cf_im * bi
    bb_im_ref[...] = cf_re * bi + cf_im * br


def _s5_params(lam_re, lam_im, log_dt, b_re, b_im):
    g, p = lam_re.shape
    s = g * p
    ch = b_re.shape[-1]
    ldt = jnp.broadcast_to(log_dt[:, None], (g, p)).reshape(1, s)
    brt = jnp.transpose(b_re.reshape(s, ch))
    bit = jnp.transpose(b_im.reshape(s, ch))
    return pl.pallas_call(
        _s5_param_kernel,
        out_shape=[jax.ShapeDtypeStruct((S5_POWERS, s), F32), jax.ShapeDtypeStruct((S5_POWERS, s), F32),
                   jax.ShapeDtypeStruct((ch, s), F32), jax.ShapeDtypeStruct((ch, s), F32)],
        name="s5_params",
    )(lam_re.reshape(1, s), lam_im.reshape(1, s), ldt, brt, bit)


def _cmul(ar, ai, xr, xi):
    return ar * xr - ai * xi, ar * xi + ai * xr


def _s5_kernel(u_ref, h0r_ref, h0i_ref, pwr_ref, pwi_ref, bblk_ref, cre_ref, cim_ref, d_ref, wglu_ref, bglu_ref,
               on_ref, o_ref, hr_out_ref, hi_out_ref, bu_ref, hre_ref, him_ref, cr_ref, ci_ref, *, tc, nct):
    c = pl.program_id(1)
    S = S5_STATE
    LW = 128

    @pl.when(c == 0)
    def _():
        cr_ref[...] = h0r_ref[0]
        ci_ref[...] = h0i_ref[0]

    u = u_ref[...]
    NK = GW // 128
    SW = S // NK
    for kc in range(NK):
        ukc = _bf(u[:, kc * 128:(kc + 1) * 128])
        for half in range(2):
            cs = slice(half * S + kc * SW, half * S + (kc + 1) * SW)
            res = jnp.dot(ukc, bblk_ref[kc * 128:(kc + 1) * 128, cs], preferred_element_type=F32)
            for j in range(SW // LW):
                bu_ref[(half * S + kc * SW) // LW + j] = res[:, j * LW:(j + 1) * LW]

    nt = tc // 8
    trow = _iota2(nt, LW, 0)

    def lane_block(lc, _):
        pw = lambda n: (pwr_ref[lc, n:n + 1, :], pwi_ref[lc, n:n + 1, :])
        ar, ai = pw(0)
        xr = xi = None
        for r in range(8):
            tr = pl.ds(r, nt, stride=8)
            nr, ni = bu_ref[lc, tr, :], bu_ref[S // LW + lc, tr, :]
            if r:
                sr, si = _cmul(ar, ai, xr, xi)
                nr, ni = nr + sr, ni + si
            xr, xi = nr, ni
            hre_ref[lc, tr, :] = xr
            him_ref[lc, tr, :] = xi
        k = 1
        while k < nt:
            pr, pi = pw(8 + k - 1)
            sr, si = _cmul(jnp.where(trow >= k, pr, 0.0), jnp.where(trow >= k, pi, 0.0),
                           pltpu.roll(xr, k, 0), pltpu.roll(xi, k, 0))
            xr, xi = xr + sr, xi + si
            k *= 2
        kr, ki = cr_ref[lc], ci_ref[lc]
        sr, si = _cmul(pwr_ref[lc, 8:8 + nt, :], pwi_ref[lc, 8:8 + nt, :], kr, ki)
        xr, xi = xr + sr, xi + si
        cr_ref[lc] = xr[nt - 1:nt, :]
        ci_ref[lc] = xi[nt - 1:nt, :]
        pr_ = jnp.where(trow == 0, kr, pltpu.roll(xr, 1, 0))
        pi_ = jnp.where(trow == 0, ki, pltpu.roll(xi, 1, 0))
        for r in range(8):
            tr = pl.ds(r, nt, stride=8)
            sr, si = _cmul(*pw(r), pr_, pi_)
            hre_ref[lc, tr, :] = hre_ref[lc, tr, :] + sr
            him_ref[lc, tr, :] = him_ref[lc, tr, :] + si
        return 0

    def two_lane_blocks(i, _):
        lane_block(2 * i, 0)
        return lane_block(2 * i + 1, 0)

    lax.fori_loop(0, S // LW // 2, two_lane_blocks, 0)

    ys = []
    for kc in range(NK):
        ss, os_ = slice(kc * SW, (kc + 1) * SW), slice(kc * 128, (kc + 1) * 128)
        blocks = range(kc * SW // LW, (kc + 1) * SW // LW)
        hr = jnp.concatenate([_bf(hre_ref[j]) for j in blocks], axis=1)
        hi = jnp.concatenate([_bf(him_ref[j]) for j in blocks], axis=1)
        ys.append(jnp.dot(hr, cre_ref[ss, os_], preferred_element_type=F32)
                  - jnp.dot(hi, cim_ref[ss, os_], preferred_element_type=F32))
    y = jnp.concatenate(ys, axis=1) + d_ref[...] * u
    g = jax.nn.gelu(y)
    out = g * jax.nn.sigmoid(_dot(g, wglu_ref[...]) + bglu_ref[...])
    o_ref[...] = _rms_rows(out, on_ref[...])

    @pl.when(c == nct - 1)
    def _():
        hr_out_ref[0] = cr_ref[...]
        hi_out_ref[0] = ci_ref[...]


def _s5_call(z, row0, nseq, t, h0_re, h0_im, pw_re, pw_im, bblk, c_re, c_im, d_skip, w_glu, b_glu, out_norm,
             col_blk, tc):
    nct = t // tc
    S = S5_STATE
    NB = S // 128
    blk0 = row0 // tc
    const2 = lambda b, c: (0, 0)
    state = pl.BlockSpec((1, NB, 1, 128), lambda b, c: (b, 0, 0, 0))
    powers = pl.BlockSpec((NB, S5_POWERS, 128), lambda b, c: (0, 0, 0))
    lane_blocks = lambda p: jnp.transpose(p.reshape(S5_POWERS, NB, 128), (1, 0, 2))
    outs = pl.pallas_call(
        functools.partial(_s5_kernel, tc=tc, nct=nct),
        grid=(nseq, nct),
        in_specs=[pl.BlockSpec((tc, GW), lambda b, c: (blk0 + b * nct + c, col_blk)),
                  state, state, powers, powers,
                  pl.BlockSpec((GW, 2 * S), const2),
                  pl.BlockSpec((S, GW), const2), pl.BlockSpec((S, GW), const2),
                  pl.BlockSpec((1, GW), const2),
                  pl.BlockSpec((GW, GW), const2), pl.BlockSpec((1, GW), const2), pl.BlockSpec((1, GW), const2)],
        out_specs=[pl.BlockSpec((tc, GW), lambda b, c: (b * nct + c, 0)), state, state],
        out_shape=[jax.ShapeDtypeStruct((nseq * t, GW), F32),
                   jax.ShapeDtypeStruct((nseq, NB, 1, 128), F32), jax.ShapeDtypeStruct((nseq, NB, 1, 128), F32)],
        scratch_shapes=[pltpu.VMEM((2 * NB, tc, 128), F32), pltpu.VMEM((NB, tc, 128), F32),
                        pltpu.VMEM((NB, tc, 128), F32),
                        pltpu.VMEM((NB, 1, 128), F32), pltpu.VMEM((NB, 1, 128), F32)],
        compiler_params=_cparams("parallel", "arbitrary"),
        name="s5",
    )(z, h0_re.reshape(nseq, NB, 1, 128), h0_im.reshape(nseq, NB, 1, 128), lane_blocks(pw_re), lane_blocks(pw_im),
      bblk, c_re, c_im, d_skip, w_glu, b_glu, out_norm)
    return outs[0], outs[1].reshape(nseq, S), outs[2].reshape(nseq, S)


def _sbprep_kernel(q_ref, k_ref, v_ref, gmat_ref, qg_ref, kg_ref, *refs, transposed):
    gmat = gmat_ref[...]

    def head_rms(x, gain):
        ms = _dot2_exact_rhs(x * x, gmat) * (1.0 / HD)
        return x * lax.rsqrt(ms + RMS_EPS) * gain

    if transposed:
        qb_ref, kb_ref, vb_ref, kt_ref, vt_ref = refs[-5:]
    else:
        qb_ref, kb_ref, vb_ref, kn_ref = refs
    qb_ref[...] = _bf(head_rms(q_ref[...], qg_ref[...]) * (HD ** -0.5))
    kn = head_rms(k_ref[...], kg_ref[...])
    v = v_ref[...]
    kb_ref[...] = _bf(kn)
    vb_ref[...] = _bf(v)
    if transposed:
        kt_ref[...] = kn.T
        vt_ref[...] = v.T
    else:
        kn_ref[...] = kn


def _sbprep_call(z, row0, nseq, t, gmat, q_gain, k_gain, layer=0, depth=0, kt_prev=None, vt_prev=None):
    tm = min(t, 512)
    nct = t // tm
    blk0 = row0 // tm
    zcol = lambda j: pl.BlockSpec((tm, GW), lambda b, c: (blk0 + b * nct + c, j))
    const = lambda shp: pl.BlockSpec(shp, lambda b, c: (0,) * len(shp))
    row = pl.BlockSpec((tm, GW), lambda b, c: (b * nct + c, 0))
    bf_shape = jax.ShapeDtypeStruct((nseq * t, GW), BF16)
    in_specs = [zcol(3), zcol(4), zcol(5), const((GW, GW)), const((1, GW)), const((1, GW))]
    args = [z, z, z, gmat, q_gain, k_gain]
    aliases = {}
    if depth:
        tspec = pl.BlockSpec((None, None, GW, tm), lambda b, c: (layer, b, 0, c))
        tshape = jax.ShapeDtypeStruct((depth, nseq, GW, t), F32)
        out_specs, out_shape = [row, row, row, tspec, tspec], [bf_shape, bf_shape, bf_shape, tshape, tshape]
        if kt_prev is not None:
            in_specs += [pl.BlockSpec(memory_space=pl.ANY)] * 2
            args += [kt_prev, vt_prev]
            aliases = {6: 3, 7: 4}
    else:
        out_specs = [row, row, row, row]
        out_shape = [bf_shape, bf_shape, bf_shape, jax.ShapeDtypeStruct((nseq * t, GW), F32)]
    return pl.pallas_call(
        functools.partial(_sbprep_kernel, transposed=bool(depth)),
        grid=(nseq, nct),
        in_specs=in_specs,
        out_specs=out_specs,
        out_shape=out_shape,
        input_output_aliases=aliases,
        compiler_params=_cparams("parallel", "parallel"),
        name="sbprep",
    )(*args)


def _sb_tiles(qhs, k2s, v2s, carries, umat, mask):
    qk = [functools.partial(lax.dot_general, qhs[p], k2s[p], _NT, preferred_element_type=F32)
          for p in range(len(qhs))]
    pv = [functools.partial(lambda w, v2: jnp.dot(w, v2, preferred_element_type=F32), v2=v2s[p])
          for p in range(len(qhs))]
    return _sb_stages(qk, pv, carries, umat, mask)


def _sb_stages(qk, pv, carries, umat, mask, skew=1):
    d = functools.partial(jnp.dot, preferred_element_type=F32)
    n = len(qk)
    pvs, tots = [None] * n, [None] * n

    def pair_stages(p):
        z = qk[p]()
        yield
        lk = -(jnp.maximum(z, 0.0) + jnp.log(1.0 + jnp.exp(-jnp.abs(z))))
        if mask is not None:
            lk = jnp.where(mask, lk, 0.0)
        aft = d(_bf(lk), umat)
        tots[p] = jnp.broadcast_to(aft[:, 0:1] + lk[:, 0:1], (lk.shape[0], 2 * HD))
        yield
        e = z + lk + aft
        if carries[p] is not None:
            reps = e.shape[1] // (2 * HD)
            e = e + (carries[p] if reps == 1 else jnp.concatenate([carries[p]] * reps, axis=1))
        w = jnp.exp(e)
        if mask is not None:
            w = jnp.where(mask, w, 0.0)
        pvs[p] = pv[p](_bf(w))

    _round_robin([pair_stages(p) for p in range(n)], skew)
    return pvs, tots


def _sb_kernel(q_ref, kd_ref, vd_ref, kp_ref, vp_ref, on_ref, o_ref, qh_ref, acc_ref, carry_ref,
               *, tq, tk, npast, past_from_grid):
    PW = 2 * HD
    npairs = GW // PW
    m0 = _iota2(1, PW, 1) < HD
    row = _iota2(2 * tq, tq, 0)
    causal = _iota2(2 * tq, tq, 1) < jnp.where(row >= tq, row - tq, row)

    ud = jnp.where(_iota2(tq, tq, 0) > _iota2(tq, tq, 1), 1.0, 0.0).astype(BF16)
    up = jnp.where(_iota2(tk, tk, 0) > _iota2(tk, tk, 1), 1.0, 0.0).astype(BF16)
    pairs = [slice(p * PW, (p + 1) * PW) for p in range(npairs)]

    qhs = [_bf(_hat(q_ref[:, sl], m0)) for sl in pairs]
    pvs, tots = _sb_tiles(qhs, [_bf(kd_ref[:, sl]) for sl in pairs], [_bf(vd_ref[:, sl]) for sl in pairs],
                          [None] * npairs, ud, causal)
    for p in range(npairs):
        qh_ref[p] = qhs[p]
        acc_ref[p] = pvs[p]
        carry_ref[p] = tots[p]

    nblk = pl.program_id(1) * (tq // tk) if past_from_grid else npast

    def body(i, _):
        j = nblk - 1 - i
        r0 = pl.multiple_of(j * tk, tk)
        pvs, tots = _sb_tiles([qh_ref[p] for p in range(npairs)],
                              [_bf(kp_ref[pl.ds(r0, tk), sl]) for sl in pairs],
                              [_bf(vp_ref[pl.ds(r0, tk), sl]) for sl in pairs],
                              [carry_ref[p] for p in range(npairs)], up, None)
        for p in range(npairs):
            acc_ref[p] += pvs[p]
            carry_ref[p] += tots[p]
        return 0

    lax.fori_loop(0, nblk, body, 0)

    outs = []
    ss = jnp.zeros((tq, 1), F32)
    for p in range(npairs):
        a = acc_ref[p]
        o = jnp.where(m0, a[:tq], a[tq:])
        ss = ss + jnp.sum(o * o, axis=-1, keepdims=True)
        outs.append(o)
    scale = lax.rsqrt(ss * (1.0 / GW) + RMS_EPS)
    for p in range(npairs):
        sl = slice(p * PW, (p + 1) * PW)
        o_ref[:, sl] = outs[p] * scale * on_ref[:, sl]


def _sb_call(qn, kn, z, kpast, vpast, row0, nseq, t, tq, tk, past_len, v_col_blk, out_norm, layer=0):
    nq = t // tq
    blk0 = row0 // tq
    npairs = GW // (2 * HD)
    if kpast is None:
        kp_arr, vp_arr = kn, z
        sblk = row0 // t
        kp_spec = pl.BlockSpec((t, GW), lambda b, i: (sblk + b, 0))
        vp_spec = pl.BlockSpec((t, GW), lambda b, i: (sblk + b, v_col_blk))
        npast, from_grid = 0, True
    else:
        kp_arr, vp_arr = kpast, vpast
        kp_spec = pl.BlockSpec((None, None, past_len, GW), lambda b, i: (layer, b, 0, 0))
        vp_spec = kp_spec
        npast, from_grid = past_len // tk, False
    qrow = lambda b, i: (blk0 + b * nq + i, 0)
    return pl.pallas_call(
        functools.partial(_sb_kernel, tq=tq, tk=tk, npast=npast, past_from_grid=from_grid),
        grid=(nseq, nq),
        in_specs=[pl.BlockSpec((tq, GW), qrow), pl.BlockSpec((tq, GW), qrow),
                  pl.BlockSpec((tq, GW), lambda b, i: (blk0 + b * nq + i, v_col_blk)),
                  kp_spec, vp_spec, pl.BlockSpec((1, GW), lambda b, i: (0, 0))],
        out_specs=pl.BlockSpec((tq, GW), lambda b, i: (b * nq + i, 0)),
        out_shape=jax.ShapeDtypeStruct((nseq * t, GW), F32),
        scratch_shapes=[pltpu.VMEM((npairs, 2 * tq, 2 * HD), BF16), pltpu.VMEM((npairs, 2 * tq, 2 * HD), F32),
                        pltpu.VMEM((npairs, 2 * tq, 2 * HD), F32)],
        compiler_params=_cparams("parallel", "arbitrary"),
        name="sb_attn",
    )(qn, kn, z, kp_arr, vp_arr, out_norm)


def _sbs_kernel(q_ref, kd_ref, vd_ref, kt_ref, vt_ref, on_ref, o_ref, qh_ref, acc_ref, carry_ref,
                *, tq, tk, nblk, nchunks):
    c = pl.program_id(1)
    nh = GW // HD
    npairs = nh // 2
    hs = [slice(h * HD, (h + 1) * HD) for h in range(nh)]
    stack = functools.partial(jnp.concatenate, axis=0)
    mm = functools.partial(lax.dot_general, preferred_element_type=F32)

    @pl.when(c == 0)
    def _():
        row = _iota2(2 * tq, tq, 0)
        causal = _iota2(2 * tq, tq, 1) < jnp.where(row >= tq, row - tq, row)
        ud = jnp.where(_iota2(tq, tq, 0) > _iota2(tq, tq, 1), 1.0, 0.0).astype(BF16)
        qs = [_bf(q_ref[:, s]) for s in hs]
        kd = [_bf(kd_ref[:, s]) for s in hs]
        vd = [_bf(vd_ref[:, s]) for s in hs]
        for h in range(nh):
            qh_ref[h] = qs[h]
        qk = [lambda p=p: stack([mm(qs[2 * p], kd[2 * p], _NT), mm(qs[2 * p + 1], kd[2 * p + 1], _NT)])
              for p in range(npairs)]
        pv = [lambda w, p=p: stack([mm(w[:tq], vd[2 * p], _NN), mm(w[tq:], vd[2 * p + 1], _NN)])
              for p in range(npairs)]
        pvs, tots = _sb_stages(qk, pv, [None] * npairs, ud, causal, skew=0)
        for p in range(npairs):
            acc_ref[p] = pvs[p]
            carry_ref[p] = tots[p]

    up = jnp.where(_iota2(tk, tk, 0) > _iota2(tk, tk, 1), 1.0, 0.0).astype(BF16)
    for jb in reversed(range(nblk)):
        ks = slice(jb * tk, (jb + 1) * tk)
        qk = [lambda p=p: stack([mm(qh_ref[2 * p], _bf(kt_ref[2 * p, :, ks]), _NN),
                                 mm(qh_ref[2 * p + 1], _bf(kt_ref[2 * p + 1, :, ks]), _NN)])
              for p in range(npairs)]
        pv = [lambda w, p=p: stack([mm(w[:tq], _bf(vt_ref[2 * p, :, ks]), _NT),
                                    mm(w[tq:], _bf(vt_ref[2 * p + 1, :, ks]), _NT)])
              for p in range(npairs)]
        pvs, tots = _sb_stages(qk, pv, [carry_ref[p] for p in range(npairs)], up, None, skew=0)
        for p in range(npairs):
            acc_ref[p] += pvs[p]
            carry_ref[p] += tots[p]

    @pl.when(c == nchunks - 1)
    def _():
        heads = []
        for p in range(npairs):
            a = acc_ref[p]
            heads += [a[:tq], a[tq:]]
        o_ref[...] = _rms_rows(jnp.concatenate(heads, axis=1), on_ref[...])


def _sbs_call(qb, kb, vb, cache_kt, cache_vt, row0, nseq, tq, tk, tkc, out_norm, layer):
    past_len = cache_kt.shape[-1]
    nchunks = past_len // tkc
    blk0 = row0 // tq
    nh = GW // HD
    qrow = pl.BlockSpec((tq, GW), lambda b, c: (blk0 + b, 0))
    cspec = pl.BlockSpec((None, None, nh, HD, tkc), lambda b, c: (layer, b, 0, 0, nchunks - 1 - c))
    return pl.pallas_call(
        functools.partial(_sbs_kernel, tq=tq, tk=tk, nblk=tkc // tk, nchunks=nchunks),
        grid=(nseq, nchunks),
        in_specs=[qrow, qrow, qrow, cspec, cspec, pl.BlockSpec((1, GW), lambda b, c: (0, 0))],
        out_specs=pl.BlockSpec((tq, GW), lambda b, c: (b, 0)),
        out_shape=jax.ShapeDtypeStruct((nseq * tq, GW), F32),
        scratch_shapes=[pltpu.VMEM((nh, tq, HD), BF16), pltpu.VMEM((nh // 2, 2 * tq, HD), F32),
                        pltpu.VMEM((nh // 2, 2 * tq, 2 * HD), F32)],
        compiler_params=_cparams("parallel", "arbitrary"),
        name="sb_attn_cache",
    )(qb, kb, vb, cache_kt, cache_vt, out_norm)


def _rwkv_prep_kernel(zr_ref, zk_ref, zv_ref, zwa_ref, zg_ref, first_ref, mu_ref, w0_ref, w2_ref, a0_ref, a2_ref,
                      g2_ref, kk_ref, ka_ref, rk_ref, gmat_ref,
                      r_out, lw_out, k_out, v_out, kk_out, b_out, g_out, bonus_out, prev_ref, *, tm, wr):
    c = pl.program_id(1)
    cols = prev_ref.shape[1]

    @pl.when(c == 0)
    def _():
        prev_ref[...] = first_ref[0]

    first_row = _iota2(tm, 1, 0) == 0

    def shifted(ref, lo, width):
        x = ref[...]
        prev = jnp.where(first_row, prev_ref[:, lo:lo + width], pltpu.roll(x, 1, 0))
        return x + (prev - x) * mu_ref[:, lo:lo + width]

    r = shifted(zr_ref, 0, GW)
    k = shifted(zk_ref, GW, GW)
    v = shifted(zv_ref, 2 * GW, GW)
    wa = shifted(zwa_ref, 3 * GW, 2 * wr)
    gl = shifted(zg_ref, 3 * GW + 2 * wr, cols - 3 * GW - 2 * wr)
    for ref, lo in ((zr_ref, 0), (zk_ref, GW), (zv_ref, 2 * GW), (zwa_ref, 3 * GW), (zg_ref, 3 * GW + 2 * wr)):
        prev_ref[:, lo:lo + ref.shape[1]] = ref[tm - 1:tm, :]

    xw = w0_ref[...] + _dot(jnp.tanh(wa), w2_ref[...])
    w_log = -(jnp.maximum(-xw, 0.0) + jnp.log(1.0 + jnp.exp(-jnp.abs(xw)))) - 0.5
    a = jax.nn.sigmoid(a0_ref[...] + _dot(wa, a2_ref[...]))
    g_out[...] = _dot(jax.nn.sigmoid(gl), g2_ref[...])
    gmat = gmat_ref[...]
    kk = k * kk_ref[...]
    kk = kk / jnp.maximum(jnp.sqrt(_dot2_exact_rhs(kk * kk, gmat)), 1e-12)
    k = k * (1.0 + (a - 1.0) * ka_ref[...])
    r_out[...] = r
    lw_out[...] = -jnp.exp(w_log)
    k_out[...] = k
    v_out[...] = v
    kk_out[...] = kk
    b_out[...] = kk * a
    bonus_out[...] = _dot2_exact_rhs(r * k * rk_ref[...], gmat) * v


def _rwkv_prep_call(z, row0, nseq, t, tm, first, mu, w0, w2p, a0, a2p, g2, k_k, k_a, r_k, gmat, wr):
    nct = t // tm
    blk0 = row0 // tm
    cols = mu.shape[-1]
    zrow = lambda blkw, off: pl.BlockSpec((tm, blkw), lambda b, c: (blk0 + b * nct + c, off))
    c0 = (z.shape[1] - cols)
    const = lambda shp: pl.BlockSpec(shp, lambda b, c: (0,) * len(shp))
    orow = pl.BlockSpec((tm, GW), lambda b, c: (b * nct + c, 0))
    return pl.pallas_call(
        functools.partial(_rwkv_prep_kernel, tm=tm, wr=wr),
        grid=(nseq, nct),
        in_specs=[zrow(GW, c0 // GW), zrow(GW, c0 // GW + 1), zrow(GW, c0 // GW + 2),
                  zrow(2 * wr, (c0 + 3 * GW) // (2 * wr)), zrow(cols - 3 * GW - 2 * wr, (c0 + 3 * GW) // (2 * wr) + 1),
                  pl.BlockSpec((1, 1, cols), lambda b, c: (b, 0, 0)),
                  const((1, cols)), const((1, GW)), const((2 * wr, GW)), const((1, GW)), const((2 * wr, GW)),
                  const((cols - 3 * GW - 2 * wr, GW)), const((1, GW)), const((1, GW)), const((1, GW)),
                  const((GW, GW))],
        out_specs=[orow] * 8,
        out_shape=[jax.ShapeDtypeStruct((nseq * t, GW), F32)] * 8,
        scratch_shapes=[pltpu.VMEM((1, cols), F32)],
        compiler_params=_cparams("parallel", "arbitrary"),
        name="rwkv_prep",
    )(z, z, z, z, z, first.reshape(nseq, 1, cols), mu, w0, w2p, a0, a2p, g2, k_k, k_a, r_k, gmat)


def _hat(x, m0):
    return jnp.concatenate([jnp.where(m0, x, 0.0), jnp.where(m0, 0.0, x)], axis=0)


def _rwkv_chunk_kernel(r_ref, lw_ref, k_ref, v_ref, kk_ref, b_ref, rt_out, p3_out, m_out, n_out, *, group, nch):
    C = RWKV_CHUNK
    C2 = 2 * C
    lane = _iota2(1, 2 * HD, 1)
    m0 = lane < HD
    ri = _iota2(C2, C2, 0)
    ci = _iota2(C2, C2, 1)
    same = (ri < C) == (ci < C)
    strict = jnp.logical_and(same, ri > ci)
    incl = jnp.logical_and(same, ri >= ci)
    ltri = jnp.where(_iota2(C, C, 0) >= _iota2(C, C, 1), 1.0, 0.0).astype(BF16)
    eye = _iota2(2 * HD, 2 * HD, 0) == _iota2(2 * HD, 2 * HD, 1)
    eye2 = ri == ci

    def pair_stages(ch, p):
        sl = slice(p * 2 * HD, (p + 1) * 2 * HD)
        rows = slice(ch * C, (ch + 1) * C)
        lw = lw_ref[rows, sl]
        cl = _dot2_exact_lhs(ltri, lw)
        yield
        clast = cl[C - 1:C, :]
        kkt = _hat(kk_ref[rows, sl] * jnp.exp(cl - lw), m0)
        rt = _hat(r_ref[rows, sl] * jnp.exp(cl), m0)
        einv = jnp.exp(-cl)
        kb = _split(_hat(k_ref[rows, sl] * einv, m0))
        bb = _split(_hat(b_ref[rows, sl] * einv, m0))
        efin = jnp.exp(clast - cl)
        kh = _split(_hat(k_ref[rows, sl] * efin, m0))
        bh = _split(_hat(b_ref[rows, sl] * efin, m0))
        vh = _split(_hat(v_ref[rows, sl], m0))
        lhs = _split(jnp.concatenate([kkt, rt], axis=0))
        gk = _dot3s(lhs, kb, _NT)
        gb = _dot3s(lhs, bb, _NT)
        yield
        a_kk = jnp.where(strict, gk[:C2], 0.0)
        a_rk = _split(jnp.where(incl, gk[C2:], 0.0))
        a_kb = jnp.where(strict, gb[:C2], 0.0)
        a_rb = _split(jnp.where(incl, gb[C2:], 0.0))
        p1 = _dot3s(_split(a_kk), vh)
        tinv = jnp.where(eye2, 1.0, 0.0) - a_kb
        lp = _split(a_kb)
        n = 2
        while n < C:
            yield
            lpf = _dot3s(lp, lp)
            lp = _split(lpf)
            tinv = tinv + _dot3s(_split(tinv), lp)
            n *= 2
        yield
        x = _dot3s(_split(tinv), _split(jnp.concatenate([kkt, p1], axis=1)))
        yield
        xs = _split(x)
        kt = (xs[0][:, :2 * HD], xs[1][:, :2 * HD])
        p2 = (xs[0][:, 2 * HD:], xs[1][:, 2 * HD:])
        y2 = _dot3s(a_rb, xs)
        p3 = _dot3s(a_rk, vh)
        mm = _dot3s(kt, bh, _TN)
        nn = _dot3s(vh, kh, _TN) - _dot3s(p2, bh, _TN)
        yield
        rt_out[ch, p] = rt - y2[:, :2 * HD]
        p3_out[ch, p] = p3 - y2[:, 2 * HD:]
        m_out[ch, p] = jnp.where(eye, jnp.exp(clast), 0.0) - mm
        n_out[ch, p] = nn

    npairs = GW // (2 * HD)
    for ch in range(nch):
        for p0 in range(0, npairs, group):
            _round_robin([pair_stages(ch, p) for p in range(p0, p0 + group)])


def _rwkv_chunk_call(r, lw, k, v, kk, b, nch=2):
    n = r.shape[0]
    C = RWKV_CHUNK
    nc = n // C
    npairs = GW // (2 * HD)
    row = pl.BlockSpec((nch * C, GW), lambda i: (i, 0))
    blk = lambda rows: pl.BlockSpec((nch, npairs, rows, 2 * HD), lambda i: (i, 0, 0, 0))
    shp = lambda rows: jax.ShapeDtypeStruct((nc, npairs, rows, 2 * HD), F32)
    return pl.pallas_call(
        functools.partial(_rwkv_chunk_kernel, group=4, nch=nch),
        grid=(nc // nch,),
        in_specs=[row] * 6,
        out_specs=[blk(2 * C), blk(2 * C), blk(2 * HD), blk(2 * HD)],
        out_shape=[shp(2 * C), shp(2 * C), shp(2 * HD), shp(2 * HD)],
        compiler_params=_cparams("parallel"),
        name="rwkv_chunk",
    )(r, lw, k, v, kk, b)


def _rwkv_seq_kernel(rt_ref, p3_ref, m_ref, n_ref, s0_ref, g_ref, bonus_ref, lnw_ref, lnb_ref, gmat_ref,
                     o_ref, s_out_ref, s_ref, *, nct, gsz):
    c = pl.program_id(1)
    C = RWKV_CHUNK

    @pl.when(c == 0)
    def _():
        s_ref[...] = s0_ref[...]

    gm = gmat_ref[0:2 * HD, 0:2 * HD]
    npairs = GW // (2 * HD)
    pairs = [slice(p * 2 * HD, (p + 1) * 2 * HD) for p in range(npairs)]
    sp = [(s, p) for s in range(gsz) for p in range(npairs)]
    ss = [_split(s_ref[s, p]) for s, p in sp]
    yhs = [_dot3s(_split(rt_ref[s, p]), ss[i], _NT) + p3_ref[s, p] for i, (s, p) in enumerate(sp)]
    snew = [_dot3s(ss[i], _split(m_ref[s, p])) + n_ref[s, p] for i, (s, p) in enumerate(sp)]
    ys = [yh[:C] + yh[C:] for yh in yhs]
    means = [_dot2_exact_rhs(y, gm) * (1.0 / HD) for y in ys]
    ds = [y - m for y, m in zip(ys, means)]
    vrs = [_dot2_exact_rhs(d * d, gm) * (1.0 / HD) for d in ds]
    for i, (s, p) in enumerate(sp):
        sl = pairs[p]
        yn = ds[i] * lax.rsqrt(vrs[i] + GN_EPS) * lnw_ref[:, sl] + lnb_ref[:, sl]
        o_ref[s, :, sl] = (yn + bonus_ref[s, :, sl]) * g_ref[s, :, sl]
        s_ref[s, p] = snew[i]

    @pl.when(c == nct - 1)
    def _():
        s_out_ref[...] = s_ref[...]


def _rwkv_seq_call(rt, p3, m, n, nseq, nct, gsz, s0, g, bonus, lnx_w, lnx_b, gmat):
    C = RWKV_CHUNK
    npairs = GW // (2 * HD)
    c5 = lambda a: a.reshape(nseq, nct, *a.shape[1:])
    cblk = lambda rows: pl.BlockSpec((gsz, None, npairs, rows, 2 * HD), lambda b, c: (b, c, 0, 0, 0))
    sblk = pl.BlockSpec((gsz, npairs, 2 * HD, 2 * HD), lambda b, c: (b, 0, 0, 0))
    rowblk = pl.BlockSpec((gsz, C, GW), lambda b, c: (b, c, 0))
    const = lambda shp: pl.BlockSpec(shp, lambda b, c: (0,) * len(shp))
    od, s_out = pl.pallas_call(
        functools.partial(_rwkv_seq_kernel, nct=nct, gsz=gsz),
        grid=(nseq // gsz, nct),
        in_specs=[cblk(2 * C), cblk(2 * C), cblk(2 * HD), cblk(2 * HD), sblk, rowblk, rowblk,
                  const((1, GW)), const((1, GW)), const((GW, GW))],
        out_specs=[rowblk, sblk],
        out_shape=[jax.ShapeDtypeStruct((nseq, nct * C, GW), F32),
                   jax.ShapeDtypeStruct((nseq, npairs, 2 * HD, 2 * HD), F32)],
        scratch_shapes=[pltpu.VMEM((gsz, npairs, 2 * HD, 2 * HD), F32)],
        compiler_params=_cparams("parallel", "arbitrary"),
        name="rwkv_seq",
    )(c5(rt), c5(p3), c5(m), c5(n), s0, g.reshape(nseq, nct * C, GW), bonus.reshape(nseq, nct * C, GW),
      lnx_w, lnx_b, gmat)
    return od.reshape(nseq * nct * C, GW), s_out


def _pair_states(s):
    b, h = s.shape[:2]
    s = s.reshape(b, h // 2, 2, HD, HD)
    z = jnp.zeros_like(s[:, :, 0])
    top = jnp.concatenate([s[:, :, 0], z], axis=-1)
    bot = jnp.concatenate([z, s[:, :, 1]], axis=-1)
    return jnp.concatenate([top, bot], axis=-2)


def _unpair_states(s2):
    b, hp = s2.shape[:2]
    return jnp.stack([s2[:, :, :HD, :HD], s2[:, :, HD:, HD:]], axis=2).reshape(b, 2 * hp, HD, HD)


def _forward(xp, xs, cache_k, cache_v, s5_re0, s5_im0, wkv0, shift0, W):
    Bp, Tp, D = xp.shape
    Bs, Ts, _ = xs.shape
    depth = W['ln_ffn1'].shape[0]
    past_len = cache_k.shape[2]
    Np, Ns = Bp * Tp, Bs * Ts
    n_heads_d = GW // HD
    wr = W['rwkv_w2'].shape[1]
    dcols = W['rwkv_mu'].shape[-1]

    x = jnp.concatenate([xp.reshape(Np, D), xs.reshape(Ns, D)], axis=0)
    cache_kt = jnp.transpose(cache_k, (0, 1, 3, 4, 2))
    cache_vt = jnp.transpose(cache_v, (0, 1, 3, 4, 2))
    wbf = {k: _bf(W[k]) for k in ('w_ffn1_gate', 'w_ffn1_up', 'w_ffn1_down', 'w_in', 'w_out',
                                  'w_ffn2_gate', 'w_ffn2_up', 'w_ffn2_down')}
    hid = lax.broadcasted_iota(jnp.int32, (GW, GW), 0) // HD
    gmat = (hid == jnp.transpose(hid)).astype(BF16)
    grp = lax.broadcasted_iota(jnp.int32, (GW, S5_STATE), 0) // (GW // 32) == \
        lax.broadcasted_iota(jnp.int32, (GW, S5_STATE), 1) // (S5_STATE // 32)
    zeros_first = jnp.zeros((Bp, dcols), F32)
    zeros_s5 = jnp.zeros((Bp, S5_STATE), F32)
    zeros_wkv = jnp.zeros((Bp, n_heads_d // 2, 2 * HD, 2 * HD), F32)

    h = _rms_call(x, W['ln_ffn1'][0][None])
    outs = {k: [] for k in ('ks', 'vs', 's5rp', 's5ip', 's5rs', 's5is', 'wkvp', 'wkvs', 'shp', 'shs', 'gv')}
    kt_buf = vt_buf = None
    for l in range(depth):
        g1 = lambda name: W[name][l][None]
        x, h = _ffn_call(x, h, wbf['w_ffn1_gate'], wbf['w_ffn1_up'], wbf['w_ffn1_down'], g1('ln_mix'), l)
        z = _inproj_call(h, wbf['w_in'], l)

        oa_p, _ = _gmlp_call(z, 0, Np, min(Tp, GMLP_CHUNK), g1('gmlp_v_norm'), W['gmlp_ws'][l], W['gmlp_b'][l],
                             g1('out_norm_a'))
        oa_s, gv_s = _gmlp_call(z, Np, Ns, min(Ts, GMLP_CHUNK), g1('gmlp_v_norm'), W['gmlp_ws'][l], W['gmlp_b'][l],
                                g1('out_norm_a'))

        pw_re, pw_im, bb_re, bb_im = _s5_params(W['s5_lam_re'][l], W['s5_lam_im'][l], W['s5_log_dt'][l],
                                                W['s5_b_re'][l], W['s5_b_im'][l])
        bblk = jnp.concatenate([jnp.where(grp, jnp.tile(bb_re, (32, 1)), 0.0),
                                jnp.where(grp, jnp.tile(bb_im, (32, 1)), 0.0)], axis=1).astype(BF16)
        grp_t = jnp.transpose(grp)
        cre = jnp.where(grp_t, jnp.tile(jnp.transpose(W['s5_c_re'][l], (0, 2, 1)).reshape(S5_STATE, -1), (1, 32)),
                        0.0).astype(BF16)
        cim = jnp.where(grp_t, jnp.tile(jnp.transpose(W['s5_c_im'][l], (0, 2, 1)).reshape(S5_STATE, -1), (1, 32)),
                        0.0).astype(BF16)
        s5_args = (pw_re, pw_im, bblk, cre, cim, g1('s5_d'), W['s5_w_glu'][l], g1('s5_b_glu'), g1('out_norm_b'))
        ob_p, s5rp, s5ip = _s5_call(z, 0, Bp, Tp, zeros_s5, zeros_s5, *s5_args, col_blk=2, tc=min(Tp, S5_TC))
        ob_s, s5rs, s5is = _s5_call(z, Np, Bs, Ts, s5_re0[l].reshape(Bs, S5_STATE), s5_im0[l].reshape(Bs, S5_STATE),
                                    *s5_args, col_blk=2, tc=min(Ts, S5_TC))

        qg = jnp.tile(W['sb_q_norm'][l], GW // HD)[None]
        kg = jnp.tile(W['sb_k_norm'][l], GW // HD)[None]
        qb_p, kb_p, vb_p, kt_buf, vt_buf = _sbprep_call(z, 0, Bp, Tp, gmat, qg, kg, l, depth, kt_buf, vt_buf)
        qb_s, kb_s, vb_s, kn_s = _sbprep_call(z, Np, Bs, Ts, gmat, qg, kg)
        oc_p = _sb_call(qb_p, kb_p, vb_p, None, None, 0, Bp, Tp, min(Tp, SB_TQ), min(Tp, SB_TK), 0, 0,
                        g1('out_norm_c'))
        oc_s = _sbs_call(qb_s, kb_s, vb_s, cache_kt, cache_vt, 0, Bs, Ts, min(past_len, SB_TK_SAMPLE),
                         min(past_len, SB_CACHE_CHUNK), g1('out_norm_c'), l)

        w2p = jnp.concatenate([W['rwkv_w2'][l], jnp.zeros_like(W['rwkv_a2'][l])], axis=0)
        a2p = jnp.concatenate([jnp.zeros_like(W['rwkv_w2'][l]), W['rwkv_a2'][l]], axis=0)
        rk = W['rwkv_r_k'][l].reshape(1, GW)
        prep_args = (g1('rwkv_mu'), g1('rwkv_w0'), w2p, g1('rwkv_a0'), a2p, W['rwkv_g2'][l], g1('rwkv_k_k'),
                     g1('rwkv_k_a'), rk, gmat, wr)
        pp = _rwkv_prep_call(z, 0, Bp, Tp, min(Tp, 512), zeros_first, *prep_args)
        ps = _rwkv_prep_call(z, Np, Bs, Ts, Ts, shift0[l], *prep_args)
        seq_args = (g1('rwkv_lnx_w'), g1('rwkv_lnx_b'), gmat)
        od_p, wkv_p = _rwkv_seq_call(*_rwkv_chunk_call(*pp[:6]), Bp, Tp // RWKV_CHUNK, min(Bp, 2), zeros_wkv,
                                     pp[6], pp[7], *seq_args)
        od_s, wkv_s = _rwkv_seq_call(*_rwkv_chunk_call(*ps[:6]), Bs, Ts // RWKV_CHUNK, min(Bs, 4),
                                     _pair_states(wkv0[l]), ps[6], ps[7], *seq_args)

        gain_next = g1('ln_ffn2')
        x, h = _outproj_call(x, (oa_p, ob_p, oc_p, od_p), (oa_s, ob_s, oc_s, od_s), wbf['w_out'], gain_next, l)
        gain_next = W['ln_ffn1'][l + 1][None] if l + 1 < depth else g1('ln_ffn2')
        x, h = _ffn_call(x, h, wbf['w_ffn2_gate'], wbf['w_ffn2_up'], wbf['w_ffn2_down'], gain_next, l)

        nh = GW // HD
        outs['ks'].append(kn_s.reshape(Bs, Ts, nh, HD))
        outs['vs'].append(z[Np:, 5 * GW:6 * GW].reshape(Bs, Ts, nh, HD))
        outs['s5rp'].append(s5rp.reshape(Bp, 32, -1))
        outs['s5ip'].append(s5ip.reshape(Bp, 32, -1))
        outs['s5rs'].append(s5rs.reshape(Bs, 32, -1))
        outs['s5is'].append(s5is.reshape(Bs, 32, -1))
        outs['wkvp'].append(_unpair_states(wkv_p))
        outs['wkvs'].append(_unpair_states(wkv_s))
        zd = z[:, z.shape[1] - dcols:]
        outs['shp'].append(zd[:Np].reshape(Bp, Tp, dcols)[:, -1])
        outs['shs'].append(zd[Np:].reshape(Bs, Ts, dcols)[:, -1])
        outs['gv'].append(gv_s.reshape(Bs, Ts, GW))

    st = lambda k: jnp.stack(outs[k], axis=0)
    untr = lambda a: jnp.transpose(a.reshape(depth, Bp, GW // HD, HD, Tp), (0, 1, 4, 2, 3))
    return (x[:Np].reshape(Bp, Tp, D), x[Np:].reshape(Bs, Ts, D),
            untr(kt_buf), untr(vt_buf), st('ks'), st('vs'),
            st('s5rp'), st('s5ip'), st('s5rs'), st('s5is'),
            st('wkvp'), st('wkvs'), st('shp'), st('shs'), st('gv'))


_WEIGHT_NAMES = ('ln_ffn1', 'w_ffn1_gate', 'w_ffn1_up', 'w_ffn1_down', 'ln_mix', 'w_in',
                 'gmlp_v_norm', 'gmlp_ws', 'gmlp_b', 'out_norm_a',
                 's5_lam_re', 's5_lam_im', 's5_log_dt', 's5_b_re', 's5_b_im', 's5_c_re', 's5_c_im', 's5_d',
                 's5_w_glu', 's5_b_glu', 'out_norm_b',
                 'sb_q_norm', 'sb_k_norm', 'out_norm_c',
                 'rwkv_mu', 'rwkv_w0', 'rwkv_w2', 'rwkv_a0', 'rwkv_a2', 'rwkv_g2', 'rwkv_k_k', 'rwkv_k_a', 'rwkv_r_k',
                 'rwkv_lnx_w', 'rwkv_lnx_b',
                 'w_out', 'ln_ffn2', 'w_ffn2_gate', 'w_ffn2_up', 'w_ffn2_down')


def kernel(x_prompt, x_sample, cache_sb_k, cache_sb_v, state_s5_re, state_s5_im, state_rwkv_wkv, state_rwkv_shift,
           ln_ffn1, w_ffn1_gate, w_ffn1_up, w_ffn1_down, ln_mix, w_in,
           gmlp_v_norm, gmlp_ws, gmlp_b, out_norm_a,
           s5_lam_re, s5_lam_im, s5_log_dt, s5_b_re, s5_b_im, s5_c_re, s5_c_im, s5_d, s5_w_glu, s5_b_glu, out_norm_b,
           sb_q_norm, sb_k_norm, out_norm_c,
           rwkv_mu, rwkv_w0, rwkv_w2, rwkv_a0, rwkv_a2, rwkv_g2, rwkv_k_k, rwkv_k_a, rwkv_r_k, rwkv_lnx_w, rwkv_lnx_b,
           w_out, ln_ffn2, w_ffn2_gate, w_ffn2_up, w_ffn2_down):
    weights = (ln_ffn1, w_ffn1_gate, w_ffn1_up, w_ffn1_down, ln_mix, w_in,
               gmlp_v_norm, gmlp_ws, gmlp_b, out_norm_a,
               s5_lam_re, s5_lam_im, s5_log_dt, s5_b_re, s5_b_im, s5_c_re, s5_c_im, s5_d, s5_w_glu, s5_b_glu,
               out_norm_b, sb_q_norm, sb_k_norm, out_norm_c,
               rwkv_mu, rwkv_w0, rwkv_w2, rwkv_a0, rwkv_a2, rwkv_g2, rwkv_k_k, rwkv_k_a, rwkv_r_k, rwkv_lnx_w,
               rwkv_lnx_b, w_out, ln_ffn2, w_ffn2_gate, w_ffn2_up, w_ffn2_down)
    W = dict(zip(_WEIGHT_NAMES, weights))
    return _forward(x_prompt, x_sample, cache_sb_k, cache_sb_v, state_s5_re, state_s5_im, state_rwkv_wkv,
                    state_rwkv_shift, W)
```
